```python
import jax, jax.numpy as jnp
from jax import lax
import numpy as np

D_MODEL = 1024
BATCH = 8
SEQ = 2048
DEPTH = 4

N_HEADS = 8
HEAD_DIM = D_MODEL // N_HEADS
MOBA_BLOCK = 256
MOBA_TOPK = 3
Q_CHUNK = 16
NEG_INF = -1e30
LRU_WIDTH = D_MODEL
LRU_BLOCKS = 8
LRU_BLOCK_DIM = LRU_WIDTH // LRU_BLOCKS
CONV_WIDTH = 4
LRU_C = 8.0
FFN_HIDDEN = -(-8 * D_MODEL // (3 * 256)) * 256
DN_ALPHA = (2 * DEPTH) ** 0.25
DN_BETA = (8 * DEPTH) ** -0.25
LN_EPS = 1e-5
N_ATTN_LAYERS = (DEPTH + 1) // 2
N_LRU_LAYERS = DEPTH // 2

kernel_name = "moba_rglru_deepnorm_hybrid"


def layer_norm(x, g, b):
    xf = x.astype(jnp.float32)
    mu = jnp.mean(xf, axis=-1, keepdims=True)
    var = jnp.mean(jnp.square(xf - mu), axis=-1, keepdims=True)
    return ((xf - mu) * lax.rsqrt(var + LN_EPS) * g.astype(jnp.float32) + b.astype(jnp.float32)).astype(x.dtype)


def alibi_slopes(n_heads):
    return jnp.exp2(-8.0 * (jnp.arange(n_heads, dtype=jnp.float32) + 1.0) / n_heads)


def moba_attention(x, w_qkv, w_o):
    B, S, _ = x.shape
    f32 = jnp.float32
    qkv = x @ w_qkv
    q, k, v = jnp.split(qkv, 3, axis=-1)
    to_heads = lambda t: t.reshape(B, S, N_HEADS, HEAD_DIM).transpose(0, 2, 1, 3)
    q, k, v = to_heads(q), to_heads(k), to_heads(v)

    n_blk = -(-S // MOBA_BLOCK)
    pad = n_blk * MOBA_BLOCK - S
    kb = jnp.pad(k, ((0, 0), (0, 0), (0, pad), (0, 0))).reshape(B, N_HEADS, n_blk, MOBA_BLOCK, HEAD_DIM)
    vb = jnp.pad(v, ((0, 0), (0, 0), (0, pad), (0, 0))).reshape(B, N_HEADS, n_blk, MOBA_BLOCK, HEAD_DIM)

    kmean = jnp.mean(kb.astype(f32), axis=3)
    pos = jnp.arange(S)
    q_blk = pos // MOBA_BLOCK
    gate = jnp.einsum('bhsd,bhnd->bhsn', q.astype(f32), kmean)
    past = jnp.arange(n_blk)[None, :] < q_blk[:, None]
    gate = jnp.where(past[None, None], gate, -jnp.inf)
    k_sel = min(MOBA_TOPK, n_blk)
    _, sel = lax.top_k(gate, k_sel)
    sel_valid = sel < q_blk[None, None, :, None]

    scale = HEAD_DIM ** -0.5
    slopes = alibi_slopes(N_HEADS)
    b_ix = jnp.arange(B)[:, None, None, None]
    h_ix = jnp.arange(N_HEADS)[None, :, None, None]
    offs = jnp.arange(MOBA_BLOCK)
    n_chunks = S // Q_CHUNK

    def chunk(c):
        t0 = c * Q_CHUNK
        qc = lax.dynamic_slice_in_dim(q, t0, Q_CHUNK, axis=2).astype(f32)
        sc = lax.dynamic_slice_in_dim(sel, t0, Q_CHUNK, axis=2)
        vc = lax.dynamic_slice_in_dim(sel_valid, t0, Q_CHUNK, axis=2)
        t = t0 + jnp.arange(Q_CHUNK)
        kg = kb[b_ix, h_ix, sc].astype(f32)
        vg = vb[b_ix, h_ix, sc].astype(f32)
        s_sel = jnp.einsum('bhqd,bhqnkd->bhqnk', qc, kg) * scale
        key_pos = sc[..., None] * MOBA_BLOCK + offs
        dist_sel = (t[None, None, :, None, None] - key_pos).astype(f32)
        s_sel = s_sel - slopes[None, :, None, None, None] * dist_sel
        s_sel = jnp.where(vc[..., None], s_sel, NEG_INF)
        own = t0 // MOBA_BLOCK
        ko = lax.dynamic_index_in_dim(kb, own, axis=2, keepdims=False).astype(f32)
        vo = lax.dynamic_index_in_dim(vb, own, axis=2, keepdims=False).astype(f32)
        dist_own = t[:, None] - (own * MOBA_BLOCK + offs)[None, :]
        s_own = jnp.einsum('bhqd,bhkd->bhqk', qc, ko) * scale
        s_own = jnp.where(dist_own[None, None] >= 0,
                          s_own - slopes[None, :, None, None] * dist_own.astype(f32)[None, None],
                          NEG_INF)
        scores = jnp.concatenate([s_sel.reshape(B, N_HEADS, Q_CHUNK, k_sel * MOBA_BLOCK), s_own], axis=-1)
        p = jax.nn.softmax(scores, axis=-1)
        p_sel = p[..., :k_sel * MOBA_BLOCK].reshape(B, N_HEADS, Q_CHUNK, k_sel, MOBA_BLOCK)
        p_own = p[..., k_sel * MOBA_BLOCK:]
        out = jnp.einsum('bhqnk,bhqnkd->bhqd', p_sel, vg) + jnp.einsum('bhqk,bhkd->bhqd', p_own, vo)
        return out.astype(x.dtype)

    o = lax.map(chunk, jnp.arange(n_chunks))
    o = o.transpose(1, 2, 0, 3, 4).reshape(B, N_HEADS, S, HEAD_DIM)
    o = o.transpose(0, 2, 1, 3).reshape(B, S, N_HEADS * HEAD_DIM)
    return o @ w_o


def rglru_block(x, w_in, conv_w, conv_b, w_a, b_a, w_x, b_x, lam, w_out):
    B, S, _ = x.shape
    f32 = jnp.float32
    xb, yb = jnp.split(x @ w_in, 2, axis=-1)
    gate_branch = jax.nn.gelu(yb)
    xp = jnp.pad(xb, ((0, 0), (CONV_WIDTH - 1, 0), (0, 0)))
    xc = sum(xp[:, tap:tap + S] * conv_w[tap] for tap in range(CONV_WIDTH)) + conv_b
    xg = xc.reshape(B, S, LRU_BLOCKS, LRU_BLOCK_DIM)
    r = jax.nn.sigmoid(jnp.einsum('bsgc,gcd->bsgd', xg, w_a).reshape(B, S, LRU_WIDTH) + b_a)
    i = jax.nn.sigmoid(jnp.einsum('bsgc,gcd->bsgd', xg, w_x).reshape(B, S, LRU_WIDTH) + b_x)
    log_a = -LRU_C * r.astype(f32) * jax.nn.softplus(-lam.astype(f32))
    a = jnp.exp(log_a)
    b_in = jnp.sqrt(-jnp.expm1(2.0 * log_a)) * (i * xc).astype(f32)

    def combine(left, right):
        a1, b1 = left
        a2, b2 = right
        return a1 * a2, a2 * b1 + b2

    _, h = lax.associative_scan(combine, (a, b_in), axis=1)
    return (h.astype(x.dtype) * gate_branch) @ w_out


def swiglu(x, w_in, w_out):
    g, u = jnp.split(x @ w_in, 2, axis=-1)
    return (jax.nn.silu(g) * u) @ w_out


def setup_inputs(seed: int = 0) -> dict:
    key = jax.random.key(seed)
    ks = jax.random.split(key, 20)
    f32 = jnp.float32
    D, DR, F, G, DG = D_MODEL, LRU_WIDTH, FFN_HIDDEN, LRU_BLOCKS, LRU_BLOCK_DIM
    nA, nR = N_ATTN_LAYERS, N_LRU_LAYERS
    nrm = lambda k, shape, s: jax.random.normal(k, shape, f32) * s
    x = nrm(ks[0], (BATCH, SEQ, D), 1.0)
    attn_w_qkv = nrm(ks[1], (nA, D, 3 * D), D ** -0.5)
    attn_w_o = nrm(ks[2], (nA, D, D), D ** -0.5 * DN_BETA)
    lru_w_in = nrm(ks[3], (nR, D, 2 * DR), D ** -0.5)
    lru_conv_w = nrm(ks[4], (nR, CONV_WIDTH, DR), CONV_WIDTH ** -0.5)
    lru_conv_b = nrm(ks[5], (nR, DR), 0.01)
    lru_w_a = nrm(ks[6], (nR, G, DG, DG), DG ** -0.5)
    lru_b_a = nrm(ks[7], (nR, DR), 0.01)
    lru_w_x = nrm(ks[8], (nR, G, DG, DG), DG ** -0.5)
    lru_b_x = nrm(ks[9], (nR, DR), 0.01)
    u = jax.random.uniform(ks[10], (nR, DR), f32, 0.9, 0.999)
    p = u ** (1.0 / LRU_C)
    lru_lambda = jnp.log(p) - jnp.log1p(-p)
    lru_w_out = nrm(ks[11], (nR, DR, D), DR ** -0.5 * DN_BETA)
    ffn_w_in = nrm(ks[12], (DEPTH, D, 2 * F), D ** -0.5)
    ffn_w_out = nrm(ks[13], (DEPTH, F, D), F ** -0.5 * DN_BETA)
    ln_g = 1.0 + nrm(ks[14], (DEPTH, 2, D), 0.02)
    ln_b = nrm(ks[15], (DEPTH, 2, D), 0.02)
    return {"x": x, "attn_w_qkv": attn_w_qkv, "attn_w_o": attn_w_o,
            "lru_w_in": lru_w_in, "lru_conv_w": lru_conv_w, "lru_conv_b": lru_conv_b,
            "lru_w_a": lru_w_a, "lru_b_a": lru_b_a, "lru_w_x": lru_w_x, "lru_b_x": lru_b_x,
            "lru_lambda": lru_lambda, "lru_w_out": lru_w_out,
            "ffn_w_in": ffn_w_in, "ffn_w_out": ffn_w_out, "ln_g": ln_g, "ln_b": ln_b}


def reference(x, attn_w_qkv, attn_w_o, lru_w_in, lru_conv_w, lru_conv_b, lru_w_a, lru_b_a,
              lru_w_x, lru_b_x, lru_lambda, lru_w_out, ffn_w_in, ffn_w_out, ln_g, ln_b):
    h = x
    for i in range(DEPTH):
        j = i // 2
        if i % 2 == 0:
            mix = moba_attention(h, attn_w_qkv[j], attn_w_o[j])
        else:
            mix = rglru_block(h, lru_w_in[j], lru_conv_w[j], lru_conv_b[j], lru_w_a[j], lru_b_a[j],
                              lru_w_x[j], lru_b_x[j], lru_lambda[j], lru_w_out[j])
        h = layer_norm(DN_ALPHA * h + mix, ln_g[i, 0], ln_b[i, 0])
        h = layer_norm(DN_ALPHA * h + swiglu(h, ffn_w_in[i], ffn_w_out[i]), ln_g[i, 1], ln_b[i, 1])
    return h
```

```python
import functools

import jax
import jax.numpy as jnp
from jax import lax
from jax.experimental import pallas as pl
from jax.experimental.pallas import tpu as pltpu

N_HEADS = 8
MOBA_BLOCK = 256
MOBA_TOPK = 3
NEG_INF = -1e30
LRU_BLOCKS = 8
CONV_WIDTH = 4
LRU_C = 8.0
LN_EPS = 1e-5

V7X_SUBLANES = 8
V7X_LANES = 128
V7X_VMEM_LIMIT_BYTES = 56 * 1024 * 1024

ROW_TILE = 512
LRU_SEQ_TILE = 256
FFN_CHUNK = 256

_F32 = jnp.float32
_BF16 = jnp.bfloat16
_NT = (((1,), (1,)), ((), ()))


def _compiler_params(semantics):
    return pltpu.CompilerParams(dimension_semantics=semantics,
                                vmem_limit_bytes=V7X_VMEM_LIMIT_BYTES)


def _resident(shape):
    zeros = (0,) * len(shape)
    return pl.BlockSpec(shape, lambda *_: zeros, pipeline_mode=pl.Buffered(1))


def _layer_norm(y, g, b):
    mu = jnp.mean(y, axis=-1, keepdims=True)
    yc = y - mu
    var = jnp.mean(yc * yc, axis=-1, keepdims=True)
    return yc * lax.rsqrt(var + LN_EPS) * g + b


def _dot(a, b):
    return jnp.dot(a, b, preferred_element_type=_F32)


def _qkv_kernel(x_ref, w_ref, qkv_ref, kmean_ref, *, d_model, scale):
    xb = x_ref[...].astype(_BF16)
    rows = xb.shape[0]
    col_chunk = 512
    for c in range(0, 3 * d_model, col_chunk):
        acc = _dot(xb, w_ref[:, c:c + col_chunk])
        if c < d_model:
            acc = acc * scale
        elif c < 2 * d_model:
            for r in range(rows // MOBA_BLOCK):
                blk = acc[r * MOBA_BLOCK:(r + 1) * MOBA_BLOCK, :]
                kmean_ref[0, r:r + 1, c - d_model:c - d_model + col_chunk] = (
                    jnp.sum(blk, axis=0, keepdims=True) * (1.0 / MOBA_BLOCK))
        qkv_ref[:, c:c + col_chunk] = acc.astype(_BF16)


def _qkv_projection(h, w_qkv, scale):
    m, d = h.shape
    blocks_per_tile = ROW_TILE // MOBA_BLOCK
    return pl.pallas_call(
        functools.partial(_qkv_kernel, d_model=d, scale=scale),
        grid=(m // ROW_TILE,),
        in_specs=[pl.BlockSpec((ROW_TILE, d), lambda i: (i, 0)),
                  _resident((d, 3 * d))],
        out_specs=[pl.BlockSpec((ROW_TILE, 3 * d), lambda i: (i, 0)),
                   pl.BlockSpec((1, blocks_per_tile, d), lambda i: (i, 0, 0))],
        out_shape=[jax.ShapeDtypeStruct((m, 3 * d), _BF16),
                   jax.ShapeDtypeStruct((m // ROW_TILE, blocks_per_tile, d), _F32)],
        compiler_params=_compiler_params(("arbitrary",)),
        name="qkv_projection",
    )(h, w_qkv)


def _moba_kernel(slopes_ref, q_ref, k_ref, v_ref, kmean_ref, o_ref, *, n_blk):
    head = pl.program_id(1)
    qi = pl.program_id(2)
    slope = slopes_ref[head]
    q = q_ref[...]
    km = kmean_ref[...]
    km_hi = km.astype(_BF16)
    km_lo = (km - km_hi.astype(_F32)).astype(_BF16)
    gate = (lax.dot_general(q, km_hi, _NT, preferred_element_type=_F32)
            + lax.dot_general(q, km_lo, _NT, preferred_element_type=_F32))

    blk_id = lax.broadcasted_iota(jnp.int32, gate.shape, 1).astype(_F32)
    qi_f = qi.astype(_F32)
    g = jnp.where(blk_id < qi_f, gate, -jnp.inf)
    picks = []
    for _ in range(min(MOBA_TOPK, n_blk)):
        best = jnp.max(g, axis=-1, keepdims=True)
        idx = jnp.min(jnp.where(g == best, blk_id, float(n_blk)), axis=-1, keepdims=True)
        picks.append(jnp.where(best > -jnp.inf, idx, -1.0))
        g = jnp.where(blk_id == idx, -jnp.inf, g)

    row = lax.broadcasted_iota(jnp.int32, (MOBA_BLOCK, MOBA_BLOCK), 0)
    col = lax.broadcasted_iota(jnp.int32, (MOBA_BLOCK, MOBA_BLOCK), 1)
    local_bias = (col - row).astype(_F32) * slope
    causal = jnp.where(col > row, NEG_INF, 0.0)

    s = lax.dot_general(q, k_ref[...], _NT, preferred_element_type=_F32)
    pieces = []
    for n in range(n_blk):
        picked = picks[0] == float(n)
        for p in picks[1:]:
            picked = picked | (p == float(n))
        block_off = (qi - n).astype(_F32) * (-float(MOBA_BLOCK) * slope)
        is_own = qi == n
        col_bias = jnp.where(picked, block_off, jnp.where(is_own, 0.0, NEG_INF))
        s_n = s[:, n * MOBA_BLOCK:(n + 1) * MOBA_BLOCK] + local_bias + col_bias
        pieces.append(s_n + jnp.where(is_own, causal, 0.0))
    s = jnp.concatenate(pieces, axis=-1)
    m = jnp.max(s, axis=-1, keepdims=True)
    p = jnp.exp(s - m)
    denom = jnp.sum(p, axis=-1, keepdims=True)
    out = _dot(p.astype(_BF16), v_ref[...])
    o_ref[...] = (out / denom).astype(o_ref.dtype)


def _moba_attention(qkv, kmean, slopes, batch, seq):
    m, d3 = qkv.shape
    d = d3 // 3
    dh = d // N_HEADS
    n_blk = seq // MOBA_BLOCK
    return pl.pallas_call(
        functools.partial(_moba_kernel, n_blk=n_blk),
        grid=(batch, N_HEADS, n_blk),
        in_specs=[pl.BlockSpec(memory_space=pltpu.SMEM),
                  pl.BlockSpec((MOBA_BLOCK, dh), lambda b, h, i: (b * n_blk + i, h)),
                  pl.BlockSpec((seq, dh), lambda b, h, i: (b, N_HEADS + h)),
                  pl.BlockSpec((seq, dh), lambda b, h, i: (b, 2 * N_HEADS + h)),
                  pl.BlockSpec((None, n_blk, dh), lambda b, h, i: (b, 0, h))],
        out_specs=pl.BlockSpec((MOBA_BLOCK, dh), lambda b, h, i: (b * n_blk + i, h)),
        out_shape=jax.ShapeDtypeStruct((m, d), _BF16),
        compiler_params=_compiler_params(("arbitrary", "arbitrary", "arbitrary")),
        name="moba_attention",
    )(slopes, qkv, qkv, qkv, kmean)


def _proj_ln_kernel(a_ref, w_ref, res_ref, g_ref, b_ref, out_ref, *, alpha):
    y = _dot(a_ref[...], w_ref[...]) + alpha * res_ref[...]
    out_ref[...] = _layer_norm(y, g_ref[...], b_ref[...])


def _proj_residual_ln(a, w, res, g, b, alpha):
    m, d = res.shape
    k = a.shape[1]
    row_spec = lambda width: pl.BlockSpec((ROW_TILE, width), lambda i: (i, 0))
    return pl.pallas_call(
        functools.partial(_proj_ln_kernel, alpha=alpha),
        grid=(m // ROW_TILE,),
        in_specs=[row_spec(k), _resident((k, d)), row_spec(d),
                  _resident((1, d)), _resident((1, d))],
        out_specs=row_spec(d),
        out_shape=jax.ShapeDtypeStruct((m, d), _F32),
        compiler_params=_compiler_params(("arbitrary",)),
        name="out_projection_ln",
    )(a, w, res, g, b)


def _ffn_kernel(x_ref, w_in_ref, w_out_ref, g_ref, b_ref, out_ref, *, hidden, alpha):
    x = x_ref[...]
    xb = x.astype(_BF16)
    acc = alpha * x
    for c in range(0, hidden, FFN_CHUNK):
        gate = _dot(xb, w_in_ref[:, c:c + FFN_CHUNK])
        up = _dot(xb, w_in_ref[:, hidden + c:hidden + c + FFN_CHUNK])
        act = (gate * jax.nn.sigmoid(gate) * up).astype(_BF16)
        acc = acc + _dot(act, w_out_ref[c:c + FFN_CHUNK, :])
    out_ref[...] = _layer_norm(acc, g_ref[...], b_ref[...])


def _ffn_layer(h, w_in, w_out, g, b, alpha):
    m, d = h.shape
    hidden = w_out.shape[0]
    row_spec = pl.BlockSpec((ROW_TILE, d), lambda i: (i, 0))
    return pl.pallas_call(
        functools.partial(_ffn_kernel, hidden=hidden, alpha=alpha),
        grid=(m // ROW_TILE,),
        in_specs=[row_spec, _resident((d, 2 * hidden)), _resident((hidden, d)),
                  _resident((1, d)), _resident((1, d))],
        out_specs=row_spec,
        out_shape=jax.ShapeDtypeStruct((m, d), _F32),
        compiler_params=_compiler_params(("arbitrary",)),
        name="swiglu_ffn_ln",
    )(h, w_in, w_out, g, b)


def _gelu_tanh(x):
    inner = 0.7978845608028654 * (x + 0.044715 * (x * x * x))
    return 0.5 * x * (1.0 + jnp.tanh(inner))


def _softplus(z):
    return jnp.maximum(z, 0.0) + jnp.log1p(jnp.exp(-jnp.abs(z)))


def _lru_kernel(x_ref, w_in_ref, conv_w_ref, conv_b_ref, w_gate_ref, b_a_ref, b_x_ref,
                lam_ref, w_out_ref, g_ref, b_ref, out_ref,
                xbuf, a_s, b_s, h_s, ac_s, gated_s, carry, *, alpha):
    lane_groups, seq_tile, lanes = a_s.shape
    width = lane_groups * lanes
    chunk = seq_tile // V7X_SUBLANES
    pad = V7X_SUBLANES
    lane_slice = lambda gi: slice(gi * lanes, (gi + 1) * lanes)

    @pl.when(pl.program_id(1) == 0)
    def _():
        xbuf[0:pad, :] = jnp.zeros((pad, width), _F32)
        carry[...] = jnp.zeros_like(carry)

    x = x_ref[...]
    xb = x.astype(_BF16)
    xbuf[pad:pad + seq_tile, :] = _dot(xb, w_in_ref[:, 0:width])
    gate_branch = _gelu_tanh(_dot(xb, w_in_ref[:, width:2 * width]))

    xc = conv_b_ref[...]
    for tap in range(CONV_WIDTH):
        back = CONV_WIDTH - 1 - tap
        xc = xc + xbuf[pad - back:pad - back + seq_tile, :] * conv_w_ref[tap:tap + 1, :]
    xbuf[0:pad, :] = xbuf[seq_tile:seq_tile + pad, :]

    xcb = xc.astype(_BF16)
    group = width // LRU_BLOCKS
    pre_a, pre_x = [], []
    for gidx in range(LRU_BLOCKS):
        both = _dot(xcb[:, gidx * group:(gidx + 1) * group], w_gate_ref[gidx])
        pre_a.append(both[:, 0:group])
        pre_x.append(both[:, group:2 * group])
    r = jax.nn.sigmoid(jnp.concatenate(pre_a, axis=-1) + b_a_ref[...])
    i = jax.nn.sigmoid(jnp.concatenate(pre_x, axis=-1) + b_x_ref[...])
    log_a = (-LRU_C) * r * _softplus(-lam_ref[...])
    a = jnp.exp(log_a)
    b_in = jnp.sqrt(1.0 - a * a) * (i * xc)
    for gi in range(lane_groups):
        a_s[gi] = a[:, lane_slice(gi)]
        b_s[gi] = b_in[:, lane_slice(gi)]

    for gi in range(lane_groups):
        h_loc = jnp.zeros((V7X_SUBLANES, lanes), _F32)
        a_cum = jnp.ones((V7X_SUBLANES, lanes), _F32)
        for j in range(chunk):
            rows = pl.ds(j, V7X_SUBLANES, stride=chunk)
            a_j = a_s[gi, rows, :]
            h_loc = a_j * h_loc + b_s[gi, rows, :]
            a_cum = a_j * a_cum
            h_s[gi, rows, :] = h_loc
            ac_s[gi, rows, :] = a_cum

        state = carry[:, lane_slice(gi)]
        for c in range(V7X_SUBLANES):
            rows = slice(c * chunk, (c + 1) * chunk)
            h_rows = h_s[gi, rows, :] + ac_s[gi, rows, :] * state
            gated_s[rows, lane_slice(gi)] = (
                h_rows * gate_branch[rows, lane_slice(gi)]).astype(_BF16)
            state = h_loc[c:c + 1, :] + a_cum[c:c + 1, :] * state
        carry[:, lane_slice(gi)] = state

    y = _dot(gated_s[...], w_out_ref[...]) + alpha * x
    out_ref[...] = _layer_norm(y, g_ref[...], b_ref[...])


def _lru_layer(h, batch, seq, w_in, conv_w, conv_b, w_gate, b_a, b_x, lam, w_out, g, b, alpha):
    m, d = h.shape
    width = w_out.shape[0]
    tiles = seq // LRU_SEQ_TILE
    scan_shape = (width // V7X_LANES, LRU_SEQ_TILE, V7X_LANES)
    row_spec = pl.BlockSpec((LRU_SEQ_TILE, d), lambda bi, ti: (bi * tiles + ti, 0))
    return pl.pallas_call(
        functools.partial(_lru_kernel, alpha=alpha),
        grid=(batch, tiles),
        in_specs=[row_spec, _resident(w_in.shape), _resident(conv_w.shape),
                  _resident(conv_b.shape), _resident(w_gate.shape), _resident(b_a.shape),
                  _resident(b_x.shape), _resident(lam.shape), _resident(w_out.shape),
                  _resident(g.shape), _resident(b.shape)],
        out_specs=row_spec,
        out_shape=jax.ShapeDtypeStruct((m, d), _F32),
        scratch_shapes=[pltpu.VMEM((LRU_SEQ_TILE + V7X_SUBLANES, width), _F32),
                        pltpu.VMEM(scan_shape, _F32),
                        pltpu.VMEM(scan_shape, _F32),
                        pltpu.VMEM(scan_shape, _F32),
                        pltpu.VMEM(scan_shape, _F32),
                        pltpu.VMEM((LRU_SEQ_TILE, width), _BF16),
                        pltpu.VMEM((1, width), _F32)],
        compiler_params=_compiler_params(("arbitrary", "arbitrary")),
        name="rglru_block_ln",
    )(h, w_in, conv_w, conv_b, w_gate, b_a, b_x, lam, w_out, g, b)


def kernel(x, attn_w_qkv, attn_w_o, lru_w_in, lru_conv_w, lru_conv_b, lru_w_a, lru_b_a,
           lru_w_x, lru_b_x, lru_lambda, lru_w_out, ffn_w_in, ffn_w_out, ln_g, ln_b):
    batch, seq, d = x.shape
    depth = ffn_w_in.shape[0]
    assert seq % MOBA_BLOCK == 0 and seq % LRU_SEQ_TILE == 0 and (batch * seq) % ROW_TILE == 0
    assert d % N_HEADS == 0 and ROW_TILE % MOBA_BLOCK == 0 and seq % ROW_TILE == 0
    alpha = (2 * depth) ** 0.25
    head_dim = d // N_HEADS
    n_blk = seq // MOBA_BLOCK
    slopes = jnp.exp2(-8.0 * (jnp.arange(N_HEADS, dtype=_F32) + 1.0) / N_HEADS)
    row = lambda v: v.reshape(1, -1)

    h = x.reshape(batch * seq, d)
    for layer in range(depth):
        j = layer // 2
        g0, b0 = row(ln_g[layer, 0]), row(ln_b[layer, 0])
        g1, b1 = row(ln_g[layer, 1]), row(ln_b[layer, 1])
        if layer % 2 == 0:
            qkv, kmean = _qkv_projection(h, attn_w_qkv[j].astype(_BF16), head_dim ** -0.5)
            o = _moba_attention(qkv, kmean.reshape(batch, n_blk, d), slopes, batch, seq)
            h = _proj_residual_ln(o, attn_w_o[j].astype(_BF16), h, g0, b0, alpha)
        else:
            w_gate = jnp.concatenate([lru_w_a[j], lru_w_x[j]], axis=-1).astype(_BF16)
            h = _lru_layer(h, batch, seq, lru_w_in[j].astype(_BF16), lru_conv_w[j],
                           row(lru_conv_b[j]), w_gate, row(lru_b_a[j]), row(lru_b_x[j]),
                           row(lru_lambda[j]), lru_w_out[j].astype(_BF16), g0, b0, alpha)
        h = _ffn_layer(h, ffn_w_in[layer].astype(_BF16), ffn_w_out[layer].astype(_BF16),
                       g1, b1, alpha)
    return h.reshape(batch, seq, d)
```

```python
import functools

import jax
import jax.numpy as jnp
from jax import lax
from jax.experimental import pallas as pl
from jax.experimental.pallas import tpu as pltpu

N_HEADS = 8
MOBA_BLOCK = 256
MOBA_TOPK = 3
NEG_INF = -1e30
LRU_BLOCKS = 8
CONV_WIDTH = 4
LRU_C = 8.0
LN_EPS = 1e-5
LOG2_E = 1.4426950408889634

V7X_SUBLANES = 8
V7X_LANES = 128
V7X_VMEM_LIMIT_BYTES = 56 * 1024 * 1024

ROW_TILE = 512
LRU_SEQ_TILE = 256
FFN_CHUNK = 256

_F32 = jnp.float32
_BF16 = jnp.bfloat16
_NT = (((1,), (1,)), ((), ()))


def _compiler_params(semantics):
    return pltpu.CompilerParams(dimension_semantics=semantics,
                                vmem_limit_bytes=V7X_VMEM_LIMIT_BYTES)


def _resident(shape):
    zeros = (0,) * len(shape)
    return pl.BlockSpec(shape, lambda *_: zeros, pipeline_mode=pl.Buffered(1))


def _layer_norm(y, g, b):
    mu = jnp.mean(y, axis=-1, keepdims=True)
    yc = y - mu
    var = jnp.mean(yc * yc, axis=-1, keepdims=True)
    return yc * lax.rsqrt(var + LN_EPS) * g + b


def _dot(a, b):
    return jnp.dot(a, b, preferred_element_type=_F32)


def _qkv_kernel(x_ref, w_ref, qkv_ref, kmean_ref, *, d_model, scale):
    xb = x_ref[...].astype(_BF16)
    rows = xb.shape[0]
    col_chunk = 512
    for c in range(0, 3 * d_model, col_chunk):
        acc = _dot(xb, w_ref[:, c:c + col_chunk])
        if c < d_model:
            acc = acc * scale
        elif c < 2 * d_model:
            for r in range(rows // MOBA_BLOCK):
                blk = acc[r * MOBA_BLOCK:(r + 1) * MOBA_BLOCK, :]
                kmean_ref[0, r:r + 1, c - d_model:c - d_model + col_chunk] = (
                    jnp.sum(blk, axis=0, keepdims=True) * (1.0 / MOBA_BLOCK))
        qkv_ref[:, c:c + col_chunk] = acc.astype(_BF16)


def _qkv_projection(h, w_qkv, scale):
    m, d = h.shape
    blocks_per_tile = ROW_TILE // MOBA_BLOCK
    return pl.pallas_call(
        functools.partial(_qkv_kernel, d_model=d, scale=scale),
        grid=(m // ROW_TILE,),
        in_specs=[pl.BlockSpec((ROW_TILE, d), lambda i: (i, 0)),
                  _resident((d, 3 * d))],
        out_specs=[pl.BlockSpec((ROW_TILE, 3 * d), lambda i: (i, 0)),
                   pl.BlockSpec((1, blocks_per_tile, d), lambda i: (i, 0, 0))],
        out_shape=[jax.ShapeDtypeStruct((m, 3 * d), _BF16),
                   jax.ShapeDtypeStruct((m // ROW_TILE, blocks_per_tile, d), _F32)],
        compiler_params=_compiler_params(("arbitrary",)),
        name="qkv_projection",
    )(h, w_qkv)


def _moba_kernel(slopes_ref, q_ref, k_ref, v_ref, kmean_ref, o_ref, s_scr, p_scr, *, n_blk):
    blk = MOBA_BLOCK
    slope2 = slopes_ref[pl.program_id(1)] * LOG2_E
    km = kmean_ref[...]
    km_hi = km.astype(_BF16)
    km_lo = (km - km_hi.astype(_F32)).astype(_BF16)
    key_bias = lax.broadcasted_iota(jnp.int32, (1, blk), 1).astype(_F32) * slope2
    row = lax.broadcasted_iota(jnp.int32, (blk, blk), 0)
    col = lax.broadcasted_iota(jnp.int32, (blk, blk), 1)

    def block_scores(n, own):
        t = s_scr[:, n * blk:(n + 1) * blk] + key_bias
        return jnp.where(col > row, NEG_INF, t) if own else t

    for qi in range(n_blk):
        width = (qi + 1) * blk
        q = q_ref[qi * blk:(qi + 1) * blk, :]
        s_scr[:, 0:width] = lax.dot_general(q, k_ref[0:width, :], _NT,
                                            preferred_element_type=_F32)

        offsets = [slope2 * float(-blk * (qi - n)) for n in range(qi)]
        if qi > MOBA_TOPK:
            gate = (lax.dot_general(q, km_hi, _NT, preferred_element_type=_F32)
                    + lax.dot_general(q, km_lo, _NT, preferred_element_type=_F32))
            g = [jnp.broadcast_to(gate[:, n:n + 1], (blk, V7X_LANES)) for n in range(qi)]
            losses = [jnp.zeros((blk, V7X_LANES), _F32) for _ in range(qi)]
            for n in range(qi):
                for m in range(n):
                    m_first = jnp.where(g[m] >= g[n], 1.0, 0.0)
                    losses[n] = losses[n] + m_first
                    losses[m] = losses[m] + (1.0 - m_first)
            row_term = [jnp.where(losses[n] < float(MOBA_TOPK), offsets[n], NEG_INF)
                        for n in range(qi)]
        else:
            row_term = offsets

        m_run = jnp.max(block_scores(qi, True), axis=-1, keepdims=True)
        for n in range(qi):
            m_run = jnp.maximum(m_run, jnp.max(block_scores(n, False), axis=-1,
                                               keepdims=True) + row_term[n])
        denom = None
        for n in range(qi + 1):
            shift = m_run if n == qi else m_run - row_term[n]
            if shift.shape[1] == V7X_LANES:
                shift = jnp.concatenate([shift] * (blk // V7X_LANES), axis=-1)
            p = jnp.exp2(block_scores(n, n == qi) - shift)
            part = jnp.sum(p, axis=-1, keepdims=True)
            denom = part if denom is None else denom + part
            p_scr[:, n * blk:(n + 1) * blk] = p.astype(_BF16)
        out = _dot(p_scr[:, 0:width], v_ref[0:width, :])
        o_ref[qi * blk:(qi + 1) * blk, :] = (out / denom).astype(o_ref.dtype)


def _moba_attention(qkv, kmean, slopes, batch, seq):
    m, d3 = qkv.shape
    d = d3 // 3
    dh = d // N_HEADS
    n_blk = seq // MOBA_BLOCK
    return pl.pallas_call(
        functools.partial(_moba_kernel, n_blk=n_blk),
        grid=(batch, N_HEADS),
        in_specs=[pl.BlockSpec(memory_space=pltpu.SMEM),
                  pl.BlockSpec((seq, dh), lambda b, h: (b, h)),
                  pl.BlockSpec((seq, dh), lambda b, h: (b, N_HEADS + h)),
                  pl.BlockSpec((seq, dh), lambda b, h: (b, 2 * N_HEADS + h)),
                  pl.BlockSpec((None, n_blk, dh), lambda b, h: (b, 0, h))],
        out_specs=pl.BlockSpec((seq, dh), lambda b, h: (b, h)),
        out_shape=jax.ShapeDtypeStruct((m, d), _BF16),
        scratch_shapes=[pltpu.VMEM((MOBA_BLOCK, seq), _F32),
                        pltpu.VMEM((MOBA_BLOCK, seq), _BF16)],
        compiler_params=_compiler_params(("arbitrary", "arbitrary")),
        name="moba_attention",
    )(slopes, qkv, qkv, qkv, kmean)


def _proj_ln_kernel(a_ref, w_ref, res_ref, g_ref, b_ref, out_ref, *, alpha):
    y = _dot(a_ref[...], w_ref[...]) + alpha * res_ref[...]
    out_ref[...] = _layer_norm(y, g_ref[...], b_ref[...])


def _proj_residual_ln(a, w, res, g, b, alpha):
    m, d = res.shape
    k = a.shape[1]
    row_spec = lambda width: pl.BlockSpec((ROW_TILE, width), lambda i: (i, 0))
    return pl.pallas_call(
        functools.partial(_proj_ln_kernel, alpha=alpha),
        grid=(m // ROW_TILE,),
        in_specs=[row_spec(k), _resident((k, d)), row_spec(d),
                  _resident((1, d)), _resident((1, d))],
        out_specs=row_spec(d),
        out_shape=jax.ShapeDtypeStruct((m, d), _F32),
        compiler_params=_compiler_params(("arbitrary",)),
        name="out_projection_ln",
    )(a, w, res, g, b)


def _ffn_kernel(x_ref, w_in_ref, w_out_ref, g_ref, b_ref, out_ref, *, hidden, alpha):
    x = x_ref[...]
    xb = x.astype(_BF16)
    acc = alpha * x
    for c in range(0, hidden, FFN_CHUNK):
        gate = _dot(xb, w_in_ref[:, c:c + FFN_CHUNK])
        up = _dot(xb, w_in_ref[:, hidden + c:hidden + c + FFN_CHUNK])
        act = (gate * jax.nn.sigmoid(gate) * up).astype(_BF16)
        acc = acc + _dot(act, w_out_ref[c:c + FFN_CHUNK, :])
    out_ref[...] = _layer_norm(acc, g_ref[...], b_ref[...])


def _ffn_layer(h, w_in, w_out, g, b, alpha):
    m, d = h.shape
    hidden = w_out.shape[0]
    row_spec = pl.BlockSpec((ROW_TILE, d), lambda i: (i, 0))
    return pl.pallas_call(
        functools.partial(_ffn_kernel, hidden=hidden, alpha=alpha),
        grid=(m // ROW_TILE,),
        in_specs=[row_spec, _resident((d, 2 * hidden)), _resident((hidden, d)),
                  _resident((1, d)), _resident((1, d))],
        out_specs=row_spec,
        out_shape=jax.ShapeDtypeStruct((m, d), _F32),
        compiler_params=_compiler_params(("arbitrary",)),
        name="swiglu_ffn_ln",
    )(h, w_in, w_out, g, b)


def _gelu_tanh(x):
    inner = 0.7978845608028654 * (x + 0.044715 * (x * x * x))
    return 0.5 * x * (1.0 + jnp.tanh(inner))


def _softplus(z):
    return jnp.maximum(z, 0.0) + jnp.log1p(jnp.exp(-jnp.abs(z)))


def _lru_kernel(x_ref, w_in_ref, conv_w_ref, conv_b_ref, w_gate_ref, b_a_ref, b_x_ref,
                lam_ref, w_out_ref, g_ref, b_ref, out_ref,
                xbuf, a_s, b_s, h_s, ac_s, gated_s, carry, *, alpha):
    lane_groups, seq_tile, lanes = a_s.shape
    width = lane_groups * lanes
    chunk = seq_tile // V7X_SUBLANES
    pad = V7X_SUBLANES
    lane_slice = lambda gi: slice(gi * lanes, (gi + 1) * lanes)

    @pl.when(pl.program_id(1) == 0)
    def _():
        xbuf[0:pad, :] = jnp.zeros((pad, width), _F32)
        carry[...] = jnp.zeros_like(carry)

    x = x_ref[...]
    xb = x.astype(_BF16)
    xbuf[pad:pad + seq_tile, :] = _dot(xb, w_in_ref[:, 0:width])
    gate_branch = _gelu_tanh(_dot(xb, w_in_ref[:, width:2 * width]))

    xc = conv_b_ref[...]
    for tap in range(CONV_WIDTH):
        back = CONV_WIDTH - 1 - tap
        xc = xc + xbuf[pad - back:pad - back + seq_tile, :] * conv_w_ref[tap:tap + 1, :]
    xbuf[0:pad, :] = xbuf[seq_tile:seq_tile + pad, :]

    xcb = xc.astype(_BF16)
    group = width // LRU_BLOCKS
    pre_a, pre_x = [], []
    for gidx in range(LRU_BLOCKS):
        both = _dot(xcb[:, gidx * group:(gidx + 1) * group], w_gate_ref[gidx])
        pre_a.append(both[:, 0:group])
        pre_x.append(both[:, group:2 * group])
    r = jax.nn.sigmoid(jnp.concatenate(pre_a, axis=-1) + b_a_ref[...])
    i = jax.nn.sigmoid(jnp.concatenate(pre_x, axis=-1) + b_x_ref[...])
    log_a = (-LRU_C) * r * _softplus(-lam_ref[...])
    a = jnp.exp(log_a)
    b_in = jnp.sqrt(1.0 - a * a) * (i * xc)
    for gi in range(lane_groups):
        a_s[gi] = a[:, lane_slice(gi)]
        b_s[gi] = b_in[:, lane_slice(gi)]

    for gi in range(lane_groups):
        h_loc = jnp.zeros((V7X_SUBLANES, lanes), _F32)
        a_cum = jnp.ones((V7X_SUBLANES, lanes), _F32)
        for j in range(chunk):
            rows = pl.ds(j, V7X_SUBLANES, stride=chunk)
            a_j = a_s[gi, rows, :]
            h_loc = a_j * h_loc + b_s[gi, rows, :]
            a_cum = a_j * a_cum
            h_s[gi, rows, :] = h_loc
            ac_s[gi, rows, :] = a_cum

        state = carry[:, lane_slice(gi)]
        for c in range(V7X_SUBLANES):
            rows = slice(c * chunk, (c + 1) * chunk)
            h_rows = h_s[gi, rows, :] + ac_s[gi, rows, :] * state
            gated_s[rows, lane_slice(gi)] = (
                h_rows * gate_branch[rows, lane_slice(gi)]).astype(_BF16)
            state = h_loc[c:c + 1, :] + a_cum[c:c + 1, :] * state
        carry[:, lane_slice(gi)] = state

    y = _dot(gated_s[...], w_out_ref[...]) + alpha * x
    out_ref[...] = _layer_norm(y, g_ref[...], b_ref[...])


def _lru_layer(h, batch, seq, w_in, conv_w, conv_b, w_gate, b_a, b_x, lam, w_out, g, b, alpha):
    m, d = h.shape
    width = w_out.shape[0]
    tiles = seq // LRU_SEQ_TILE
    scan_shape = (width // V7X_LANES, LRU_SEQ_TILE, V7X_LANES)
    row_spec = pl.BlockSpec((LRU_SEQ_TILE, d), lambda bi, ti: (bi * tiles + ti, 0))
    return pl.pallas_call(
        functools.partial(_lru_kernel, alpha=alpha),
        grid=(batch, tiles),
        in_specs=[row_spec, _resident(w_in.shape), _resident(conv_w.shape),
                  _resident(conv_b.shape), _resident(w_gate.shape), _resident(b_a.shape),
                  _resident(b_x.shape), _resident(lam.shape), _resident(w_out.shape),
                  _resident(g.shape), _resident(b.shape)],
        out_specs=row_spec,
        out_shape=jax.ShapeDtypeStruct((m, d), _F32),
        scratch_shapes=[pltpu.VMEM((LRU_SEQ_TILE + V7X_SUBLANES, width), _F32),
                        pltpu.VMEM(scan_shape, _F32),
                        pltpu.VMEM(scan_shape, _F32),
                        pltpu.VMEM(scan_shape, _F32),
                        pltpu.VMEM(scan_shape, _F32),
                        pltpu.VMEM((LRU_SEQ_TILE, width), _BF16),
                        pltpu.VMEM((1, width), _F32)],
        compiler_params=_compiler_params(("arbitrary", "arbitrary")),
        name="rglru_block_ln",
    )(h, w_in, conv_w, conv_b, w_gate, b_a, b_x, lam, w_out, g, b)


def kernel(x, attn_w_qkv, attn_w_o, lru_w_in, lru_conv_w, lru_conv_b, lru_w_a, lru_b_a,
           lru_w_x, lru_b_x, lru_lambda, lru_w_out, ffn_w_in, ffn_w_out, ln_g, ln_b):
    batch, seq, d = x.shape
    depth = ffn_w_in.shape[0]
    assert seq % MOBA_BLOCK == 0 and seq % LRU_SEQ_TILE == 0 and (batch * seq) % ROW_TILE == 0
    assert d % N_HEADS == 0 and ROW_TILE % MOBA_BLOCK == 0 and seq % ROW_TILE == 0
    alpha = (2 * depth) ** 0.25
    head_dim = d // N_HEADS
    n_blk = seq // MOBA_BLOCK
    slopes = jnp.exp2(-8.0 * (jnp.arange(N_HEADS, dtype=_F32) + 1.0) / N_HEADS)
    row = lambda v: v.reshape(1, -1)

    h = x.reshape(batch * seq, d)
    for layer in range(depth):
        j = layer // 2
        g0, b0 = row(ln_g[layer, 0]), row(ln_b[layer, 0])
        g1, b1 = row(ln_g[layer, 1]), row(ln_b[layer, 1])
        if layer % 2 == 0:
            qkv, kmean = _qkv_projection(h, attn_w_qkv[j].astype(_BF16),
                                         head_dim ** -0.5 * LOG2_E)
            o = _moba_attention(qkv, kmean.reshape(batch, n_blk, d), slopes, batch, seq)
            h = _proj_residual_ln(o, attn_w_o[j].astype(_BF16), h, g0, b0, alpha)
        else:
            w_gate = jnp.concatenate([lru_w_a[j], lru_w_x[j]], axis=-1).astype(_BF16)
            h = _lru_layer(h, batch, seq, lru_w_in[j].astype(_BF16), lru_conv_w[j],
                           row(lru_conv_b[j]), w_gate, row(lru_b_a[j]), row(lru_b_x[j]),
                           row(lru_lambda[j]), lru_w_out[j].astype(_BF16), g0, b0, alpha)
        h = _ffn_layer(h, ffn_w_in[layer].astype(_BF16), ffn_w_out[layer].astype(_BF16),
                       g1, b1, alpha)
    return h.reshape(batch, seq, d)
```

```python
import functools

import jax
import jax.numpy as jnp
from jax import lax
from jax.experimental import pallas as pl
from jax.experimental.pallas import tpu as pltpu

N_HEADS = 8
MOBA_BLOCK = 256
MOBA_TOPK = 3
NEG_INF = -1e30
LRU_BLOCKS = 8
CONV_WIDTH = 4
LRU_C = 8.0
LN_EPS = 1e-5
LOG2_E = 1.4426950408889634

V7X_SUBLANES = 8
V7X_LANES = 128
V7X_VMEM_LIMIT_BYTES = 56 * 1024 * 1024

ROW_TILE = 512
STREAM_TILE = MOBA_BLOCK
FFN_CHUNK = 256

_F32 = jnp.float32
_BF16 = jnp.bfloat16
_NT = (((1,), (1,)), ((), ()))


def _compiler_params(semantics):
    return pltpu.CompilerParams(dimension_semantics=semantics,
                                vmem_limit_bytes=V7X_VMEM_LIMIT_BYTES)


def _resident(shape):
    zeros = (0,) * len(shape)
    return pl.BlockSpec(shape, lambda *_: zeros, pipeline_mode=pl.Buffered(1))


def _to_stream_order(x):
    batch, seq, d = x.shape
    steps = STREAM_TILE // V7X_SUBLANES
    x = x.reshape(batch, seq // STREAM_TILE, V7X_SUBLANES, steps, d)
    return x.transpose(0, 1, 3, 2, 4).reshape(batch * seq, d)


def _from_stream_order(h, batch, seq):
    d = h.shape[-1]
    steps = STREAM_TILE // V7X_SUBLANES
    h = h.reshape(batch, seq // STREAM_TILE, steps, V7X_SUBLANES, d)
    return h.transpose(0, 1, 3, 2, 4).reshape(batch, seq, d)


def _tile_time(idx):
    steps = STREAM_TILE // V7X_SUBLANES
    shift = V7X_SUBLANES.bit_length() - 1
    return (idx & (V7X_SUBLANES - 1)) * steps + (idx >> shift)


def _layer_norm(y, g, b):
    mu = jnp.mean(y, axis=-1, keepdims=True)
    yc = y - mu
    var = jnp.mean(yc * yc, axis=-1, keepdims=True)
    return yc * lax.rsqrt(var + LN_EPS) * g + b


def _dot(a, b):
    return jnp.dot(a, b, preferred_element_type=_F32)


def _qkv_kernel(x_ref, w_ref, qkv_ref, kmean_ref, *, d_model, scale):
    xb = x_ref[...].astype(_BF16)
    rows = xb.shape[0]
    col_chunk = 512
    for c in range(0, 3 * d_model, col_chunk):
        acc = _dot(xb, w_ref[:, c:c + col_chunk])
        if c < d_model:
            acc = acc * scale
        elif c < 2 * d_model:
            for r in range(rows // MOBA_BLOCK):
                blk = acc[r * MOBA_BLOCK:(r + 1) * MOBA_BLOCK, :]
                kmean_ref[0, r:r + 1, c - d_model:c - d_model + col_chunk] = (
                    jnp.sum(blk, axis=0, keepdims=True) * (1.0 / MOBA_BLOCK))
        qkv_ref[:, c:c + col_chunk] = acc.astype(_BF16)


def _qkv_projection(h, w_qkv, scale):
    m, d = h.shape
    blocks_per_tile = ROW_TILE // MOBA_BLOCK
    return pl.pallas_call(
        functools.partial(_qkv_kernel, d_model=d, scale=scale),
        grid=(m // ROW_TILE,),
        in_specs=[pl.BlockSpec((ROW_TILE, d), lambda i: (i, 0)),
                  _resident((d, 3 * d))],
        out_specs=[pl.BlockSpec((ROW_TILE, 3 * d), lambda i: (i, 0)),
                   pl.BlockSpec((1, blocks_per_tile, d), lambda i: (i, 0, 0))],
        out_shape=[jax.ShapeDtypeStruct((m, 3 * d), _BF16),
                   jax.ShapeDtypeStruct((m // ROW_TILE, blocks_per_tile, d), _F32)],
        compiler_params=_compiler_params(("arbitrary",)),
        name="qkv_projection",
    )(h, w_qkv)


def _moba_kernel(slopes_ref, q_ref, k_ref, v_ref, kmean_ref, o_ref, s_scr, p_scr, *, n_blk):
    blk = MOBA_BLOCK
    slope2 = slopes_ref[pl.program_id(1)] * LOG2_E
    km = kmean_ref[...]
    km_hi = km.astype(_BF16)
    km_lo = (km - km_hi.astype(_F32)).astype(_BF16)
    key_bias = _tile_time(lax.broadcasted_iota(jnp.int32, (1, blk), 1)).astype(_F32) * slope2
    row = _tile_time(lax.broadcasted_iota(jnp.int32, (blk, blk), 0))
    col = _tile_time(lax.broadcasted_iota(jnp.int32, (blk, blk), 1))

    def block_scores(n, own):
        t = s_scr[:, n * blk:(n + 1) * blk] + key_bias
        return jnp.where(col > row, NEG_INF, t) if own else t

    for qi in range(n_blk):
        width = (qi + 1) * blk
        q = q_ref[qi * blk:(qi + 1) * blk, :]
        s_scr[:, 0:width] = lax.dot_general(q, k_ref[0:width, :], _NT,
                                            preferred_element_type=_F32)

        offsets = [slope2 * float(-blk * (qi - n)) for n in range(qi)]
        if qi > MOBA_TOPK:
            gate = (lax.dot_general(q, km_hi, _NT, preferred_element_type=_F32)
                    + lax.dot_general(q, km_lo, _NT, preferred_element_type=_F32))
            g = [jnp.broadcast_to(gate[:, n:n + 1], (blk, V7X_LANES)) for n in range(qi)]
            losses = [jnp.zeros((blk, V7X_LANES), _F32) for _ in range(qi)]
            for n in range(qi):
                for m in range(n):
                    m_first = jnp.where(g[m] >= g[n], 1.0, 0.0)
                    losses[n] = losses[n] + m_first
                    losses[m] = losses[m] + (1.0 - m_first)
            row_term = [jnp.where(losses[n] < float(MOBA_TOPK), offsets[n], NEG_INF)
                        for n in range(qi)]
        else:
            row_term = offsets

        m_run = jnp.max(block_scores(qi, True), axis=-1, keepdims=True)
        for n in range(qi):
            m_run = jnp.maximum(m_run, jnp.max(block_scores(n, False), axis=-1,
                                               keepdims=True) + row_term[n])
        denom = None
        for n in range(qi + 1):
            shift = m_run if n == qi else m_run - row_term[n]
            if shift.shape[1] == V7X_LANES:
                shift = jnp.concatenate([shift] * (blk // V7X_LANES), axis=-1)
            p = jnp.exp2(block_scores(n, n == qi) - shift)
            part = jnp.sum(p, axis=-1, keepdims=True)
            denom = part if denom is None else denom + part
            p_scr[:, n * blk:(n + 1) * blk] = p.astype(_BF16)
        out = _dot(p_scr[:, 0:width], v_ref[0:width, :])
        o_ref[qi * blk:(qi + 1) * blk, :] = (out / denom).astype(o_ref.dtype)


def _moba_attention(qkv, kmean, slopes, batch, seq):
    m, d3 = qkv.shape
    d = d3 // 3
    dh = d // N_HEADS
    n_blk = seq // MOBA_BLOCK
    return pl.pallas_call(
        functools.partial(_moba_kernel, n_blk=n_blk),
        grid=(batch, N_HEADS),
        in_specs=[pl.BlockSpec(memory_space=pltpu.SMEM),
                  pl.BlockSpec((seq, dh), lambda b, h: (b, h)),
                  pl.BlockSpec((seq, dh), lambda b, h: (b, N_HEADS + h)),
                  pl.BlockSpec((seq, dh), lambda b, h: (b, 2 * N_HEADS + h)),
                  pl.BlockSpec((None, n_blk, dh), lambda b, h: (b, 0, h))],
        out_specs=pl.BlockSpec((seq, dh), lambda b, h: (b, h)),
        out_shape=jax.ShapeDtypeStruct((m, d), _BF16),
        scratch_shapes=[pltpu.VMEM((MOBA_BLOCK, seq), _F32),
                        pltpu.VMEM((MOBA_BLOCK, seq), _BF16)],
        compiler_params=_compiler_params(("arbitrary", "arbitrary")),
        name="moba_attention",
    )(slopes, qkv, qkv, qkv, kmean)


def _proj_ln_kernel(a_ref, w_ref, res_ref, g_ref, b_ref, out_ref, *, alpha):
    y = _dot(a_ref[...], w_ref[...]) + alpha * res_ref[...]
    out_ref[...] = _layer_norm(y, g_ref[...], b_ref[...])


def _proj_residual_ln(a, w, res, g, b, alpha):
    m, d = res.shape
    k = a.shape[1]
    row_spec = lambda width: pl.BlockSpec((ROW_TILE, width), lambda i: (i, 0))
    return pl.pallas_call(
        functools.partial(_proj_ln_kernel, alpha=alpha),
        grid=(m // ROW_TILE,),
        in_specs=[row_spec(k), _resident((k, d)), row_spec(d),
                  _resident((1, d)), _resident((1, d))],
        out_specs=row_spec(d),
        out_shape=jax.ShapeDtypeStruct((m, d), _F32),
        compiler_params=_compiler_params(("arbitrary",)),
        name="out_projection_ln",
    )(a, w, res, g, b)


def _ffn_kernel(x_ref, w_in_ref, w_out_ref, g_ref, b_ref, out_ref, *, hidden, alpha):
    x = x_ref[...]
    xb = x.astype(_BF16)
    acc = alpha * x
    for c in range(0, hidden, FFN_CHUNK):
        gate = _dot(xb, w_in_ref[:, c:c + FFN_CHUNK])
        up = _dot(xb, w_in_ref[:, hidden + c:hidden + c + FFN_CHUNK])
        act = (gate * jax.nn.sigmoid(gate) * up).astype(_BF16)
        acc = acc + _dot(act, w_out_ref[c:c + FFN_CHUNK, :])
    out_ref[...] = _layer_norm(acc, g_ref[...], b_ref[...])


def _ffn_layer(h, w_in, w_out, g, b, alpha):
    m, d = h.shape
    hidden = w_out.shape[0]
    row_spec = pl.BlockSpec((ROW_TILE, d), lambda i: (i, 0))
    return pl.pallas_call(
        functools.partial(_ffn_kernel, hidden=hidden, alpha=alpha),
        grid=(m // ROW_TILE,),
        in_specs=[row_spec, _resident((d, 2 * hidden)), _resident((hidden, d)),
                  _resident((1, d)), _resident((1, d))],
        out_specs=row_spec,
        out_shape=jax.ShapeDtypeStruct((m, d), _F32),
        compiler_params=_compiler_params(("arbitrary",)),
        name="swiglu_ffn_ln",
    )(h, w_in, w_out, g, b)


def _gelu_tanh(x):
    inner = 0.7978845608028654 * (x + 0.044715 * (x * x * x))
    return 0.5 * x * (1.0 + jnp.tanh(inner))


def _softplus(z):
    return jnp.maximum(z, 0.0) + jnp.log1p(jnp.exp(-jnp.abs(z)))


def _lru_kernel(x_ref, w_in_ref, conv_w_ref, conv_b_ref, w_gate_ref, b_a_ref, b_x_ref,
                lam_ref, w_out_ref, g_ref, b_ref, out_ref,
                xbuf, tail, a_s, b_s, h_s, ac_s, carry, *, alpha):
    sub = V7X_SUBLANES
    seq_tile, width = a_s.shape
    steps = seq_tile // sub
    halo = (CONV_WIDTH - 1) * sub

    @pl.when(pl.program_id(1) == 0)
    def _():
        tail[...] = jnp.zeros_like(tail)
        carry[...] = jnp.zeros_like(carry)

    x = x_ref[...]
    xb = x.astype(_BF16)
    xbuf[halo:halo + seq_tile, :] = _dot(xb, w_in_ref[:, 0:width])
    gate_branch = _gelu_tanh(_dot(xb, w_in_ref[:, width:2 * width]))

    first_sublane = lax.broadcasted_iota(jnp.int32, (sub, width), 0) == 0
    for g in range(CONV_WIDTH - 1):
        rows = slice(g * sub, (g + 1) * sub)
        cur = xbuf[seq_tile + g * sub:seq_tile + (g + 1) * sub, :]
        xbuf[rows, :] = jnp.where(first_sublane, pltpu.roll(tail[rows, :], 1, axis=0),
                                  pltpu.roll(cur, 1, axis=0))
        tail[rows, :] = cur

    xc = conv_b_ref[...]
    for tap in range(CONV_WIDTH):
        start = halo - (CONV_WIDTH - 1 - tap) * sub
        xc = xc + xbuf[start:start + seq_tile, :] * conv_w_ref[tap:tap + 1, :]

    xcb = xc.astype(_BF16)
    group = width // LRU_BLOCKS
    pre_a, pre_x = [], []
    for gidx in range(LRU_BLOCKS):
        both = _dot(xcb[:, gidx * group:(gidx + 1) * group], w_gate_ref[gidx])
        pre_a.append(both[:, 0:group])
        pre_x.append(both[:, group:2 * group])
    r = jax.nn.sigmoid(jnp.concatenate(pre_a, axis=-1) + b_a_ref[...])
    i = jax.nn.sigmoid(jnp.concatenate(pre_x, axis=-1) + b_x_ref[...])
    log_a = (-LRU_C) * r * _softplus(-lam_ref[...])
    a = jnp.exp(log_a)
    a_s[...] = a
    b_s[...] = jnp.sqrt(1.0 - a * a) * (i * xc)

    h_loc = jnp.zeros((sub, width), _F32)
    a_cum = jnp.ones((sub, width), _F32)
    for j in range(steps):
        rows = slice(j * sub, (j + 1) * sub)
        a_j = a_s[rows, :]
        h_loc = a_j * h_loc + b_s[rows, :]
        a_cum = a_j * a_cum
        h_s[rows, :] = h_loc
        ac_s[rows, :] = a_cum

    state = carry[...]
    starts = []
    for c in range(sub):
        starts.append(state)
        state = h_loc[c:c + 1, :] + a_cum[c:c + 1, :] * state
    carry[...] = state
    start = jnp.tile(jnp.concatenate(starts, axis=0), (steps, 1))
    h = h_s[...] + ac_s[...] * start

    y = _dot((h * gate_branch).astype(_BF16), w_out_ref[...]) + alpha * x
    out_ref[...] = _layer_norm(y, g_ref[...], b_ref[...])


def _lru_layer(h, batch, seq, w_in, conv_w, conv_b, w_gate, b_a, b_x, lam, w_out, g, b, alpha):
    m, d = h.shape
    width = w_out.shape[0]
    tiles = seq // STREAM_TILE
    halo = (CONV_WIDTH - 1) * V7X_SUBLANES
    tile_f32 = pltpu.VMEM((STREAM_TILE, width), _F32)
    row_spec = pl.BlockSpec((STREAM_TILE, d), lambda bi, ti: (bi * tiles + ti, 0))
    return pl.pallas_call(
        functools.partial(_lru_kernel, alpha=alpha),
        grid=(batch, tiles),
        in_specs=[row_spec, _resident(w_in.shape), _resident(conv_w.shape),
                  _resident(conv_b.shape), _resident(w_gate.shape), _resident(b_a.shape),
                  _resident(b_x.shape), _resident(lam.shape), _resident(w_out.shape),
                  _resident(g.shape), _resident(b.shape)],
        out_specs=row_spec,
        out_shape=jax.ShapeDtypeStruct((m, d), _F32),
        scratch_shapes=[pltpu.VMEM((halo + STREAM_TILE, width), _F32),
                        pltpu.VMEM((halo, width), _F32),
                        tile_f32,
                        tile_f32,
                        tile_f32,
                        tile_f32,
                        pltpu.VMEM((1, width), _F32)],
        compiler_params=_compiler_params(("arbitrary", "arbitrary")),
        name="rglru_block_ln",
    )(h, w_in, conv_w, conv_b, w_gate, b_a, b_x, lam, w_out, g, b)


def kernel(x, attn_w_qkv, attn_w_o, lru_w_in, lru_conv_w, lru_conv_b, lru_w_a, lru_b_a,
           lru_w_x, lru_b_x, lru_lambda, lru_w_out, ffn_w_in, ffn_w_out, ln_g, ln_b):
    batch, seq, d = x.shape
    depth = ffn_w_in.shape[0]
    assert seq % MOBA_BLOCK == 0 and (batch * seq) % ROW_TILE == 0
    assert d % N_HEADS == 0 and ROW_TILE % MOBA_BLOCK == 0 and seq % ROW_TILE == 0
    alpha = (2 * depth) ** 0.25
    head_dim = d // N_HEADS
    n_blk = seq // MOBA_BLOCK
    slopes = jnp.exp2(-8.0 * (jnp.arange(N_HEADS, dtype=_F32) + 1.0) / N_HEADS)
    row = lambda v: v.reshape(1, -1)

    h = _to_stream_order(x)
    for layer in range(depth):
        j = layer // 2
        g0, b0 = row(ln_g[layer, 0]), row(ln_b[layer, 0])
        g1, b1 = row(ln_g[layer, 1]), row(ln_b[layer, 1])
        if layer % 2 == 0:
            qkv, kmean = _qkv_projection(h, attn_w_qkv[j].astype(_BF16),
                                         head_dim ** -0.5 * LOG2_E)
            o = _moba_attention(qkv, kmean.reshape(batch, n_blk, d), slopes, batch, seq)
            h = _proj_residual_ln(o, attn_w_o[j].astype(_BF16), h, g0, b0, alpha)
        else:
            w_gate = jnp.concatenate([lru_w_a[j], lru_w_x[j]], axis=-1).astype(_BF16)
            h = _lru_layer(h, batch, seq, lru_w_in[j].astype(_BF16), lru_conv_w[j],
                           row(lru_conv_b[j]), w_gate, row(lru_b_a[j]), row(lru_b_x[j]),
                           row(lru_lambda[j]), lru_w_out[j].astype(_BF16), g0, b0, alpha)
        h = _ffn_layer(h, ffn_w_in[layer].astype(_BF16), ffn_w_out[layer].astype(_BF16),
                       g1, b1, alpha)
    return _from_stream_order(h, batch, seq)
```

```python
import functools

import jax
import jax.numpy as jnp
from jax import lax
from jax.experimental import pallas as pl
from jax.experimental.pallas import tpu as pltpu

N_HEADS = 8
MOBA_BLOCK = 256
MOBA_TOPK = 3
NEG_INF = -1e30
LRU_BLOCKS = 8
CONV_WIDTH = 4
LRU_C = 8.0
LN_EPS = 1e-5
LOG2_E = 1.4426950408889634

V7X_SUBLANES = 8
V7X_LANES = 128
V7X_VMEM_LIMIT_BYTES = 56 * 1024 * 1024

ROW_TILE = 512
STREAM_TILE = MOBA_BLOCK
FFN_CHUNK = 256

_F32 = jnp.float32
_BF16 = jnp.bfloat16
_NT = (((1,), (1,)), ((), ()))


def _compiler_params(semantics):
    return pltpu.CompilerParams(dimension_semantics=semantics,
                                vmem_limit_bytes=V7X_VMEM_LIMIT_BYTES)


def _resident(shape):
    zeros = (0,) * len(shape)
    return pl.BlockSpec(shape, lambda *_: zeros, pipeline_mode=pl.Buffered(1))


def _resident_layer(stacked_shape, layer):
    index = (layer,) + (0,) * (len(stacked_shape) - 1)
    return pl.BlockSpec((None,) + tuple(stacked_shape[1:]), lambda *_: index,
                        pipeline_mode=pl.Buffered(1))


def _to_stream_order(x):
    batch, seq, d = x.shape
    steps = STREAM_TILE // V7X_SUBLANES
    x = x.reshape(batch, seq // STREAM_TILE, V7X_SUBLANES, steps, d)
    return x.transpose(0, 1, 3, 2, 4).reshape(batch * seq, d)


def _from_stream_order(h, batch, seq):
    d = h.shape[-1]
    steps = STREAM_TILE // V7X_SUBLANES
    h = h.reshape(batch, seq // STREAM_TILE, steps, V7X_SUBLANES, d)
    return h.transpose(0, 1, 3, 2, 4).reshape(batch, seq, d)


def _tile_time(idx):
    steps = STREAM_TILE // V7X_SUBLANES
    shift = V7X_SUBLANES.bit_length() - 1
    return (idx & (V7X_SUBLANES - 1)) * steps + (idx >> shift)


def _layer_norm(y, g, b):
    mu = jnp.mean(y, axis=-1, keepdims=True)
    yc = y - mu
    var = jnp.mean(yc * yc, axis=-1, keepdims=True)
    return yc * lax.rsqrt(var + LN_EPS) * g + b


def _dot(a, b):
    return jnp.dot(a, b, preferred_element_type=_F32)


def _qkv_kernel(x_ref, w_qk_ref, w_vt_ref, qk_ref, vt_ref, kmean_ref, *, d_model, scale):
    xb = x_ref[...].astype(_BF16)
    rows = xb.shape[0]
    chunk = 512
    for c in range(0, 2 * d_model, chunk):
        acc = _dot(xb, w_qk_ref[:, c:c + chunk])
        if c < d_model:
            acc = acc * scale
        else:
            for r in range(rows // MOBA_BLOCK):
                blk = acc[r * MOBA_BLOCK:(r + 1) * MOBA_BLOCK, :]
                kmean_ref[0, r:r + 1, c - d_model:c - d_model + chunk] = (
                    jnp.sum(blk, axis=0, keepdims=True) * (1.0 / MOBA_BLOCK))
        qk_ref[:, c:c + chunk] = acc.astype(_BF16)
    for c in range(0, d_model, chunk):
        vt_ref[c:c + chunk, :] = lax.dot_general(
            w_vt_ref[c:c + chunk, :], xb, _NT, preferred_element_type=_F32).astype(_BF16)


def _qkv_projection(h, w_qk, w_vt, layer, scale, batch, seq):
    m, d = h.shape
    blocks_per_tile = ROW_TILE // MOBA_BLOCK
    tiles_per_seq = seq // ROW_TILE
    return pl.pallas_call(
        functools.partial(_qkv_kernel, d_model=d, scale=scale),
        grid=(m // ROW_TILE,),
        in_specs=[pl.BlockSpec((ROW_TILE, d), lambda i: (i, 0)),
                  _resident_layer(w_qk.shape, layer), _resident_layer(w_vt.shape, layer)],
        out_specs=[pl.BlockSpec((ROW_TILE, 2 * d), lambda i: (i, 0)),
                   pl.BlockSpec((None, d, ROW_TILE),
                                lambda i: (i // tiles_per_seq, 0, i % tiles_per_seq)),
                   pl.BlockSpec((1, blocks_per_tile, d), lambda i: (i, 0, 0))],
        out_shape=[jax.ShapeDtypeStruct((m, 2 * d), _BF16),
                   jax.ShapeDtypeStruct((batch, d, seq), _BF16),
                   jax.ShapeDtypeStruct((m // ROW_TILE, blocks_per_tile, d), _F32)],
        compiler_params=_compiler_params(("arbitrary",)),
        name="qkv_projection",
    )(h, w_qk, w_vt)


def _split_bf16(x, parts):
    terms = []
    for _ in range(parts):
        t = x.astype(_BF16)
        terms.append(t)
        x = x - t.astype(_F32)
    return terms


def _moba_kernel(slopes_ref, q_ref, k_ref, vt_ref, kmean_ref, o_ref,
                 k_aug, vt_aug, causal, s_scr, p_scr, *, n_blk):
    blk = MOBA_BLOCK
    seq, dh = k_ref.shape
    bias_lanes = 3

    @pl.when((pl.program_id(0) == 0) & (pl.program_id(1) == 0))
    def _():
        key_time = _tile_time(lax.broadcasted_iota(jnp.int32, (seq, dh), 0) & (blk - 1))
        key_lane = lax.broadcasted_iota(jnp.int32, (seq, dh), 1)
        k_aug[:, dh:2 * dh] = jnp.where(key_lane < bias_lanes, key_time, 0).astype(_BF16)
        extra = vt_aug.shape[0] - dh
        ones_row = lax.broadcasted_iota(jnp.int32, (extra, seq), 0) == 0
        vt_aug[dh:dh + extra, :] = jnp.where(ones_row, 1.0, 0.0).astype(_BF16)
        key_t = _tile_time(lax.broadcasted_iota(jnp.int32, (blk, blk), 0))
        query_t = _tile_time(lax.broadcasted_iota(jnp.int32, (blk, blk), 1))
        causal[...] = jnp.where(key_t > query_t, NEG_INF, 0.0)

    k_aug[:, 0:dh] = k_ref[...]
    vt_aug[0:dh, :] = vt_ref[...]

    slope2 = slopes_ref[pl.program_id(1)] * LOG2_E
    lane = lax.broadcasted_iota(jnp.int32, (blk, dh), 1)
    q_bias = jnp.zeros((blk, dh), _F32)
    for i, term in enumerate(_split_bf16(jnp.full((blk, dh), slope2, _F32), bias_lanes)):
        q_bias = jnp.where(lane == i, term.astype(_F32), q_bias)
    q_bias = q_bias.astype(_BF16)

    km_hi, km_lo = _split_bf16(kmean_ref[...], 2)
    block_id = lax.broadcasted_iota(jnp.int32, (n_blk, blk), 0)

    def score_stage(qi):
        width = (qi + 1) * blk
        q = q_ref[qi * blk:(qi + 1) * blk, :]
        s_scr[qi % 2, 0:width, :] = lax.dot_general(
            k_aug[0:width, :], jnp.concatenate([q, q_bias], axis=-1), _NT,
            preferred_element_type=_F32)
        if qi <= MOBA_TOPK:
            return None
        gate = (lax.dot_general(km_hi, q, _NT, preferred_element_type=_F32)
                + lax.dot_general(km_lo, q, _NT, preferred_element_type=_F32))
        ahead = jnp.zeros((n_blk, blk), _F32)
        for m in range(qi):
            g_m = gate[m:m + 1, :]
            wins = (g_m > gate) | ((g_m == gate) & (block_id > m))
            ahead = ahead + jnp.where(wins, 1.0, 0.0)
        return (ahead < float(MOBA_TOPK)) & (block_id < qi)

    def softmax_stage(qi, picked):
        s_buf, p_buf = s_scr.at[qi % 2], p_scr.at[qi % 2]

        def block_scores(n):
            t = s_buf[n * blk:(n + 1) * blk, :]
            return t + causal[...] if n == qi else t

        query_term = [slope2 * float(-blk * (qi - n)) for n in range(qi)]
        if picked is not None:
            query_term = [jnp.where(picked[n:n + 1, :], query_term[n], NEG_INF)
                          for n in range(qi)]
        m_run = jnp.max(block_scores(qi), axis=0, keepdims=True)
        for n in range(qi):
            m_run = jnp.maximum(m_run,
                                jnp.max(block_scores(n), axis=0, keepdims=True) + query_term[n])
        for n in range(qi + 1):
            shift = m_run if n == qi else m_run - query_term[n]
            p_buf[n * blk:(n + 1) * blk, :] = jnp.exp2(block_scores(n) - shift).astype(_BF16)

    def output_stage(qi):
        width = (qi + 1) * blk
        out = _dot(vt_aug[:, 0:width], p_scr[qi % 2, 0:width, :])
        o_ref[qi * blk:(qi + 1) * blk, :] = (out[0:dh, :] / out[dh:dh + 1, :]).T.astype(o_ref.dtype)

    picked = score_stage(0)
    for qi in range(n_blk):
        picked_next = score_stage(qi + 1) if qi + 1 < n_blk else None
        softmax_stage(qi, picked)
        output_stage(qi)
        picked = picked_next


def _moba_attention(qk, vt, kmean, slopes, batch, seq):
    m, d2 = qk.shape
    d = d2 // 2
    dh = d // N_HEADS
    n_blk = seq // MOBA_BLOCK
    bf16_rows = 2 * V7X_SUBLANES
    return pl.pallas_call(
        functools.partial(_moba_kernel, n_blk=n_blk),
        grid=(batch, N_HEADS),
        in_specs=[pl.BlockSpec(memory_space=pltpu.SMEM),
                  pl.BlockSpec((seq, dh), lambda b, h: (b, h)),
                  pl.BlockSpec((seq, dh), lambda b, h: (b, N_HEADS + h)),
                  pl.BlockSpec((None, dh, seq), lambda b, h: (b, h, 0)),
                  pl.BlockSpec((None, n_blk, dh), lambda b, h: (b, 0, h))],
        out_specs=pl.BlockSpec((seq, dh), lambda b, h: (b, h)),
        out_shape=jax.ShapeDtypeStruct((m, d), _BF16),
        scratch_shapes=[pltpu.VMEM((seq, 2 * dh), _BF16),
                        pltpu.VMEM((dh + bf16_rows, seq), _BF16),
                        pltpu.VMEM((MOBA_BLOCK, MOBA_BLOCK), _F32),
                        pltpu.VMEM((2, seq, MOBA_BLOCK), _F32),
                        pltpu.VMEM((2, seq, MOBA_BLOCK), _BF16)],
        compiler_params=_compiler_params(("arbitrary", "arbitrary")),
        name="moba_attention",
    )(slopes, qk, qk, vt, kmean)


def _proj_ln_kernel(a_ref, w_ref, res_ref, g_ref, b_ref, out_ref, *, alpha):
    y = _dot(a_ref[...], w_ref[...]) + alpha * res_ref[...]
    out_ref[...] = _layer_norm(y, g_ref[...], b_ref[...])


def _proj_residual_ln(a, w, layer, res, g, b, alpha):
    m, d = res.shape
    k = a.shape[1]
    row_spec = lambda width: pl.BlockSpec((ROW_TILE, width), lambda i: (i, 0))
    return pl.pallas_call(
        functools.partial(_proj_ln_kernel, alpha=alpha),
        grid=(m // ROW_TILE,),
        in_specs=[row_spec(k), _resident_layer(w.shape, layer), row_spec(d),
                  _resident((1, d)), _resident((1, d))],
        out_specs=row_spec(d),
        out_shape=jax.ShapeDtypeStruct((m, d), _F32),
        compiler_params=_compiler_params(("arbitrary",)),
        name="out_projection_ln",
    )(a, w, res, g, b)


def _ffn_kernel(x_ref, w_in_ref, w_out_ref, g_ref, b_ref, out_ref, *, hidden, alpha):
    x = x_ref[...]
    xb = x.astype(_BF16)
    acc = alpha * x
    for c in range(0, hidden, FFN_CHUNK):
        gate = _dot(xb, w_in_ref[:, c:c + FFN_CHUNK])
        up = _dot(xb, w_in_ref[:, hidden + c:hidden + c + FFN_CHUNK])
        act = (gate * jax.nn.sigmoid(gate) * up).astype(_BF16)
        acc = acc + _dot(act, w_out_ref[c:c + FFN_CHUNK, :])
    out_ref[...] = _layer_norm(acc, g_ref[...], b_ref[...])


def _ffn_layer(h, w_in, w_out, layer, g, b, alpha):
    m, d = h.shape
    hidden = w_out.shape[1]
    row_spec = pl.BlockSpec((ROW_TILE, d), lambda i: (i, 0))
    return pl.pallas_call(
        functools.partial(_ffn_kernel, hidden=hidden, alpha=alpha),
        grid=(m // ROW_TILE,),
        in_specs=[row_spec, _resident_layer(w_in.shape, layer),
                  _resident_layer(w_out.shape, layer),
                  _resident((1, d)), _resident((1, d))],
        out_specs=row_spec,
        out_shape=jax.ShapeDtypeStruct((m, d), _F32),
        compiler_params=_compiler_params(("arbitrary",)),
        name="swiglu_ffn_ln",
    )(h, w_in, w_out, g, b)


def _gelu_tanh(x):
    inner = 0.7978845608028654 * (x + 0.044715 * (x * x * x))
    return 0.5 * x * (1.0 + jnp.tanh(inner))


def _softplus(z):
    return jnp.maximum(z, 0.0) + jnp.log1p(jnp.exp(-jnp.abs(z)))


def _lru_kernel(x_ref, w_in_ref, conv_w_ref, conv_b_ref, w_gate_ref, b_a_ref, b_x_ref,
                lam_ref, w_out_ref, g_ref, b_ref, out_ref,
                xbuf, tail, a_s, b_s, h_s, ac_s, carry, *, alpha):
    sub = V7X_SUBLANES
    seq_tile, width = a_s.shape
    steps = seq_tile // sub
    halo = (CONV_WIDTH - 1) * sub

    @pl.when(pl.program_id(1) == 0)
    def _():
        tail[...] = jnp.zeros_like(tail)
        carry[...] = jnp.zeros_like(carry)

    x = x_ref[...]
    xb = x.astype(_BF16)
    xbuf[halo:halo + seq_tile, :] = _dot(xb, w_in_ref[:, 0:width])
    gate_branch = _gelu_tanh(_dot(xb, w_in_ref[:, width:2 * width]))

    first_sublane = lax.broadcasted_iota(jnp.int32, (sub, width), 0) == 0
    for g in range(CONV_WIDTH - 1):
        rows = slice(g * sub, (g + 1) * sub)
        cur = xbuf[seq_tile + g * sub:seq_tile + (g + 1) * sub, :]
        xbuf[rows, :] = jnp.where(first_sublane, pltpu.roll(tail[rows, :], 1, axis=0),
                                  pltpu.roll(cur, 1, axis=0))
        tail[rows, :] = cur

    xc = conv_b_ref[...]
    for tap in range(CONV_WIDTH):
        start = halo - (CONV_WIDTH - 1 - tap) * sub
        xc = xc + xbuf[start:start + seq_tile, :] * conv_w_ref[tap:tap + 1, :]

    xcb = xc.astype(_BF16)
    group = width // LRU_BLOCKS
    pre_a, pre_x = [], []
    for gidx in range(LRU_BLOCKS):
        both = _dot(xcb[:, gidx * group:(gidx + 1) * group], w_gate_ref[gidx])
        pre_a.append(both[:, 0:group])
        pre_x.append(both[:, group:2 * group])
    r = jax.nn.sigmoid(jnp.concatenate(pre_a, axis=-1) + b_a_ref[...])
    i = jax.nn.sigmoid(jnp.concatenate(pre_x, axis=-1) + b_x_ref[...])
    log_a = (-LRU_C) * r * _softplus(-lam_ref[...])
    a = jnp.exp(log_a)
    a_s[...] = a
    b_s[...] = jnp.sqrt(1.0 - a * a) * (i * xc)

    h_loc = jnp.zeros((sub, width), _F32)
    a_cum = jnp.ones((sub, width), _F32)
    for j in range(steps):
        rows = slice(j * sub, (j + 1) * sub)
        a_j = a_s[rows, :]
        h_loc = a_j * h_loc + b_s[rows, :]
        a_cum = a_j * a_cum
        h_s[rows, :] = h_loc
        ac_s[rows, :] = a_cum

    state = carry[...]
    starts = []
    for c in range(sub):
        starts.append(state)
        state = h_loc[c:c + 1, :] + a_cum[c:c + 1, :] * state
    carry[...] = state
    start = jnp.tile(jnp.concatenate(starts, axis=0), (steps, 1))
    h = h_s[...] + ac_s[...] * start

    y = _dot((h * gate_branch).astype(_BF16), w_out_ref[...]) + alpha * x
    out_ref[...] = _layer_norm(y, g_ref[...], b_ref[...])


def _lru_layer(h, batch, seq, layer, w_in, conv_w, conv_b, w_gate, b_a, b_x, lam, w_out, g, b,
               alpha):
    m, d = h.shape
    width = w_out.shape[1]
    tiles = seq // STREAM_TILE
    halo = (CONV_WIDTH - 1) * V7X_SUBLANES
    tile_f32 = pltpu.VMEM((STREAM_TILE, width), _F32)
    row_spec = pl.BlockSpec((STREAM_TILE, d), lambda bi, ti: (bi * tiles + ti, 0))
    return pl.pallas_call(
        functools.partial(_lru_kernel, alpha=alpha),
        grid=(batch, tiles),
        in_specs=[row_spec, _resident_layer(w_in.shape, layer), _resident(conv_w.shape),
                  _resident(conv_b.shape), _resident_layer(w_gate.shape, layer),
                  _resident(b_a.shape), _resident(b_x.shape), _resident(lam.shape),
                  _resident_layer(w_out.shape, layer), _resident(g.shape), _resident(b.shape)],
        out_specs=row_spec,
        out_shape=jax.ShapeDtypeStruct((m, d), _F32),
        scratch_shapes=[pltpu.VMEM((halo + STREAM_TILE, width), _F32),
                        pltpu.VMEM((halo, width), _F32),
                        tile_f32,
                        tile_f32,
                        tile_f32,
                        tile_f32,
                        pltpu.VMEM((1, width), _F32)],
        compiler_params=_compiler_params(("arbitrary", "arbitrary")),
        name="rglru_block_ln",
    )(h, w_in, conv_w, conv_b, w_gate, b_a, b_x, lam, w_out, g, b)


def kernel(x, attn_w_qkv, attn_w_o, lru_w_in, lru_conv_w, lru_conv_b, lru_w_a, lru_b_a,
           lru_w_x, lru_b_x, lru_lambda, lru_w_out, ffn_w_in, ffn_w_out, ln_g, ln_b):
    batch, seq, d = x.shape
    depth = ffn_w_in.shape[0]
    assert seq % MOBA_BLOCK == 0 and (batch * seq) % ROW_TILE == 0
    assert d % N_HEADS == 0 and ROW_TILE % MOBA_BLOCK == 0 and seq % ROW_TILE == 0
    alpha = (2 * depth) ** 0.25
    head_dim = d // N_HEADS
    n_blk = seq // MOBA_BLOCK
    slopes = jnp.exp2(-8.0 * (jnp.arange(N_HEADS, dtype=_F32) + 1.0) / N_HEADS)
    row = lambda v: v.reshape(1, -1)

    w_qk = attn_w_qkv[:, :, 0:2 * d].astype(_BF16)
    w_vt = jnp.swapaxes(attn_w_qkv[:, :, 2 * d:3 * d], 1, 2).astype(_BF16)
    w_o = attn_w_o.astype(_BF16)
    w_lru_in, w_lru_out = lru_w_in.astype(_BF16), lru_w_out.astype(_BF16)
    w_gate = jnp.concatenate([lru_w_a, lru_w_x], axis=-1).astype(_BF16)
    w_ffn_in, w_ffn_out = ffn_w_in.astype(_BF16), ffn_w_out.astype(_BF16)

    h = _to_stream_order(x)
    for layer in range(depth):
        j = layer // 2
        g0, b0 = row(ln_g[layer, 0]), row(ln_b[layer, 0])
        g1, b1 = row(ln_g[layer, 1]), row(ln_b[layer, 1])
        if layer % 2 == 0:
            qk, vt, kmean = _qkv_projection(h, w_qk, w_vt, j, head_dim ** -0.5 * LOG2_E,
                                            batch, seq)
            o = _moba_attention(qk, vt, kmean.reshape(batch, n_blk, d), slopes, batch, seq)
            h = _proj_residual_ln(o, w_o, j, h, g0, b0, alpha)
        else:
            h = _lru_layer(h, batch, seq, j, w_lru_in, lru_conv_w[j], row(lru_conv_b[j]),
                           w_gate, row(lru_b_a[j]), row(lru_b_x[j]), row(lru_lambda[j]),
                           w_lru_out, g0, b0, alpha)
        h = _ffn_layer(h, w_ffn_in, w_ffn_out, layer, g1, b1, alpha)
    return _from_stream_order(h, batch, seq)
```

```python
import functools

import jax
import jax.numpy as jnp
from jax import lax
from jax.experimental import pallas as pl
from jax.experimental.pallas import tpu as pltpu

N_HEADS = 8
MOBA_BLOCK = 256
MOBA_TOPK = 3
NEG_INF = -1e30
LRU_BLOCKS = 8
CONV_WIDTH = 4
LRU_C = 8.0
LN_EPS = 1e-5
LOG2_E = 1.4426950408889634

V7X_SUBLANES = 8
V7X_LANES = 128
V7X_VMEM_LIMIT_BYTES = 56 * 1024 * 1024

ROW_TILE = 512
STREAM_TILE = MOBA_BLOCK
FFN_CHUNK = 256
FFN_ROW_TILE = 1024
FFN_SUB_ROWS = 512
LRU_TILES = 2

_F32 = jnp.float32
_BF16 = jnp.bfloat16
_NT = (((1,), (1,)), ((), ()))


def _compiler_params(semantics):
    return pltpu.CompilerParams(dimension_semantics=semantics,
                                vmem_limit_bytes=V7X_VMEM_LIMIT_BYTES)


def _resident(shape):
    zeros = (0,) * len(shape)
    return pl.BlockSpec(shape, lambda *_: zeros, pipeline_mode=pl.Buffered(1))


def _resident_layer(stacked_shape, layer):
    index = (layer,) + (0,) * (len(stacked_shape) - 1)
    return pl.BlockSpec((None,) + tuple(stacked_shape[1:]), lambda *_: index,
                        pipeline_mode=pl.Buffered(1))


def _to_stream_order(x):
    batch, seq, d = x.shape
    steps = STREAM_TILE // V7X_SUBLANES
    x = x.reshape(batch, seq // STREAM_TILE, V7X_SUBLANES, steps, d)
    return x.transpose(0, 1, 3, 2, 4).reshape(batch * seq, d)


def _from_stream_order(h, batch, seq):
    d = h.shape[-1]
    steps = STREAM_TILE // V7X_SUBLANES
    h = h.reshape(batch, seq // STREAM_TILE, steps, V7X_SUBLANES, d)
    return h.transpose(0, 1, 3, 2, 4).reshape(batch, seq, d)


def _tile_time(idx):
    steps = STREAM_TILE // V7X_SUBLANES
    shift = V7X_SUBLANES.bit_length() - 1
    return (idx & (V7X_SUBLANES - 1)) * steps + (idx >> shift)


def _layer_norm(y, g, b):
    mu = jnp.mean(y, axis=-1, keepdims=True)
    yc = y - mu
    var = jnp.mean(yc * yc, axis=-1, keepdims=True)
    return yc * lax.rsqrt(var + LN_EPS) * g + b


def _dot(a, b):
    return jnp.dot(a, b, preferred_element_type=_F32)


def _qkv_kernel(x_ref, w_qk_ref, w_vt_ref, qk_ref, vt_ref, kmean_ref, *, d_model, scale):
    xb = x_ref[...].astype(_BF16)
    rows = xb.shape[0]
    chunk = 512
    for c in range(0, 2 * d_model, chunk):
        acc = _dot(xb, w_qk_ref[:, c:c + chunk])
        if c < d_model:
            acc = acc * scale
        else:
            for r in range(rows // MOBA_BLOCK):
                blk = acc[r * MOBA_BLOCK:(r + 1) * MOBA_BLOCK, :]
                kmean_ref[0, r:r + 1, c - d_model:c - d_model + chunk] = (
                    jnp.sum(blk, axis=0, keepdims=True) * (1.0 / MOBA_BLOCK))
        qk_ref[:, c:c + chunk] = acc.astype(_BF16)
    for c in range(0, d_model, chunk):
        vt_ref[c:c + chunk, :] = lax.dot_general(
            w_vt_ref[c:c + chunk, :], xb, _NT, preferred_element_type=_F32).astype(_BF16)


def _qkv_projection(h, w_qk, w_vt, layer, scale, batch, seq):
    m, d = h.shape
    blocks_per_tile = ROW_TILE // MOBA_BLOCK
    tiles_per_seq = seq // ROW_TILE
    return pl.pallas_call(
        functools.partial(_qkv_kernel, d_model=d, scale=scale),
        grid=(m // ROW_TILE,),
        in_specs=[pl.BlockSpec((ROW_TILE, d), lambda i: (i, 0)),
                  _resident_layer(w_qk.shape, layer), _resident_layer(w_vt.shape, layer)],
        out_specs=[pl.BlockSpec((ROW_TILE, 2 * d), lambda i: (i, 0)),
                   pl.BlockSpec((None, d, ROW_TILE),
                                lambda i: (i // tiles_per_seq, 0, i % tiles_per_seq)),
                   pl.BlockSpec((1, blocks_per_tile, d), lambda i: (i, 0, 0))],
        out_shape=[jax.ShapeDtypeStruct((m, 2 * d), _BF16),
                   jax.ShapeDtypeStruct((batch, d, seq), _BF16),
                   jax.ShapeDtypeStruct((m // ROW_TILE, blocks_per_tile, d), _F32)],
        compiler_params=_compiler_params(("arbitrary",)),
        name="qkv_projection",
    )(h, w_qk, w_vt)


def _split_bf16(x, parts):
    terms = []
    for _ in range(parts):
        t = x.astype(_BF16)
        terms.append(t)
        x = x - t.astype(_F32)
    return terms


def _moba_kernel(slopes_ref, q_ref, k_ref, vt_ref, kmean_ref, o_ref,
                 k_aug, vt_aug, causal, s_scr, p_scr, *, n_blk):
    blk = MOBA_BLOCK
    seq, dh = k_ref.shape
    bias_lanes = 3

    @pl.when((pl.program_id(0) == 0) & (pl.program_id(1) == 0))
    def _():
        key_time = _tile_time(lax.broadcasted_iota(jnp.int32, (seq, dh), 0) & (blk - 1))
        key_lane = lax.broadcasted_iota(jnp.int32, (seq, dh), 1)
        k_aug[:, dh:2 * dh] = jnp.where(key_lane < bias_lanes, key_time, 0).astype(_BF16)
        extra = vt_aug.shape[0] - dh
        ones_row = lax.broadcasted_iota(jnp.int32, (extra, seq), 0) == 0
        vt_aug[dh:dh + extra, :] = jnp.where(ones_row, 1.0, 0.0).astype(_BF16)
        key_t = _tile_time(lax.broadcasted_iota(jnp.int32, (blk, blk), 0))
        query_t = _tile_time(lax.broadcasted_iota(jnp.int32, (blk, blk), 1))
        causal[...] = jnp.where(key_t > query_t, NEG_INF, 0.0)

    k_aug[:, 0:dh] = k_ref[...]
    vt_aug[0:dh, :] = vt_ref[...]

    slope2 = slopes_ref[pl.program_id(1)] * LOG2_E
    lane = lax.broadcasted_iota(jnp.int32, (blk, dh), 1)
    q_bias = jnp.zeros((blk, dh), _F32)
    for i, term in enumerate(_split_bf16(jnp.full((blk, dh), slope2, _F32), bias_lanes)):
        q_bias = jnp.where(lane == i, term.astype(_F32), q_bias)
    q_bias = q_bias.astype(_BF16)

    km_hi, km_lo = _split_bf16(kmean_ref[...], 2)
    block_id = lax.broadcasted_iota(jnp.int32, (n_blk, blk), 0)

    def score_stage(qi):
        width = (qi + 1) * blk
        q = q_ref[qi * blk:(qi + 1) * blk, :]
        s_scr[qi % 2, 0:width, :] = lax.dot_general(
            k_aug[0:width, :], jnp.concatenate([q, q_bias], axis=-1), _NT,
            preferred_element_type=_F32)
        if qi <= MOBA_TOPK:
            return None
        gate = (lax.dot_general(km_hi, q, _NT, preferred_element_type=_F32)
                + lax.dot_general(km_lo, q, _NT, preferred_element_type=_F32))
        ahead = jnp.zeros((n_blk, blk), _F32)
        for m in range(qi):
            g_m = gate[m:m + 1, :]
            wins = (g_m > gate) | ((g_m == gate) & (block_id > m))
            ahead = ahead + jnp.where(wins, 1.0, 0.0)
        return (ahead < float(MOBA_TOPK)) & (block_id < qi)

    def softmax_stage(qi, picked):
        s_buf, p_buf = s_scr.at[qi % 2], p_scr.at[qi % 2]

        def block_scores(n):
            t = s_buf[n * blk:(n + 1) * blk, :]
            return t + causal[...] if n == qi else t

        query_term = [slope2 * float(-blk * (qi - n)) for n in range(qi)]
        if picked is not None:
            query_term = [jnp.where(picked[n:n + 1, :], query_term[n], NEG_INF)
                          for n in range(qi)]
        m_run = jnp.max(block_scores(qi), axis=0, keepdims=True)
        for n in range(qi):
            m_run = jnp.maximum(m_run,
                                jnp.max(block_scores(n), axis=0, keepdims=True) + query_term[n])
        for n in range(qi + 1):
            shift = m_run if n == qi else m_run - query_term[n]
            p_buf[n * blk:(n + 1) * blk, :] = jnp.exp2(block_scores(n) - shift).astype(_BF16)

    def output_stage(qi):
        width = (qi + 1) * blk
        out = _dot(vt_aug[:, 0:width], p_scr[qi % 2, 0:width, :])
        o_ref[qi * blk:(qi + 1) * blk, :] = (out[0:dh, :] / out[dh:dh + 1, :]).T.astype(o_ref.dtype)

    picked = score_stage(0)
    for qi in range(n_blk):
        picked_next = score_stage(qi + 1) if qi + 1 < n_blk else None
        softmax_stage(qi, picked)
        output_stage(qi)
        picked = picked_next


def _moba_attention(qk, vt, kmean, slopes, batch, seq):
    m, d2 = qk.shape
    d = d2 // 2
    dh = d // N_HEADS
    n_blk = seq // MOBA_BLOCK
    bf16_rows = 2 * V7X_SUBLANES
    return pl.pallas_call(
        functools.partial(_moba_kernel, n_blk=n_blk),
        grid=(batch, N_HEADS),
        in_specs=[pl.BlockSpec(memory_space=pltpu.SMEM),
                  pl.BlockSpec((seq, dh), lambda b, h: (b, h)),
                  pl.BlockSpec((seq, dh), lambda b, h: (b, N_HEADS + h)),
                  pl.BlockSpec((None, dh, seq), lambda b, h: (b, h, 0)),
                  pl.BlockSpec((None, n_blk, dh), lambda b, h: (b, 0, h))],
        out_specs=pl.BlockSpec((seq, dh), lambda b, h: (b, h)),
        out_shape=jax.ShapeDtypeStruct((m, d), _BF16),
        scratch_shapes=[pltpu.VMEM((seq, 2 * dh), _BF16),
                        pltpu.VMEM((dh + bf16_rows, seq), _BF16),
                        pltpu.VMEM((MOBA_BLOCK, MOBA_BLOCK), _F32),
                        pltpu.VMEM((2, seq, MOBA_BLOCK), _F32),
                        pltpu.VMEM((2, seq, MOBA_BLOCK), _BF16)],
        compiler_params=_compiler_params(("arbitrary", "arbitrary")),
        name="moba_attention",
    )(slopes, qk, qk, vt, kmean)


def _proj_ln_kernel(a_ref, w_ref, res_ref, g_ref, b_ref, out_ref, *, alpha):
    y = _dot(a_ref[...], w_ref[...]) + alpha * res_ref[...]
    out_ref[...] = _layer_norm(y, g_ref[...], b_ref[...])


def _proj_residual_ln(a, w, layer, res, g, b, alpha):
    m, d = res.shape
    k = a.shape[1]
    row_spec = lambda width: pl.BlockSpec((ROW_TILE, width), lambda i: (i, 0))
    return pl.pallas_call(
        functools.partial(_proj_ln_kernel, alpha=alpha),
        grid=(m // ROW_TILE,),
        in_specs=[row_spec(k), _resident_layer(w.shape, layer), row_spec(d),
                  _resident((1, d)), _resident((1, d))],
        out_specs=row_spec(d),
        out_shape=jax.ShapeDtypeStruct((m, d), _F32),
        compiler_params=_compiler_params(("arbitrary",)),
        name="out_projection_ln",
    )(a, w, res, g, b)


def _ffn_kernel(x_ref, w_in_ref, w_out_ref, g_ref, b_ref, out_ref, *, hidden, alpha):
    for r in range(0, x_ref.shape[0], FFN_SUB_ROWS):
        x = x_ref[r:r + FFN_SUB_ROWS, :]
        xb = x.astype(_BF16)
        acc = alpha * x
        for c in range(0, hidden, FFN_CHUNK):
            gate = _dot(xb, w_in_ref[:, c:c + FFN_CHUNK])
            up = _dot(xb, w_in_ref[:, hidden + c:hidden + c + FFN_CHUNK])
            act = (gate * jax.nn.sigmoid(gate) * up).astype(_BF16)
            acc = acc + _dot(act, w_out_ref[c:c + FFN_CHUNK, :])
        out_ref[r:r + FFN_SUB_ROWS, :] = _layer_norm(acc, g_ref[...], b_ref[...])


def _ffn_layer(h, w_in, w_out, layer, g, b, alpha):
    m, d = h.shape
    hidden = w_out.shape[1]
    row_spec = pl.BlockSpec((FFN_ROW_TILE, d), lambda i: (i, 0))
    return pl.pallas_call(
        functools.partial(_ffn_kernel, hidden=hidden, alpha=alpha),
        grid=(m // FFN_ROW_TILE,),
        in_specs=[row_spec, _resident_layer(w_in.shape, layer),
                  _resident_layer(w_out.shape, layer),
                  _resident((1, d)), _resident((1, d))],
        out_specs=row_spec,
        out_shape=jax.ShapeDtypeStruct((m, d), _F32),
        compiler_params=_compiler_params(("arbitrary",)),
        name="swiglu_ffn_ln",
    )(h, w_in, w_out, g, b)


def _gelu_tanh(x):
    inner = x * (0.7978845608028654 + (0.7978845608028654 * 0.044715) * (x * x))
    half_x = 0.5 * x
    return half_x * jnp.tanh(inner) + half_x


def _softplus(z):
    return jnp.maximum(z, 0.0) + jnp.log1p(jnp.exp(-jnp.abs(z)))


def _lru_kernel(x_ref, x_next_ref, w_in_ref, conv_w_ref, conv_b_ref, w_gate_ref, b_a_ref,
                b_x_ref, lam_ref, w_out_ref, g_ref, b_ref, out_ref,
                xbuf, gate_pre, tail, a_s, b_s, h_s, ac_s, carry, *, alpha, steps_per_seq):
    sub = V7X_SUBLANES
    seq_tile, width = a_s.shape
    n_tiles = x_ref.shape[0] // seq_tile
    steps = seq_tile // sub
    halo = (CONV_WIDTH - 1) * sub
    group = width // LRU_BLOCKS
    step = pl.program_id(0)
    cur, nxt = step % 2, (step + 1) % 2

    def in_projection(src_ref, slot, t):
        xb = src_ref[t * seq_tile:(t + 1) * seq_tile, :].astype(_BF16)
        xbuf[slot, t, halo:halo + seq_tile, :] = _dot(xb, w_in_ref[:, 0:width])
        gate_pre[slot, t] = _dot(xb, w_in_ref[:, width:2 * width])

    @pl.when(step == 0)
    def _():
        for t in range(n_tiles):
            in_projection(x_ref, 0, t)

    @pl.when(step % steps_per_seq == 0)
    def _():
        tail[...] = jnp.zeros_like(tail)
        carry[...] = jnp.zeros_like(carry)

    half_rate = (-0.5 * LRU_C * LOG2_E) * _softplus(-lam_ref[...])

    def conv(t):
        first_sublane = lax.broadcasted_iota(jnp.int32, (sub, width), 0) == 0
        for g in range(CONV_WIDTH - 1):
            rows = slice(g * sub, (g + 1) * sub)
            last = xbuf[cur, t, seq_tile + g * sub:seq_tile + (g + 1) * sub, :]
            xbuf[cur, t, rows, :] = jnp.where(first_sublane,
                                              pltpu.roll(tail[rows, :], 1, axis=0),
                                              pltpu.roll(last, 1, axis=0))
            tail[rows, :] = last
        xc = conv_b_ref[...]
        for tap in range(CONV_WIDTH):
            start = halo - (CONV_WIDTH - 1 - tap) * sub
            xc = xc + xbuf[cur, t, start:start + seq_tile, :] * conv_w_ref[tap:tap + 1, :]
        return xc

    def gate_projection(xc):
        xcb = xc.astype(_BF16)
        pre_a, pre_x = [], []
        for gidx in range(LRU_BLOCKS):
            both = _dot(xcb[:, gidx * group:(gidx + 1) * group], w_gate_ref[gidx])
            pre_a.append(both[:, 0:group])
            pre_x.append(both[:, group:2 * group])
        return jnp.concatenate(pre_a, axis=-1), jnp.concatenate(pre_x, axis=-1)

    def recurrence(xc, pre_a, pre_x):
        tanh_a = jnp.tanh(0.5 * (pre_a + b_a_ref[...]))
        tanh_x = jnp.tanh(0.5 * (pre_x + b_x_ref[...]))
        a = jnp.exp2(tanh_a * half_rate + half_rate)
        half_xc = 0.5 * xc
        gated_in = tanh_x * half_xc + half_xc
        y = 1.0 - a * a
        a_s[...] = a
        b_s[...] = jnp.where(y > 0.0, y * lax.rsqrt(y), 0.0) * gated_in

        h_loc = jnp.zeros((sub, width), _F32)
        a_cum = jnp.ones((sub, width), _F32)
        for j in range(steps):
            rows = slice(j * sub, (j + 1) * sub)
            a_j = a_s[rows, :]
            h_loc = a_j * h_loc + b_s[rows, :]
            a_cum = a_j * a_cum
            h_s[rows, :] = h_loc
            ac_s[rows, :] = a_cum

        state = carry[...]
        starts = []
        for c in range(sub):
            starts.append(state)
            state = h_loc[c:c + 1, :] + a_cum[c:c + 1, :] * state
        carry[...] = state
        start = jnp.tile(jnp.concatenate(starts, axis=0), (steps, 1))
        return h_s[...] + ac_s[...] * start

    def out_projection(t, h):
        rows = slice(t * seq_tile, (t + 1) * seq_tile)
        gated = (h * _gelu_tanh(gate_pre[cur, t])).astype(_BF16)
        y = _dot(gated, w_out_ref[...]) + alpha * x_ref[rows, :]
        out_ref[rows, :] = _layer_norm(y, g_ref[...], b_ref[...])

    convolved = [conv(t) for t in range(n_tiles)]
    gates = [gate_projection(convolved[0])]
    for t in range(n_tiles):
        in_projection(x_next_ref, nxt, t)
        if t + 1 < n_tiles:
            gates.append(gate_projection(convolved[t + 1]))
        out_projection(t, recurrence(convolved[t], *gates[t]))


def _lru_layer(h, batch, seq, layer, w_in, conv_w, conv_b, w_gate, b_a, b_x, lam, w_out, g, b,
               alpha):
    m, d = h.shape
    width = w_out.shape[1]
    rows = LRU_TILES * STREAM_TILE
    steps_per_seq = seq // rows
    n_steps = m // rows
    halo = (CONV_WIDTH - 1) * V7X_SUBLANES
    tile_f32 = pltpu.VMEM((STREAM_TILE, width), _F32)
    row_spec = pl.BlockSpec((rows, d), lambda i: (i, 0))
    next_spec = pl.BlockSpec((rows, d), lambda i: (jnp.minimum(i + 1, n_steps - 1), 0))
    return pl.pallas_call(
        functools.partial(_lru_kernel, alpha=alpha, steps_per_seq=steps_per_seq),
        grid=(n_steps,),
        in_specs=[row_spec, next_spec, _resident_layer(w_in.shape, layer),
                  _resident(conv_w.shape),
                  _resident(conv_b.shape), _resident_layer(w_gate.shape, layer),
                  _resident(b_a.shape), _resident(b_x.shape), _resident(lam.shape),
                  _resident_layer(w_out.shape, layer), _resident(g.shape), _resident(b.shape)],
        out_specs=row_spec,
        out_shape=jax.ShapeDtypeStruct((m, d), _F32),
        scratch_shapes=[pltpu.VMEM((2, LRU_TILES, halo + STREAM_TILE, width), _F32),
                        pltpu.VMEM((2, LRU_TILES, STREAM_TILE, width), _F32),
                        pltpu.VMEM((halo, width), _F32),
                        tile_f32,
                        tile_f32,
                        tile_f32,
                        tile_f32,
                        pltpu.VMEM((1, width), _F32)],
        compiler_params=_compiler_params(("arbitrary",)),
        name="rglru_block_ln",
    )(h, h, w_in, conv_w, conv_b, w_gate, b_a, b_x, lam, w_out, g, b)


def kernel(x, attn_w_qkv, attn_w_o, lru_w_in, lru_conv_w, lru_conv_b, lru_w_a, lru_b_a,
           lru_w_x, lru_b_x, lru_lambda, lru_w_out, ffn_w_in, ffn_w_out, ln_g, ln_b):
    batch, seq, d = x.shape
    depth = ffn_w_in.shape[0]
    assert seq % MOBA_BLOCK == 0 and (batch * seq) % ROW_TILE == 0
    assert d % N_HEADS == 0 and ROW_TILE % MOBA_BLOCK == 0 and seq % ROW_TILE == 0
    alpha = (2 * depth) ** 0.25
    head_dim = d // N_HEADS
    n_blk = seq // MOBA_BLOCK
    slopes = jnp.exp2(-8.0 * (jnp.arange(N_HEADS, dtype=_F32) + 1.0) / N_HEADS)
    row = lambda v: v.reshape(1, -1)

    w_qk = attn_w_qkv[:, :, 0:2 * d].astype(_BF16)
    w_vt = jnp.swapaxes(attn_w_qkv[:, :, 2 * d:3 * d], 1, 2).astype(_BF16)
    w_o = attn_w_o.astype(_BF16)
    w_lru_in, w_lru_out = lru_w_in.astype(_BF16), lru_w_out.astype(_BF16)
    w_gate = jnp.concatenate([lru_w_a, lru_w_x], axis=-1).astype(_BF16)
    w_ffn_in, w_ffn_out = ffn_w_in.astype(_BF16), ffn_w_out.astype(_BF16)

    h = _to_stream_order(x)
    for layer in range(depth):
        j = layer // 2
        g0, b0 = row(ln_g[layer, 0]), row(ln_b[layer, 0])
        g1, b1 = row(ln_g[layer, 1]), row(ln_b[layer, 1])
        if layer % 2 == 0:
            qk, vt, kmean = _qkv_projection(h, w_qk, w_vt, j, head_dim ** -0.5 * LOG2_E,
                                            batch, seq)
            o = _moba_attention(qk, vt, kmean.reshape(batch, n_blk, d), slopes, batch, seq)
            h = _proj_residual_ln(o, w_o, j, h, g0, b0, alpha)
        else:
            h = _lru_layer(h, batch, seq, j, w_lru_in, lru_conv_w[j], row(lru_conv_b[j]),
                           w_gate, row(lru_b_a[j]), row(lru_b_x[j]), row(lru_lambda[j]),
                           w_lru_out, g0, b0, alpha)
        h = _ffn_layer(h, w_ffn_in, w_ffn_out, layer, g1, b1, alpha)
    return _from_stream_order(h, batch, seq)
```

```python
import functools

import jax
import jax.numpy as jnp
from jax import lax
from jax.experimental import pallas as pl
from jax.experimental.pallas import tpu as pltpu

N_HEADS = 8
MOBA_BLOCK = 256
MOBA_TOPK = 3
NEG_INF = -1e30
LRU_BLOCKS = 8
CONV_WIDTH = 4
LRU_C = 8.0
LN_EPS = 1e-5
LOG2_E = 1.4426950408889634

V7X_SUBLANES = 8
V7X_LANES = 128
V7X_VMEM_LIMIT_BYTES = 56 * 1024 * 1024

ROW_TILE = 512
OUT_PROJ_ROW_TILE = 1024
STREAM_TILE = MOBA_BLOCK
FFN_CHUNK = 256
FFN_ROW_TILE = 1024
FFN_SUB_ROWS = 512
LRU_TILES = 2
MOBA_HEADS_PER_STEP = 4

_F32 = jnp.float32
_BF16 = jnp.bfloat16
_NT = (((1,), (1,)), ((), ()))


def _compiler_params(semantics):
    return pltpu.CompilerParams(dimension_semantics=semantics,
                                vmem_limit_bytes=V7X_VMEM_LIMIT_BYTES)


def _resident(shape):
    zeros = (0,) * len(shape)
    return pl.BlockSpec(shape, lambda *_: zeros, pipeline_mode=pl.Buffered(1))


def _resident_layer(stacked_shape, layer):
    index = (layer,) + (0,) * (len(stacked_shape) - 1)
    return pl.BlockSpec((None,) + tuple(stacked_shape[1:]), lambda *_: index,
                        pipeline_mode=pl.Buffered(1))


def _to_stream_order(x):
    batch, seq, d = x.shape
    steps = STREAM_TILE // V7X_SUBLANES
    x = x.reshape(batch, seq // STREAM_TILE, V7X_SUBLANES, steps, d)
    return x.transpose(0, 1, 3, 2, 4).reshape(batch * seq, d)


def _from_stream_order(h, batch, seq):
    d = h.shape[-1]
    steps = STREAM_TILE // V7X_SUBLANES
    h = h.reshape(batch, seq // STREAM_TILE, steps, V7X_SUBLANES, d)
    return h.transpose(0, 1, 3, 2, 4).reshape(batch, seq, d)


def _tile_time(idx):
    steps = STREAM_TILE // V7X_SUBLANES
    shift = V7X_SUBLANES.bit_length() - 1
    return (idx & (V7X_SUBLANES - 1)) * steps + (idx >> shift)


def _layer_norm(y, g, b):
    mu = jnp.mean(y, axis=-1, keepdims=True)
    yc = y - mu
    var = jnp.mean(yc * yc, axis=-1, keepdims=True)
    return yc * lax.rsqrt(var + LN_EPS) * g + b


def _dot(a, b):
    return jnp.dot(a, b, preferred_element_type=_F32)


def _qkv_kernel(x_ref, w_qk_ref, w_vt_ref, qk_ref, vt_ref, kmean_ref, *, d_model, scale):
    xb = x_ref[...].astype(_BF16)
    rows = xb.shape[0]
    chunk = 512
    for c in range(0, 2 * d_model, chunk):
        acc = _dot(xb, w_qk_ref[:, c:c + chunk])
        if c < d_model:
            acc = acc * scale
        else:
            for r in range(rows // MOBA_BLOCK):
                blk = acc[r * MOBA_BLOCK:(r + 1) * MOBA_BLOCK, :]
                kmean_ref[0, r:r + 1, c - d_model:c - d_model + chunk] = (
                    jnp.sum(blk, axis=0, keepdims=True) * (1.0 / MOBA_BLOCK))
        qk_ref[:, c:c + chunk] = acc.astype(_BF16)
    for c in range(0, d_model, chunk):
        vt_ref[c:c + chunk, :] = lax.dot_general(
            w_vt_ref[c:c + chunk, :], xb, _NT, preferred_element_type=_F32).astype(_BF16)


def _qkv_projection(h, w_qk, w_vt, layer, scale, batch, seq):
    m, d = h.shape
    blocks_per_tile = ROW_TILE // MOBA_BLOCK
    tiles_per_seq = seq // ROW_TILE
    return pl.pallas_call(
        functools.partial(_qkv_kernel, d_model=d, scale=scale),
        grid=(m // ROW_TILE,),
        in_specs=[pl.BlockSpec((ROW_TILE, d), lambda i: (i, 0)),
                  _resident_layer(w_qk.shape, layer), _resident_layer(w_vt.shape, layer)],
        out_specs=[pl.BlockSpec((ROW_TILE, 2 * d), lambda i: (i, 0)),
                   pl.BlockSpec((None, d, ROW_TILE),
                                lambda i: (i // tiles_per_seq, 0, i % tiles_per_seq)),
                   pl.BlockSpec((1, blocks_per_tile, d), lambda i: (i, 0, 0))],
        out_shape=[jax.ShapeDtypeStruct((m, 2 * d), _BF16),
                   jax.ShapeDtypeStruct((batch, d, seq), _BF16),
                   jax.ShapeDtypeStruct((m // ROW_TILE, blocks_per_tile, d), _F32)],
        compiler_params=_compiler_params(("arbitrary",)),
        name="qkv_projection",
    )(h, w_qk, w_vt)


def _split_bf16(x, parts):
    terms = []
    for _ in range(parts):
        t = x.astype(_BF16)
        terms.append(t)
        x = x - t.astype(_F32)
    return terms


def _moba_kernel(slopes_ref, q_ref, k_ref, vt_ref, kmean_ref, o_ref,
                 k_aug, vt_aug, causal, s_scr, p_scr, *, n_blk):
    blk = MOBA_BLOCK
    n_heads, seq, dh2 = k_aug.shape
    dh = dh2 // 2
    bias_lanes = 3
    head_cols = lambda hd: slice(hd * dh, (hd + 1) * dh)

    @pl.when((pl.program_id(0) == 0) & (pl.program_id(1) == 0))
    def _():
        key_time = _tile_time(lax.broadcasted_iota(jnp.int32, (seq, dh), 0) & (blk - 1))
        key_lane = lax.broadcasted_iota(jnp.int32, (seq, dh), 1)
        extra = vt_aug.shape[1] - dh
        ones_row = lax.broadcasted_iota(jnp.int32, (extra, seq), 0) == 0
        for hd in range(n_heads):
            k_aug[hd, :, dh:2 * dh] = jnp.where(key_lane < bias_lanes, key_time, 0).astype(_BF16)
            vt_aug[hd, dh:dh + extra, :] = jnp.where(ones_row, 1.0, 0.0).astype(_BF16)
        key_t = _tile_time(lax.broadcasted_iota(jnp.int32, (blk, blk), 0))
        query_t = _tile_time(lax.broadcasted_iota(jnp.int32, (blk, blk), 1))
        causal[...] = jnp.where(key_t > query_t, NEG_INF, 0.0)

    lane = lax.broadcasted_iota(jnp.int32, (blk, dh), 1)
    block_id = lax.broadcasted_iota(jnp.int32, (n_blk, blk), 0)
    slope2, q_bias, km_split = [], [], []
    for hd in range(n_heads):
        k_aug[hd, :, 0:dh] = k_ref[:, head_cols(hd)]
        vt_aug[hd, 0:dh, :] = vt_ref[head_cols(hd), :]
        slope2.append(slopes_ref[pl.program_id(1) * n_heads + hd] * LOG2_E)
        bias = jnp.zeros((blk, dh), _F32)
        for i, term in enumerate(_split_bf16(jnp.full((blk, dh), slope2[hd], _F32), bias_lanes)):
            bias = jnp.where(lane == i, term.astype(_F32), bias)
        q_bias.append(bias.astype(_BF16))
        km_split.append(_split_bf16(kmean_ref[:, head_cols(hd)], 2))

    def score_stage(hd, qi):
        width = (qi + 1) * blk
        q = q_ref[qi * blk:(qi + 1) * blk, head_cols(hd)]
        s_scr[hd, qi % 2, 0:width, :] = lax.dot_general(
            k_aug[hd, 0:width, :], jnp.concatenate([q, q_bias[hd]], axis=-1), _NT,
            preferred_element_type=_F32)
        if qi <= MOBA_TOPK:
            return None
        km_hi, km_lo = km_split[hd]
        gate = (lax.dot_general(km_hi, q, _NT, preferred_element_type=_F32)
                + lax.dot_general(km_lo, q, _NT, preferred_element_type=_F32))
        ahead = jnp.zeros((n_blk, blk), _F32)
        for m in range(qi):
            g_m = gate[m:m + 1, :]
            wins = (g_m > gate) | ((g_m == gate) & (block_id > m))
            ahead = ahead + jnp.where(wins, 1.0, 0.0)
        return (ahead < float(MOBA_TOPK)) & (block_id < qi)

    def softmax_stage(hd, qi, picked):
        s_buf, p_buf = s_scr.at[hd, qi % 2], p_scr.at[hd, qi % 2]

        def block_scores(n):
            t = s_buf[n * blk:(n + 1) * blk, :]
            return t + causal[...] if n == qi else t

        query_term = [slope2[hd] * float(-blk * (qi - n)) for n in range(qi)]
        if picked is not None:
            query_term = [jnp.where(picked[n:n + 1, :], query_term[n], NEG_INF)
                          for n in range(qi)]
        m_run = jnp.max(block_scores(qi), axis=0, keepdims=True)
        for n in range(qi):
            m_run = jnp.maximum(m_run,
                                jnp.max(block_scores(n), axis=0, keepdims=True) + query_term[n])
        for n in range(qi + 1):
            shift = m_run if n == qi else m_run - query_term[n]
            p_buf[n * blk:(n + 1) * blk, :] = jnp.exp2(block_scores(n) - shift).astype(_BF16)

    def output_stage(hd, qi):
        width = (qi + 1) * blk
        out = _dot(vt_aug[hd, :, 0:width], p_scr[hd, qi % 2, 0:width, :])
        o_ref[qi * blk:(qi + 1) * blk, head_cols(hd)] = (
            out[0:dh, :] / out[dh:dh + 1, :]).T.astype(o_ref.dtype)

    heads = range(n_heads)
    picked = [score_stage(hd, 0) for hd in heads]
    for qi in range(n_blk):
        picked_next = [score_stage(hd, qi + 1) if qi + 1 < n_blk else None for hd in heads]
        for hd in heads:
            softmax_stage(hd, qi, picked[hd])
        for hd in heads:
            output_stage(hd, qi)
        picked = picked_next


def _moba_attention(qk, vt, kmean, slopes, batch, seq):
    m, d2 = qk.shape
    d = d2 // 2
    dh = d // N_HEADS
    n_blk = seq // MOBA_BLOCK
    bf16_rows = 2 * V7X_SUBLANES
    hps = MOBA_HEADS_PER_STEP
    groups = N_HEADS // hps
    wide = hps * dh
    return pl.pallas_call(
        functools.partial(_moba_kernel, n_blk=n_blk),
        grid=(batch, groups),
        in_specs=[pl.BlockSpec(memory_space=pltpu.SMEM),
                  pl.BlockSpec((seq, wide), lambda b, g: (b, g)),
                  pl.BlockSpec((seq, wide), lambda b, g: (b, groups + g)),
                  pl.BlockSpec((None, wide, seq), lambda b, g: (b, g, 0)),
                  pl.BlockSpec((None, n_blk, wide), lambda b, g: (b, 0, g))],
        out_specs=pl.BlockSpec((seq, wide), lambda b, g: (b, g)),
        out_shape=jax.ShapeDtypeStruct((m, d), _BF16),
        scratch_shapes=[pltpu.VMEM((hps, seq, 2 * dh), _BF16),
                        pltpu.VMEM((hps, dh + bf16_rows, seq), _BF16),
                        pltpu.VMEM((MOBA_BLOCK, MOBA_BLOCK), _F32),
                        pltpu.VMEM((hps, 2, seq, MOBA_BLOCK), _F32),
                        pltpu.VMEM((hps, 2, seq, MOBA_BLOCK), _BF16)],
        compiler_params=_compiler_params(("arbitrary", "arbitrary")),
        name="moba_attention",
    )(slopes, qk, qk, vt, kmean)


def _proj_ln_kernel(a_ref, w_ref, res_ref, g_ref, b_ref, out_ref, *, alpha):
    y = _dot(a_ref[...], w_ref[...]) + alpha * res_ref[...]
    out_ref[...] = _layer_norm(y, g_ref[...], b_ref[...])


def _proj_residual_ln(a, w, layer, res, g, b, alpha):
    m, d = res.shape
    k = a.shape[1]
    row_spec = lambda width: pl.BlockSpec((OUT_PROJ_ROW_TILE, width), lambda i: (i, 0))
    return pl.pallas_call(
        functools.partial(_proj_ln_kernel, alpha=alpha),
        grid=(m // OUT_PROJ_ROW_TILE,),
        in_specs=[row_spec(k), _resident_layer(w.shape, layer), row_spec(d),
                  _resident((1, d)), _resident((1, d))],
        out_specs=row_spec(d),
        out_shape=jax.ShapeDtypeStruct((m, d), _F32),
        compiler_params=_compiler_params(("arbitrary",)),
        name="out_projection_ln",
    )(a, w, res, g, b)


def _ffn_kernel(x_ref, w_in_ref, w_out_ref, g_ref, b_ref, out_ref, *, hidden, alpha):
    for r in range(0, x_ref.shape[0], FFN_SUB_ROWS):
        x = x_ref[r:r + FFN_SUB_ROWS, :]
        xb = x.astype(_BF16)
        acc = alpha * x
        for c in range(0, hidden, FFN_CHUNK):
            gate = _dot(xb, w_in_ref[:, c:c + FFN_CHUNK])
            up = _dot(xb, w_in_ref[:, hidden + c:hidden + c + FFN_CHUNK])
            act = (gate * jax.nn.sigmoid(gate) * up).astype(_BF16)
            acc = acc + _dot(act, w_out_ref[c:c + FFN_CHUNK, :])
        out_ref[r:r + FFN_SUB_ROWS, :] = _layer_norm(acc, g_ref[...], b_ref[...])


def _ffn_layer(h, w_in, w_out, layer, g, b, alpha):
    m, d = h.shape
    hidden = w_out.shape[1]
    row_spec = pl.BlockSpec((FFN_ROW_TILE, d), lambda i: (i, 0))
    return pl.pallas_call(
        functools.partial(_ffn_kernel, hidden=hidden, alpha=alpha),
        grid=(m // FFN_ROW_TILE,),
        in_specs=[row_spec, _resident_layer(w_in.shape, layer),
                  _resident_layer(w_out.shape, layer),
                  _resident((1, d)), _resident((1, d))],
        out_specs=row_spec,
        out_shape=jax.ShapeDtypeStruct((m, d), _F32),
        compiler_params=_compiler_params(("arbitrary",)),
        name="swiglu_ffn_ln",
    )(h, w_in, w_out, g, b)


def _gelu_tanh(x):
    inner = x * (0.7978845608028654 + (0.7978845608028654 * 0.044715) * (x * x))
    half_x = 0.5 * x
    return half_x * jnp.tanh(inner) + half_x


def _softplus(z):
    return jnp.maximum(z, 0.0) + jnp.log1p(jnp.exp(-jnp.abs(z)))


def _lru_kernel(x_ref, x_next_ref, w_in_ref, conv_w_ref, conv_b_ref, w_gate_ref, b_a_ref,
                b_x_ref, lam_ref, w_out_ref, g_ref, b_ref, out_ref,
                xbuf, gate_pre, tail, a_s, b_s, h_s, ac_s, carry, *, alpha, steps_per_seq):
    sub = V7X_SUBLANES
    seq_tile, width = a_s.shape
    n_tiles = x_ref.shape[0] // seq_tile
    steps = seq_tile // sub
    halo = (CONV_WIDTH - 1) * sub
    group = width // LRU_BLOCKS
    step = pl.program_id(0)
    cur, nxt = step % 2, (step + 1) % 2

    def in_projection(src_ref, slot, t):
        xb = src_ref[t * seq_tile:(t + 1) * seq_tile, :].astype(_BF16)
        xbuf[slot, t, halo:halo + seq_tile, :] = _dot(xb, w_in_ref[:, 0:width])
        gate_pre[slot, t] = _dot(xb, w_in_ref[:, width:2 * width])

    @pl.when(step == 0)
    def _():
        for t in range(n_tiles):
            in_projection(x_ref, 0, t)

    @pl.when(step % steps_per_seq == 0)
    def _():
        tail[...] = jnp.zeros_like(tail)
        carry[...] = jnp.zeros_like(carry)

    half_rate = (-0.5 * LRU_C * LOG2_E) * _softplus(-lam_ref[...])

    def conv(t):
        first_sublane = lax.broadcasted_iota(jnp.int32, (sub, width), 0) == 0
        for g in range(CONV_WIDTH - 1):
            rows = slice(g * sub, (g + 1) * sub)
            last = xbuf[cur, t, seq_tile + g * sub:seq_tile + (g + 1) * sub, :]
            xbuf[cur, t, rows, :] = jnp.where(first_sublane,
                                              pltpu.roll(tail[rows, :], 1, axis=0),
                                              pltpu.roll(last, 1, axis=0))
            tail[rows, :] = last
        xc = conv_b_ref[...]
        for tap in range(CONV_WIDTH):
            start = halo - (CONV_WIDTH - 1 - tap) * sub
            xc = xc + xbuf[cur, t, start:start + seq_tile, :] * conv_w_ref[tap:tap + 1, :]
        return xc

    def gate_projection(xc):
        xcb = xc.astype(_BF16)
        pre_a, pre_x = [], []
        for gidx in range(LRU_BLOCKS):
            both = _dot(xcb[:, gidx * group:(gidx + 1) * group], w_gate_ref[gidx])
            pre_a.append(both[:, 0:group])
            pre_x.append(both[:, group:2 * group])
        return jnp.concatenate(pre_a, axis=-1), jnp.concatenate(pre_x, axis=-1)

    def recurrence(xc, pre_a, pre_x):
        tanh_a = jnp.tanh(0.5 * (pre_a + b_a_ref[...]))
        tanh_x = jnp.tanh(0.5 * (pre_x + b_x_ref[...]))
        a = jnp.exp2(tanh_a * half_rate + half_rate)
        half_xc = 0.5 * xc
        gated_in = tanh_x * half_xc + half_xc
        y = 1.0 - a * a
        a_s[...] = a
        b_s[...] = jnp.where(y > 0.0, y * lax.rsqrt(y), 0.0) * gated_in

        h_loc = jnp.zeros((sub, width), _F32)
        a_cum = jnp.ones((sub, width), _F32)
        for j in range(steps):
            rows = slice(j * sub, (j + 1) * sub)
            a_j = a_s[rows, :]
            h_loc = a_j * h_loc + b_s[rows, :]
            a_cum = a_j * a_cum
            h_s[rows, :] = h_loc
            ac_s[rows, :] = a_cum

        state = carry[...]
        starts = []
        for c in range(sub):
            starts.append(state)
            state = h_loc[c:c + 1, :] + a_cum[c:c + 1, :] * state
        carry[...] = state
        start = jnp.tile(jnp.concatenate(starts, axis=0), (steps, 1))
        return h_s[...] + ac_s[...] * start

    def out_projection(t, h):
        rows = slice(t * seq_tile, (t + 1) * seq_tile)
        gated = (h * _gelu_tanh(gate_pre[cur, t])).astype(_BF16)
        y = _dot(gated, w_out_ref[...]) + alpha * x_ref[rows, :]
        out_ref[rows, :] = _layer_norm(y, g_ref[...], b_ref[...])

    convolved = [conv(t) for t in range(n_tiles)]
    gates = [gate_projection(convolved[0])]
    for t in range(n_tiles):
        in_projection(x_next_ref, nxt, t)
        if t + 1 < n_tiles:
            gates.append(gate_projection(convolved[t + 1]))
        out_projection(t, recurrence(convolved[t], *gates[t]))


def _lru_layer(h, batch, seq, layer, w_in, conv_w, conv_b, w_gate, b_a, b_x, lam, w_out, g, b,
               alpha):
    m, d = h.shape
    width = w_out.shape[1]
    rows = LRU_TILES * STREAM_TILE
    steps_per_seq = seq // rows
    n_steps = m // rows
    halo = (CONV_WIDTH - 1) * V7X_SUBLANES
    tile_f32 = pltpu.VMEM((STREAM_TILE, width), _F32)
    row_spec = pl.BlockSpec((rows, d), lambda i: (i, 0))
    next_spec = pl.BlockSpec((rows, d), lambda i: (jnp.minimum(i + 1, n_steps - 1), 0))
    return pl.pallas_call(
        functools.partial(_lru_kernel, alpha=alpha, steps_per_seq=steps_per_seq),
        grid=(n_steps,),
        in_specs=[row_spec, next_spec, _resident_layer(w_in.shape, layer),
                  _resident(conv_w.shape),
                  _resident(conv_b.shape), _resident_layer(w_gate.shape, layer),
                  _resident(b_a.shape), _resident(b_x.shape), _resident(lam.shape),
                  _resident_layer(w_out.shape, layer), _resident(g.shape), _resident(b.shape)],
        out_specs=row_spec,
        out_shape=jax.ShapeDtypeStruct((m, d), _F32),
        scratch_shapes=[pltpu.VMEM((2, LRU_TILES, halo + STREAM_TILE, width), _F32),
                        pltpu.VMEM((2, LRU_TILES, STREAM_TILE, width), _F32),
                        pltpu.VMEM((halo, width), _F32),
                        tile_f32,
                        tile_f32,
                        tile_f32,
                        tile_f32,
                        pltpu.VMEM((1, width), _F32)],
        compiler_params=_compiler_params(("arbitrary",)),
        name="rglru_block_ln",
    )(h, h, w_in, conv_w, conv_b, w_gate, b_a, b_x, lam, w_out, g, b)


def kernel(x, attn_w_qkv, attn_w_o, lru_w_in, lru_conv_w, lru_conv_b, lru_w_a, lru_b_a,
           lru_w_x, lru_b_x, lru_lambda, lru_w_out, ffn_w_in, ffn_w_out, ln_g, ln_b):
    batch, seq, d = x.shape
    depth = ffn_w_in.shape[0]
    rows = batch * seq
    assert d % N_HEADS == 0 and N_HEADS % MOBA_HEADS_PER_STEP == 0
    assert seq % ROW_TILE == 0 and ROW_TILE % MOBA_BLOCK == 0
    assert seq % (LRU_TILES * STREAM_TILE) == 0
    assert rows % FFN_ROW_TILE == 0 and FFN_ROW_TILE % FFN_SUB_ROWS == 0
    assert rows % OUT_PROJ_ROW_TILE == 0 and ffn_w_out.shape[1] % FFN_CHUNK == 0
    alpha = (2 * depth) ** 0.25
    head_dim = d // N_HEADS
    n_blk = seq // MOBA_BLOCK
    slopes = jnp.exp2(-8.0 * (jnp.arange(N_HEADS, dtype=_F32) + 1.0) / N_HEADS)
    row = lambda v: v.reshape(1, -1)

    w_qk = attn_w_qkv[:, :, 0:2 * d].astype(_BF16)
    w_vt = jnp.swapaxes(attn_w_qkv[:, :, 2 * d:3 * d], 1, 2).astype(_BF16)
    w_o = attn_w_o.astype(_BF16)
    w_lru_in, w_lru_out = lru_w_in.astype(_BF16), lru_w_out.astype(_BF16)
    w_gate = jnp.concatenate([lru_w_a, lru_w_x], axis=-1).astype(_BF16)
    w_ffn_in, w_ffn_out = ffn_w_in.astype(_BF16), ffn_w_out.astype(_BF16)

    h = _to_stream_order(x)
    for layer in range(depth):
        j = layer // 2
        g0, b0 = row(ln_g[layer, 0]), row(ln_b[layer, 0])
        g1, b1 = row(ln_g[layer, 1]), row(ln_b[layer, 1])
        if layer % 2 == 0:
            qk, vt, kmean = _qkv_projection(h, w_qk, w_vt, j, head_dim ** -0.5 * LOG2_E,
                                            batch, seq)
            o = _moba_attention(qk, vt, kmean.reshape(batch, n_blk, d), slopes, batch, seq)
            h = _proj_residual_ln(o, w_o, j, h, g0, b0, alpha)
        else:
            h = _lru_layer(h, batch, seq, j, w_lru_in, lru_conv_w[j], row(lru_conv_b[j]),
                           w_gate, row(lru_b_a[j]), row(lru_b_x[j]), row(lru_lambda[j]),
                           w_lru_out, g0, b0, alpha)
        h = _ffn_layer(h, w_ffn_in, w_ffn_out, layer, g1, b1, alpha)
    return _from_stream_order(h, batch, seq)
```

```python
import functools

import jax
import jax.numpy as jnp
from jax import lax
from jax.experimental import pallas as pl
from jax.experimental.pallas import tpu as pltpu

N_HEADS = 8
MOBA_BLOCK = 256
MOBA_TOPK = 3
NEG_INF = -1e30
LRU_BLOCKS = 8
CONV_WIDTH = 4
LRU_C = 8.0
LN_EPS = 1e-5
LOG2_E = 1.4426950408889634

V7X_SUBLANES = 8
V7X_LANES = 128
V7X_VMEM_LIMIT_BYTES = 56 * 1024 * 1024

ROW_TILE = 512
OUT_PROJ_ROW_TILE = 1024
STREAM_TILE = MOBA_BLOCK
FFN_CHUNK = 256
FFN_ROW_TILE = 1024
FFN_SUB_ROWS = 512
LRU_TILES = 2
MOBA_HEADS_PER_STEP = 4

_F32 = jnp.float32
_BF16 = jnp.bfloat16
_NT = (((1,), (1,)), ((), ()))


def _compiler_params(semantics):
    return pltpu.CompilerParams(dimension_semantics=semantics,
                                vmem_limit_bytes=V7X_VMEM_LIMIT_BYTES)


def _resident(shape):
    zeros = (0,) * len(shape)
    return pl.BlockSpec(shape, lambda *_: zeros, pipeline_mode=pl.Buffered(1))


def _resident_layer(stacked_shape, layer):
    index = (layer,) + (0,) * (len(stacked_shape) - 1)
    return pl.BlockSpec((None,) + tuple(stacked_shape[1:]), lambda *_: index,
                        pipeline_mode=pl.Buffered(1))


def _to_stream_order(x):
    batch, seq, d = x.shape
    steps = STREAM_TILE // V7X_SUBLANES
    x = x.reshape(batch, seq // STREAM_TILE, V7X_SUBLANES, steps, d)
    return x.transpose(0, 1, 3, 2, 4).reshape(batch * seq, d)


def _from_stream_order(h, batch, seq):
    d = h.shape[-1]
    steps = STREAM_TILE // V7X_SUBLANES
    h = h.reshape(batch, seq // STREAM_TILE, steps, V7X_SUBLANES, d)
    return h.transpose(0, 1, 3, 2, 4).reshape(batch, seq, d)


def _tile_time(idx):
    steps = STREAM_TILE // V7X_SUBLANES
    shift = V7X_SUBLANES.bit_length() - 1
    return (idx & (V7X_SUBLANES - 1)) * steps + (idx >> shift)


def _layer_norm(y, g, b):
    mu = jnp.mean(y, axis=-1, keepdims=True)
    yc = y - mu
    var = jnp.mean(yc * yc, axis=-1, keepdims=True)
    return yc * lax.rsqrt(var + LN_EPS) * g + b


def _dot(a, b):
    return jnp.dot(a, b, preferred_element_type=_F32)


def _qkv_kernel(x_ref, w_ref, qk_ref, vt_ref, kmean_ref, *, d_model, scale):
    xb = x_ref[...].astype(_BF16)
    rows = xb.shape[0]
    chunk = 512
    for c in range(0, 3 * d_model, chunk):
        acc = _dot(xb, w_ref[:, c:c + chunk])
        if c >= 2 * d_model:
            vt_ref[c - 2 * d_model:c - 2 * d_model + chunk, :] = acc.T.astype(_BF16)
            continue
        if c < d_model:
            acc = acc * scale
        else:
            for r in range(rows // MOBA_BLOCK):
                blk = acc[r * MOBA_BLOCK:(r + 1) * MOBA_BLOCK, :]
                kmean_ref[0, r:r + 1, c - d_model:c - d_model + chunk] = (
                    jnp.sum(blk, axis=0, keepdims=True) * (1.0 / MOBA_BLOCK))
        qk_ref[:, c:c + chunk] = acc.astype(_BF16)


def _qkv_projection(h, w_qkv, layer, scale, batch, seq):
    m, d = h.shape
    blocks_per_tile = ROW_TILE // MOBA_BLOCK
    tiles_per_seq = seq // ROW_TILE
    return pl.pallas_call(
        functools.partial(_qkv_kernel, d_model=d, scale=scale),
        grid=(m // ROW_TILE,),
        in_specs=[pl.BlockSpec((ROW_TILE, d), lambda i: (i, 0)),
                  _resident_layer(w_qkv.shape, layer)],
        out_specs=[pl.BlockSpec((ROW_TILE, 2 * d), lambda i: (i, 0)),
                   pl.BlockSpec((None, d, ROW_TILE),
                                lambda i: (i // tiles_per_seq, 0, i % tiles_per_seq)),
                   pl.BlockSpec((1, blocks_per_tile, d), lambda i: (i, 0, 0))],
        out_shape=[jax.ShapeDtypeStruct((m, 2 * d), _BF16),
                   jax.ShapeDtypeStruct((batch, d, seq), _BF16),
                   jax.ShapeDtypeStruct((m // ROW_TILE, blocks_per_tile, d), _F32)],
        compiler_params=_compiler_params(("arbitrary",)),
        name="qkv_projection",
    )(h, w_qkv)


def _split_bf16(x, parts):
    terms = []
    for _ in range(parts):
        t = x.astype(_BF16)
        terms.append(t)
        x = x - t.astype(_F32)
    return terms


def _moba_kernel(slopes_ref, q_ref, k_ref, vt_ref, kmean_ref, o_ref,
                 k_aug, vt_aug, causal, s_scr, p_scr, *, n_blk):
    blk = MOBA_BLOCK
    n_heads, seq, dh2 = k_aug.shape
    dh = dh2 // 2
    bias_lanes = 3
    head_cols = lambda hd: slice(hd * dh, (hd + 1) * dh)

    @pl.when((pl.program_id(0) == 0) & (pl.program_id(1) == 0))
    def _():
        key_time = _tile_time(lax.broadcasted_iota(jnp.int32, (seq, dh), 0) & (blk - 1))
        key_lane = lax.broadcasted_iota(jnp.int32, (seq, dh), 1)
        extra = vt_aug.shape[1] - dh
        ones_row = lax.broadcasted_iota(jnp.int32, (extra, seq), 0) == 0
        for hd in range(n_heads):
            k_aug[hd, :, dh:2 * dh] = jnp.where(key_lane < bias_lanes, key_time, 0).astype(_BF16)
            vt_aug[hd, dh:dh + extra, :] = jnp.where(ones_row, 1.0, 0.0).astype(_BF16)
        key_t = _tile_time(lax.broadcasted_iota(jnp.int32, (blk, blk), 0))
        query_t = _tile_time(lax.broadcasted_iota(jnp.int32, (blk, blk), 1))
        causal[...] = jnp.where(key_t > query_t, NEG_INF, 0.0)

    lane = lax.broadcasted_iota(jnp.int32, (blk, dh), 1)
    block_id = lax.broadcasted_iota(jnp.int32, (n_blk, blk), 0)
    slope2, q_bias, km_split = [], [], []
    for hd in range(n_heads):
        k_aug[hd, :, 0:dh] = k_ref[:, head_cols(hd)]
        vt_aug[hd, 0:dh, :] = vt_ref[head_cols(hd), :]
        slope2.append(slopes_ref[pl.program_id(1) * n_heads + hd] * LOG2_E)
        bias = jnp.zeros((blk, dh), _F32)
        for i, term in enumerate(_split_bf16(jnp.full((blk, dh), slope2[hd], _F32), bias_lanes)):
            bias = jnp.where(lane == i, term.astype(_F32), bias)
        q_bias.append(bias.astype(_BF16))
        km_split.append(_split_bf16(kmean_ref[:, head_cols(hd)], 2))

    def score_stage(hd, qi):
        width = (qi + 1) * blk
        q = q_ref[qi * blk:(qi + 1) * blk, head_cols(hd)]
        s_scr[hd, qi % 2, 0:width, :] = lax.dot_general(
            k_aug[hd, 0:width, :], jnp.concatenate([q, q_bias[hd]], axis=-1), _NT,
            preferred_element_type=_F32)
        if qi <= MOBA_TOPK:
            return None
        km_hi, km_lo = km_split[hd]
        gate = (lax.dot_general(km_hi, q, _NT, preferred_element_type=_F32)
                + lax.dot_general(km_lo, q, _NT, preferred_element_type=_F32))
        ahead = jnp.zeros((n_blk, blk), _F32)
        for m in range(qi):
            g_m = gate[m:m + 1, :]
            wins = (g_m > gate) | ((g_m == gate) & (block_id > m))
            ahead = ahead + jnp.where(wins, 1.0, 0.0)
        return (ahead < float(MOBA_TOPK)) & (block_id < qi)

    def softmax_stage(hd, qi, picked):
        s_buf, p_buf = s_scr.at[hd, qi % 2], p_scr.at[hd, qi % 2]

        def block_scores(n):
            t = s_buf[n * blk:(n + 1) * blk, :]
            return t + causal[...] if n == qi else t

        query_term = [slope2[hd] * float(-blk * (qi - n)) for n in range(qi)]
        if picked is not None:
            query_term = [jnp.where(picked[n:n + 1, :], query_term[n], NEG_INF)
                          for n in range(qi)]
        m_run = jnp.max(block_scores(qi), axis=0, keepdims=True)
        for n in range(qi):
            m_run = jnp.maximum(m_run,
                                jnp.max(block_scores(n), axis=0, keepdims=True) + query_term[n])
        for n in range(qi + 1):
            shift = m_run if n == qi else m_run - query_term[n]
            p_buf[n * blk:(n + 1) * blk, :] = jnp.exp2(block_scores(n) - shift).astype(_BF16)

    def output_stage(hd, qi):
        width = (qi + 1) * blk
        out = _dot(vt_aug[hd, :, 0:width], p_scr[hd, qi % 2, 0:width, :])
        o_ref[qi * blk:(qi + 1) * blk, head_cols(hd)] = (
            out[0:dh, :] / out[dh:dh + 1, :]).T.astype(o_ref.dtype)

    heads = range(n_heads)
    picked = [score_stage(hd, 0) for hd in heads]
    for qi in range(n_blk):
        picked_next = [score_stage(hd, qi + 1) if qi + 1 < n_blk else None for hd in heads]
        for hd in heads:
            softmax_stage(hd, qi, picked[hd])
        for hd in heads:
            output_stage(hd, qi)
        picked = picked_next


def _moba_attention(qk, vt, kmean, slopes, batch, seq):
    m, d2 = qk.shape
    d = d2 // 2
    dh = d // N_HEADS
    n_blk = seq // MOBA_BLOCK
    bf16_rows = 2 * V7X_SUBLANES
    hps = MOBA_HEADS_PER_STEP
    groups = N_HEADS // hps
    wide = hps * dh
    return pl.pallas_call(
        functools.partial(_moba_kernel, n_blk=n_blk),
        grid=(batch, groups),
        in_specs=[pl.BlockSpec(memory_space=pltpu.SMEM),
                  pl.BlockSpec((seq, wide), lambda b, g: (b, g)),
                  pl.BlockSpec((seq, wide), lambda b, g: (b, groups + g)),
                  pl.BlockSpec((None, wide, seq), lambda b, g: (b, g, 0)),
                  pl.BlockSpec((None, n_blk, wide), lambda b, g: (b, 0, g))],
        out_specs=pl.BlockSpec((seq, wide), lambda b, g: (b, g)),
        out_shape=jax.ShapeDtypeStruct((m, d), _BF16),
        scratch_shapes=[pltpu.VMEM((hps, seq, 2 * dh), _BF16),
                        pltpu.VMEM((hps, dh + bf16_rows, seq), _BF16),
                        pltpu.VMEM((MOBA_BLOCK, MOBA_BLOCK), _F32),
                        pltpu.VMEM((hps, 2, seq, MOBA_BLOCK), _F32),
                        pltpu.VMEM((hps, 2, seq, MOBA_BLOCK), _BF16)],
        compiler_params=_compiler_params(("arbitrary", "arbitrary")),
        name="moba_attention",
    )(slopes, qk, qk, vt, kmean)


def _proj_ln_kernel(a_ref, w_ref, res_ref, g_ref, b_ref, out_ref, *, alpha):
    y = _dot(a_ref[...], w_ref[...]) + alpha * res_ref[...]
    out_ref[...] = _layer_norm(y, g_ref[...], b_ref[...])


def _proj_residual_ln(a, w, layer, res, g, b, alpha):
    m, d = res.shape
    k = a.shape[1]
    row_spec = lambda width: pl.BlockSpec((OUT_PROJ_ROW_TILE, width), lambda i: (i, 0))
    return pl.pallas_call(
        functools.partial(_proj_ln_kernel, alpha=alpha),
        grid=(m // OUT_PROJ_ROW_TILE,),
        in_specs=[row_spec(k), _resident_layer(w.shape, layer), row_spec(d),
                  _resident((1, d)), _resident((1, d))],
        out_specs=row_spec(d),
        out_shape=jax.ShapeDtypeStruct((m, d), _F32),
        compiler_params=_compiler_params(("arbitrary",)),
        name="out_projection_ln",
    )(a, w, res, g, b)


def _ffn_kernel(x_ref, w_in_ref, w_out_ref, g_ref, b_ref, out_ref, *, hidden, alpha):
    for r in range(0, x_ref.shape[0], FFN_SUB_ROWS):
        x = x_ref[r:r + FFN_SUB_ROWS, :]
        xb = x.astype(_BF16)
        acc = alpha * x
        for c in range(0, hidden, FFN_CHUNK):
            gate = _dot(xb, w_in_ref[:, c:c + FFN_CHUNK])
            up = _dot(xb, w_in_ref[:, hidden + c:hidden + c + FFN_CHUNK])
            act = (gate * jax.nn.sigmoid(gate) * up).astype(_BF16)
            acc = acc + _dot(act, w_out_ref[c:c + FFN_CHUNK, :])
        out_ref[r:r + FFN_SUB_ROWS, :] = _layer_norm(acc, g_ref[...], b_ref[...])


def _ffn_layer(h, w_in, w_out, layer, g, b, alpha):
    m, d = h.shape
    hidden = w_out.shape[1]
    row_spec = pl.BlockSpec((FFN_ROW_TILE, d), lambda i: (i, 0))
    return pl.pallas_call(
        functools.partial(_ffn_kernel, hidden=hidden, alpha=alpha),
        grid=(m // FFN_ROW_TILE,),
        in_specs=[row_spec, _resident_layer(w_in.shape, layer),
                  _resident_layer(w_out.shape, layer),
                  _resident((1, d)), _resident((1, d))],
        out_specs=row_spec,
        out_shape=jax.ShapeDtypeStruct((m, d), _F32),
        compiler_params=_compiler_params(("arbitrary",)),
        name="swiglu_ffn_ln",
    )(h, w_in, w_out, g, b)


def _gelu_tanh(x):
    inner = x * (0.7978845608028654 + (0.7978845608028654 * 0.044715) * (x * x))
    half_x = 0.5 * x
    return half_x * jnp.tanh(inner) + half_x


def _softplus(z):
    return jnp.maximum(z, 0.0) + jnp.log1p(jnp.exp(-jnp.abs(z)))


def _lru_kernel(x_ref, x_next_ref, w_in_ref, conv_w_ref, conv_b_ref, w_gate_ref, b_a_ref,
                b_x_ref, lam_ref, w_out_ref, g_ref, b_ref, out_ref,
                xbuf, gate_pre, tail, a_s, b_s, h_s, ac_s, carry, *, alpha, steps_per_seq):
    sub = V7X_SUBLANES
    seq_tile, width = a_s.shape
    n_tiles = x_ref.shape[0] // seq_tile
    steps = seq_tile // sub
    halo = (CONV_WIDTH - 1) * sub
    group = width // LRU_BLOCKS
    step = pl.program_id(0)
    cur, nxt = step % 2, (step + 1) % 2

    def in_projection(src_ref, slot, t):
        xb = src_ref[t * seq_tile:(t + 1) * seq_tile, :].astype(_BF16)
        xbuf[slot, t, halo:halo + seq_tile, :] = _dot(xb, w_in_ref[:, 0:width])
        gate_pre[slot, t] = _dot(xb, w_in_ref[:, width:2 * width])

    @pl.when(step == 0)
    def _():
        for t in range(n_tiles):
            in_projection(x_ref, 0, t)

    @pl.when(step % steps_per_seq == 0)
    def _():
        tail[...] = jnp.zeros_like(tail)
        carry[...] = jnp.zeros_like(carry)

    half_rate = (-0.5 * LRU_C * LOG2_E) * _softplus(-lam_ref[...])

    def conv(t):
        first_sublane = lax.broadcasted_iota(jnp.int32, (sub, width), 0) == 0
        for g in range(CONV_WIDTH - 1):
            rows = slice(g * sub, (g + 1) * sub)
            last = xbuf[cur, t, seq_tile + g * sub:seq_tile + (g + 1) * sub, :]
            xbuf[cur, t, rows, :] = jnp.where(first_sublane,
                                              pltpu.roll(tail[rows, :], 1, axis=0),
                                              pltpu.roll(last, 1, axis=0))
            tail[rows, :] = last
        xc = conv_b_ref[...]
        for tap in range(CONV_WIDTH):
            start = halo - (CONV_WIDTH - 1 - tap) * sub
            xc = xc + xbuf[cur, t, start:start + seq_tile, :] * conv_w_ref[tap:tap + 1, :]
        return xc

    def gate_projection(xc):
        xcb = xc.astype(_BF16)
        pre_a, pre_x = [], []
        for gidx in range(LRU_BLOCKS):
            both = _dot(xcb[:, gidx * group:(gidx + 1) * group], w_gate_ref[gidx])
            pre_a.append(both[:, 0:group])
            pre_x.append(both[:, group:2 * group])
        return jnp.concatenate(pre_a, axis=-1), jnp.concatenate(pre_x, axis=-1)

    def recurrence(xc, pre_a, pre_x):
        tanh_a = jnp.tanh(0.5 * (pre_a + b_a_ref[...]))
        tanh_x = jnp.tanh(0.5 * (pre_x + b_x_ref[...]))
        a = jnp.exp2(tanh_a * half_rate + half_rate)
        half_xc = 0.5 * xc
        gated_in = tanh_x * half_xc + half_xc
        y = 1.0 - a * a
        a_s[...] = a
        b_s[...] = jnp.where(y > 0.0, y * lax.rsqrt(y), 0.0) * gated_in

        h_loc = jnp.zeros((sub, width), _F32)
        a_cum = jnp.ones((sub, width), _F32)
        for j in range(steps):
            rows = slice(j * sub, (j + 1) * sub)
            a_j = a_s[rows, :]
            h_loc = a_j * h_loc + b_s[rows, :]
            a_cum = a_j * a_cum
            h_s[rows, :] = h_loc
            ac_s[rows, :] = a_cum

        state = carry[...]
        starts = []
        for c in range(sub):
            starts.append(state)
            state = h_loc[c:c + 1, :] + a_cum[c:c + 1, :] * state
        carry[...] = state
        start = jnp.tile(jnp.concatenate(starts, axis=0), (steps, 1))
        return h_s[...] + ac_s[...] * start

    def out_projection(t, h):
        rows = slice(t * seq_tile, (t + 1) * seq_tile)
        gated = (h * _gelu_tanh(gate_pre[cur, t])).astype(_BF16)
        y = _dot(gated, w_out_ref[...]) + alpha * x_ref[rows, :]
        out_ref[rows, :] = _layer_norm(y, g_ref[...], b_ref[...])

    convolved = [conv(t) for t in range(n_tiles)]
    gates = [gate_projection(convolved[0])]
    for t in range(n_tiles):
        in_projection(x_next_ref, nxt, t)
        if t + 1 < n_tiles:
            gates.append(gate_projection(convolved[t + 1]))
        out_projection(t, recurrence(convolved[t], *gates[t]))


def _lru_layer(h, batch, seq, layer, w_in, conv_w, conv_b, w_gate, b_a, b_x, lam, w_out, g, b,
               alpha):
    m, d = h.shape
    width = w_out.shape[1]
    rows = LRU_TILES * STREAM_TILE
    steps_per_seq = seq // rows
    n_steps = m // rows
    halo = (CONV_WIDTH - 1) * V7X_SUBLANES
    tile_f32 = pltpu.VMEM((STREAM_TILE, width), _F32)
    row_spec = pl.BlockSpec((rows, d), lambda i: (i, 0))
    next_spec = pl.BlockSpec((rows, d), lambda i: (jnp.minimum(i + 1, n_steps - 1), 0))
    return pl.pallas_call(
        functools.partial(_lru_kernel, alpha=alpha, steps_per_seq=steps_per_seq),
        grid=(n_steps,),
        in_specs=[row_spec, next_spec, _resident_layer(w_in.shape, layer),
                  _resident(conv_w.shape),
                  _resident(conv_b.shape), _resident_layer(w_gate.shape, layer),
                  _resident(b_a.shape), _resident(b_x.shape), _resident(lam.shape),
                  _resident_layer(w_out.shape, layer), _resident(g.shape), _resident(b.shape)],
        out_specs=row_spec,
        out_shape=jax.ShapeDtypeStruct((m, d), _F32),
        scratch_shapes=[pltpu.VMEM((2, LRU_TILES, halo + STREAM_TILE, width), _F32),
                        pltpu.VMEM((2, LRU_TILES, STREAM_TILE, width), _F32),
                        pltpu.VMEM((halo, width), _F32),
                        tile_f32,
                        tile_f32,
                        tile_f32,
                        tile_f32,
                        pltpu.VMEM((1, width), _F32)],
        compiler_params=_compiler_params(("arbitrary",)),
        name="rglru_block_ln",
    )(h, h, w_in, conv_w, conv_b, w_gate, b_a, b_x, lam, w_out, g, b)


def kernel(x, attn_w_qkv, attn_w_o, lru_w_in, lru_conv_w, lru_conv_b, lru_w_a, lru_b_a,
           lru_w_x, lru_b_x, lru_lambda, lru_w_out, ffn_w_in, ffn_w_out, ln_g, ln_b):
    batch, seq, d = x.shape
    depth = ffn_w_in.shape[0]
    rows = batch * seq
    assert d % N_HEADS == 0 and N_HEADS % MOBA_HEADS_PER_STEP == 0
    assert seq % ROW_TILE == 0 and ROW_TILE % MOBA_BLOCK == 0
    assert seq % (LRU_TILES * STREAM_TILE) == 0
    assert rows % FFN_ROW_TILE == 0 and FFN_ROW_TILE % FFN_SUB_ROWS == 0
    assert rows % OUT_PROJ_ROW_TILE == 0 and ffn_w_out.shape[1] % FFN_CHUNK == 0
    alpha = (2 * depth) ** 0.25
    head_dim = d // N_HEADS
    n_blk = seq // MOBA_BLOCK
    slopes = jnp.exp2(-8.0 * (jnp.arange(N_HEADS, dtype=_F32) + 1.0) / N_HEADS)
    row = lambda v: v.reshape(1, -1)

    w_qkv, w_o = attn_w_qkv.astype(_BF16), attn_w_o.astype(_BF16)
    w_lru_in, w_lru_out = lru_w_in.astype(_BF16), lru_w_out.astype(_BF16)
    w_gate = jnp.concatenate([lru_w_a, lru_w_x], axis=-1).astype(_BF16)
    w_ffn_in, w_ffn_out = ffn_w_in.astype(_BF16), ffn_w_out.astype(_BF16)

    h = _to_stream_order(x)
    for layer in range(depth):
        j = layer // 2
        g0, b0 = row(ln_g[layer, 0]), row(ln_b[layer, 0])
        g1, b1 = row(ln_g[layer, 1]), row(ln_b[layer, 1])
        if layer % 2 == 0:
            qk, vt, kmean = _qkv_projection(h, w_qkv, j, head_dim ** -0.5 * LOG2_E, batch, seq)
            o = _moba_attention(qk, vt, kmean.reshape(batch, n_blk, d), slopes, batch, seq)
            h = _proj_residual_ln(o, w_o, j, h, g0, b0, alpha)
        else:
            h = _lru_layer(h, batch, seq, j, w_lru_in, lru_conv_w[j], row(lru_conv_b[j]),
                           w_gate, row(lru_b_a[j]), row(lru_b_x[j]), row(lru_lambda[j]),
                           w_lru_out, g0, b0, alpha)
        h = _ffn_layer(h, w_ffn_in, w_ffn_out, layer, g1, b1, alpha)
    return _from_stream_order(h, batch, seq)
```

```python
import functools

import jax
import jax.numpy as jnp
from jax import lax
from jax.experimental import pallas as pl
from jax.experimental.pallas import tpu as pltpu

N_HEADS = 8
MOBA_BLOCK = 256
MOBA_TOPK = 3
NEG_INF = -1e30
LRU_BLOCKS = 8
CONV_WIDTH = 4
LRU_C = 8.0
LN_EPS = 1e-5
LOG2_E = 1.4426950408889634

V7X_SUBLANES = 8
V7X_LANES = 128
V7X_VMEM_LIMIT_BYTES = 56 * 1024 * 1024

ROW_TILE = 1024
OUT_PROJ_ROW_TILE = 1024
STREAM_TILE = MOBA_BLOCK
FFN_CHUNK = 256
FFN_ROW_TILE = 1024
FFN_SUB_ROWS = 512
MOBA_HEADS_PER_STEP = 4
_F32 = jnp.float32
_BF16 = jnp.bfloat16
_NT = (((1,), (1,)), ((), ()))


def _compiler_params(semantics):
    return pltpu.CompilerParams(dimension_semantics=semantics,
                                vmem_limit_bytes=V7X_VMEM_LIMIT_BYTES)


def _resident(shape):
    zeros = (0,) * len(shape)
    return pl.BlockSpec(shape, lambda *_: zeros, pipeline_mode=pl.Buffered(1))


def _resident_layer(stacked_shape, layer):
    index = (layer,) + (0,) * (len(stacked_shape) - 1)
    return pl.BlockSpec((None,) + tuple(stacked_shape[1:]), lambda *_: index,
                        pipeline_mode=pl.Buffered(1))


def _to_stream_order(x):
    batch, seq, d = x.shape
    steps = STREAM_TILE // V7X_SUBLANES
    x = x.reshape(batch, seq // STREAM_TILE, V7X_SUBLANES, steps, d)
    return x.transpose(0, 1, 3, 2, 4).reshape(batch * seq, d)


def _from_stream_order(h, batch, seq):
    d = h.shape[-1]
    steps = STREAM_TILE // V7X_SUBLANES
    h = h.reshape(batch, seq // STREAM_TILE, steps, V7X_SUBLANES, d)
    return h.transpose(0, 1, 3, 2, 4).reshape(batch, seq, d)


def _tile_time(idx):
    steps = STREAM_TILE // V7X_SUBLANES
    shift = V7X_SUBLANES.bit_length() - 1
    return (idx & (V7X_SUBLANES - 1)) * steps + (idx >> shift)


def _layer_norm(y, g, b):
    mu = jnp.mean(y, axis=-1, keepdims=True)
    yc = y - mu
    var = jnp.mean(yc * yc, axis=-1, keepdims=True)
    return yc * lax.rsqrt(var + LN_EPS) * g + b


def _dot(a, b):
    return jnp.dot(a, b, preferred_element_type=_F32)


def _qkv_kernel(x_ref, w_ref, qk_ref, vt_ref, kmean_ref, *, d_model, scale):
    xb = x_ref[...].astype(_BF16)
    rows = xb.shape[0]
    chunk = 512
    for c in range(0, 3 * d_model, chunk):
        acc = _dot(xb, w_ref[:, c:c + chunk])
        if c >= 2 * d_model:
            vt_ref[c - 2 * d_model:c - 2 * d_model + chunk, :] = acc.T.astype(_BF16)
            continue
        if c < d_model:
            acc = acc * scale
        else:
            for r in range(rows // MOBA_BLOCK):
                blk = acc[r * MOBA_BLOCK:(r + 1) * MOBA_BLOCK, :]
                kmean_ref[0, r:r + 1, c - d_model:c - d_model + chunk] = (
                    jnp.sum(blk, axis=0, keepdims=True) * (1.0 / MOBA_BLOCK))
        qk_ref[:, c:c + chunk] = acc.astype(_BF16)


def _qkv_projection(h, w_qkv, layer, scale, batch, seq):
    m, d = h.shape
    blocks_per_tile = ROW_TILE // MOBA_BLOCK
    tiles_per_seq = seq // ROW_TILE
    return pl.pallas_call(
        functools.partial(_qkv_kernel, d_model=d, scale=scale),
        grid=(m // ROW_TILE,),
        in_specs=[pl.BlockSpec((ROW_TILE, d), lambda i: (i, 0)),
                  _resident_layer(w_qkv.shape, layer)],
        out_specs=[pl.BlockSpec((ROW_TILE, 2 * d), lambda i: (i, 0)),
                   pl.BlockSpec((None, d, ROW_TILE),
                                lambda i: (i // tiles_per_seq, 0, i % tiles_per_seq)),
                   pl.BlockSpec((1, blocks_per_tile, d), lambda i: (i, 0, 0))],
        out_shape=[jax.ShapeDtypeStruct((m, 2 * d), _BF16),
                   jax.ShapeDtypeStruct((batch, d, seq), _BF16),
                   jax.ShapeDtypeStruct((m // ROW_TILE, blocks_per_tile, d), _F32)],
        compiler_params=_compiler_params(("arbitrary",)),
        name="qkv_projection",
    )(h, w_qkv)


def _split_bf16(x, parts):
    terms = []
    for _ in range(parts):
        t = x.astype(_BF16)
        terms.append(t)
        x = x - t.astype(_F32)
    return terms


def _moba_kernel(slopes_ref, q_ref, k_ref, vt_ref, kmean_ref, o_ref,
                 k_aug, vt_aug, causal, s_scr, p_scr, *, n_blk):
    blk = MOBA_BLOCK
    n_heads, seq, dh2 = k_aug.shape
    dh = dh2 // 2
    bias_lanes = 3
    head_cols = lambda hd: slice(hd * dh, (hd + 1) * dh)

    @pl.when((pl.program_id(0) == 0) & (pl.program_id(1) == 0))
    def _():
        key_time = _tile_time(lax.broadcasted_iota(jnp.int32, (seq, dh), 0) & (blk - 1))
        key_lane = lax.broadcasted_iota(jnp.int32, (seq, dh), 1)
        extra = vt_aug.shape[1] - dh
        ones_row = lax.broadcasted_iota(jnp.int32, (extra, seq), 0) == 0
        for hd in range(n_heads):
            k_aug[hd, :, dh:2 * dh] = jnp.where(key_lane < bias_lanes, key_time, 0).astype(_BF16)
            vt_aug[hd, dh:dh + extra, :] = jnp.where(ones_row, 1.0, 0.0).astype(_BF16)
        key_t = _tile_time(lax.broadcasted_iota(jnp.int32, (blk, blk), 0))
        query_t = _tile_time(lax.broadcasted_iota(jnp.int32, (blk, blk), 1))
        causal[...] = jnp.where(key_t > query_t, NEG_INF, 0.0)

    lane = lax.broadcasted_iota(jnp.int32, (blk, dh), 1)
    block_id = lax.broadcasted_iota(jnp.int32, (n_blk, blk), 0)
    slope2, q_bias, km_split = [], [], []
    for hd in range(n_heads):
        k_aug[hd, :, 0:dh] = k_ref[:, head_cols(hd)]
        vt_aug[hd, 0:dh, :] = vt_ref[head_cols(hd), :]
        slope2.append(slopes_ref[pl.program_id(1) * n_heads + hd] * LOG2_E)
        bias = jnp.zeros((blk, dh), _F32)
        for i, term in enumerate(_split_bf16(jnp.full((blk, dh), slope2[hd], _F32), bias_lanes)):
            bias = jnp.where(lane == i, term.astype(_F32), bias)
        q_bias.append(bias.astype(_BF16))
        km_split.append(_split_bf16(kmean_ref[:, head_cols(hd)], 2))

    def score_stage(hd, qi):
        width = (qi + 1) * blk
        q = q_ref[qi * blk:(qi + 1) * blk, head_cols(hd)]
        s_scr[hd, qi % 2, 0:width, :] = lax.dot_general(
            k_aug[hd, 0:width, :], jnp.concatenate([q, q_bias[hd]], axis=-1), _NT,
            preferred_element_type=_F32)
        if qi <= MOBA_TOPK:
            return None
        km_hi, km_lo = km_split[hd]
        gate = (lax.dot_general(km_hi, q, _NT, preferred_element_type=_F32)
                + lax.dot_general(km_lo, q, _NT, preferred_element_type=_F32))
        ahead = jnp.zeros((n_blk, blk), _F32)
        for m in range(qi):
            g_m = gate[m:m + 1, :]
            wins = (g_m > gate) | ((g_m == gate) & (block_id > m))
            ahead = ahead + jnp.where(wins, 1.0, 0.0)
        return (ahead < float(MOBA_TOPK)) & (block_id < qi)

    def softmax_stage(hd, qi, picked):
        s_buf, p_buf = s_scr.at[hd, qi % 2], p_scr.at[hd, qi % 2]

        def block_scores(n):
            t = s_buf[n * blk:(n + 1) * blk, :]
            return t + causal[...] if n == qi else t

        query_term = [slope2[hd] * float(-blk * (qi - n)) for n in range(qi)]
        if picked is not None:
            query_term = [jnp.where(picked[n:n + 1, :], query_term[n], NEG_INF)
                          for n in range(qi)]
        m_run = jnp.max(block_scores(qi), axis=0, keepdims=True)
        for n in range(qi):
            m_run = jnp.maximum(m_run,
                                jnp.max(block_scores(n), axis=0, keepdims=True) + query_term[n])
        for n in range(qi + 1):
            shift = m_run if n == qi else m_run - query_term[n]
            p_buf[n * blk:(n + 1) * blk, :] = jnp.exp2(block_scores(n) - shift).astype(_BF16)

    def output_stage(hd, qi):
        width = (qi + 1) * blk
        out = _dot(vt_aug[hd, :, 0:width], p_scr[hd, qi % 2, 0:width, :])
        o_ref[qi * blk:(qi + 1) * blk, head_cols(hd)] = (
            out[0:dh, :] / out[dh:dh + 1, :]).T.astype(o_ref.dtype)

    heads = range(n_heads)
    picked = [score_stage(hd, 0) for hd in heads]
    for qi in range(n_blk):
        picked_next = [score_stage(hd, qi + 1) if qi + 1 < n_blk else None for hd in heads]
        for hd in heads:
            softmax_stage(hd, qi, picked[hd])
        for hd in heads:
            output_stage(hd, qi)
        picked = picked_next


def _moba_attention(qk, vt, kmean, slopes, batch, seq):
    m, d2 = qk.shape
    d = d2 // 2
    dh = d // N_HEADS
    n_blk = seq // MOBA_BLOCK
    bf16_rows = 2 * V7X_SUBLANES
    hps = MOBA_HEADS_PER_STEP
    groups = N_HEADS // hps
    wide = hps * dh
    return pl.pallas_call(
        functools.partial(_moba_kernel, n_blk=n_blk),
        grid=(batch, groups),
        in_specs=[pl.BlockSpec(memory_space=pltpu.SMEM),
                  pl.BlockSpec((seq, wide), lambda b, g: (b, g)),
                  pl.BlockSpec((seq, wide), lambda b, g: (b, groups + g)),
                  pl.BlockSpec((None, wide, seq), lambda b, g: (b, g, 0)),
                  pl.BlockSpec((None, n_blk, wide), lambda b, g: (b, 0, g))],
        out_specs=pl.BlockSpec((seq, wide), lambda b, g: (b, g)),
        out_shape=jax.ShapeDtypeStruct((m, d), _BF16),
        scratch_shapes=[pltpu.VMEM((hps, seq, 2 * dh), _BF16),
                        pltpu.VMEM((hps, dh + bf16_rows, seq), _BF16),
                        pltpu.VMEM((MOBA_BLOCK, MOBA_BLOCK), _F32),
                        pltpu.VMEM((hps, 2, seq, MOBA_BLOCK), _F32),
                        pltpu.VMEM((hps, 2, seq, MOBA_BLOCK), _BF16)],
        compiler_params=_compiler_params(("arbitrary", "arbitrary")),
        name="moba_attention",
    )(slopes, qk, qk, vt, kmean)


def _proj_ln_kernel(a_ref, w_ref, res_ref, g_ref, b_ref, out_ref, *, alpha):
    y = _dot(a_ref[...], w_ref[...]) + alpha * res_ref[...]
    out_ref[...] = _layer_norm(y, g_ref[...], b_ref[...])


def _proj_residual_ln(a, w, layer, res, g, b, alpha):
    m, d = res.shape
    k = a.shape[1]
    row_spec = lambda width: pl.BlockSpec((OUT_PROJ_ROW_TILE, width), lambda i: (i, 0))
    return pl.pallas_call(
        functools.partial(_proj_ln_kernel, alpha=alpha),
        grid=(m // OUT_PROJ_ROW_TILE,),
        in_specs=[row_spec(k), _resident_layer(w.shape, layer), row_spec(d),
                  _resident((1, d)), _resident((1, d))],
        out_specs=row_spec(d),
        out_shape=jax.ShapeDtypeStruct((m, d), _F32),
        compiler_params=_compiler_params(("arbitrary",)),
        name="out_projection_ln",
    )(a, w, res, g, b)


def _ffn_kernel(x_ref, w_in_ref, w_out_ref, g_ref, b_ref, out_ref, *, hidden, alpha):
    for r in range(0, x_ref.shape[0], FFN_SUB_ROWS):
        x = x_ref[r:r + FFN_SUB_ROWS, :]
        xb = x.astype(_BF16)
        acc = alpha * x
        for c in range(0, hidden, FFN_CHUNK):
            gate = _dot(xb, w_in_ref[:, c:c + FFN_CHUNK])
            up = _dot(xb, w_in_ref[:, hidden + c:hidden + c + FFN_CHUNK])
            act = (gate * jax.nn.sigmoid(gate) * up).astype(_BF16)
            acc = acc + _dot(act, w_out_ref[c:c + FFN_CHUNK, :])
        out_ref[r:r + FFN_SUB_ROWS, :] = _layer_norm(acc, g_ref[...], b_ref[...])


def _ffn_layer(h, w_in, w_out, layer, g, b, alpha):
    m, d = h.shape
    hidden = w_out.shape[1]
    row_spec = pl.BlockSpec((FFN_ROW_TILE, d), lambda i: (i, 0))
    return pl.pallas_call(
        functools.partial(_ffn_kernel, hidden=hidden, alpha=alpha),
        grid=(m // FFN_ROW_TILE,),
        in_specs=[row_spec, _resident_layer(w_in.shape, layer),
                  _resident_layer(w_out.shape, layer),
                  _resident((1, d)), _resident((1, d))],
        out_specs=row_spec,
        out_shape=jax.ShapeDtypeStruct((m, d), _F32),
        compiler_params=_compiler_params(("arbitrary",)),
        name="swiglu_ffn_ln",
    )(h, w_in, w_out, g, b)


def _gelu_tanh(x):
    inner = x * (0.7978845608028654 + (0.7978845608028654 * 0.044715) * (x * x))
    half_x = 0.5 * x
    return half_x * jnp.tanh(inner) + half_x


def _softplus(z):
    return jnp.maximum(z, 0.0) + jnp.log1p(jnp.exp(-jnp.abs(z)))


def _lru_kernel(x_ref, x_next_ref, w_in_ref, conv_w_ref, conv_b_ref, w_gate_ref, b_a_ref,
                b_x_ref, lam_ref, w_out_ref, g_ref, b_ref, out_ref,
                xbuf_even, xbuf_odd, staged_even, staged_odd, tail, a_s, b_s, h_s, ac_s, carry,
                *, alpha, tiles_per_seq):
    sub = V7X_SUBLANES
    seq_tile, width = a_s.shape
    steps = seq_tile // sub
    halo = (CONV_WIDTH - 1) * sub
    group = width // LRU_BLOCKS
    first_tile = 2 * pl.program_id(0)
    xbuf, staged = (xbuf_even, xbuf_odd), (staged_even, staged_odd)
    gate_pre, conv_out, pre_a_plane, pre_x_plane = range(4)
    half_rate = (-0.5 * LRU_C * LOG2_E) * _softplus(-lam_ref[...])

    def starts_sequence(tile):
        return tile % tiles_per_seq == 0

    def in_projection(src_ref, row0, slot):
        xb = src_ref[row0:row0 + seq_tile, :].astype(_BF16)
        xbuf[slot][halo:halo + seq_tile, :] = _dot(xb, w_in_ref[:, 0:width])
        staged[slot][gate_pre] = _dot(xb, w_in_ref[:, width:2 * width])

    def conv_and_gates(slot, tile):
        first_sublane = lax.broadcasted_iota(jnp.int32, (sub, width), 0) == 0
        fresh = starts_sequence(tile)
        for g in range(CONV_WIDTH - 1):
            rows = slice(g * sub, (g + 1) * sub)
            last = xbuf[slot][seq_tile + g * sub:seq_tile + (g + 1) * sub, :]
            before = jnp.where(fresh, 0.0, tail[rows, :])
            xbuf[slot][rows, :] = jnp.where(first_sublane, pltpu.roll(before, 1, axis=0),
                                            pltpu.roll(last, 1, axis=0))
            tail[rows, :] = last
        xc = conv_b_ref[...]
        for tap in range(CONV_WIDTH):
            start = halo - (CONV_WIDTH - 1 - tap) * sub
            xc = xc + xbuf[slot][start:start + seq_tile, :] * conv_w_ref[tap:tap + 1, :]
        staged[slot][conv_out] = xc
        xcb = xc.astype(_BF16)
        for gidx in range(LRU_BLOCKS):
            cols = slice(gidx * group, (gidx + 1) * group)
            both = _dot(xcb[:, cols], w_gate_ref[gidx])
            staged[slot][pre_a_plane, :, cols] = both[:, 0:group]
            staged[slot][pre_x_plane, :, cols] = both[:, group:2 * group]

    def recurrence(slot, tile):
        xc = staged[slot][conv_out]
        tanh_a = jnp.tanh(0.5 * (staged[slot][pre_a_plane] + b_a_ref[...]))
        tanh_x = jnp.tanh(0.5 * (staged[slot][pre_x_plane] + b_x_ref[...]))
        a = jnp.exp2(tanh_a * half_rate + half_rate)
        half_xc = 0.5 * xc
        gated_in = tanh_x * half_xc + half_xc
        y = 1.0 - a * a
        a_s[...] = a
        b_s[...] = jnp.where(y > 0.0, y * lax.rsqrt(y), 0.0) * gated_in

        h_loc = jnp.zeros((sub, width), _F32)
        a_cum = jnp.ones((sub, width), _F32)
        for j in range(steps):
            rows = slice(j * sub, (j + 1) * sub)
            a_j = a_s[rows, :]
            h_loc = a_j * h_loc + b_s[rows, :]
            a_cum = a_j * a_cum
            h_s[rows, :] = h_loc
            ac_s[rows, :] = a_cum

        state = jnp.where(starts_sequence(tile), 0.0, carry[...])
        starts = []
        for c in range(sub):
            starts.append(state)
            state = h_loc[c:c + 1, :] + a_cum[c:c + 1, :] * state
        carry[...] = state
        start = jnp.tile(jnp.concatenate(starts, axis=0), (steps, 1))
        return h_s[...] + ac_s[...] * start

    def back(slot, row0, tile):
        rows = slice(row0, row0 + seq_tile)
        gated = (recurrence(slot, tile) * _gelu_tanh(staged[slot][gate_pre])).astype(_BF16)
        y = _dot(gated, w_out_ref[...]) + alpha * x_ref[rows, :]
        out_ref[rows, :] = _layer_norm(y, g_ref[...], b_ref[...])

    @pl.when(pl.program_id(0) == 0)
    def _():
        in_projection(x_ref, 0, 0)
        conv_and_gates(0, first_tile)

    in_projection(x_ref, seq_tile, 1)
    back(0, 0, first_tile)
    conv_and_gates(1, first_tile + 1)
    in_projection(x_next_ref, 0, 0)
    back(1, seq_tile, first_tile + 1)
    conv_and_gates(0, first_tile + 2)


def _lru_layer(h, batch, seq, layer, w_in, conv_w, conv_b, w_gate, b_a, b_x, lam, w_out, g, b,
               alpha):
    m, d = h.shape
    width = w_out.shape[1]
    rows = 2 * STREAM_TILE
    n_tiles = m // STREAM_TILE
    halo = (CONV_WIDTH - 1) * V7X_SUBLANES
    tile_f32 = pltpu.VMEM((STREAM_TILE, width), _F32)
    row_spec = pl.BlockSpec((rows, d), lambda i: (i, 0))
    next_spec = pl.BlockSpec((STREAM_TILE, d), lambda i: (jnp.minimum(2 * i + 2, n_tiles - 1), 0))
    return pl.pallas_call(
        functools.partial(_lru_kernel, alpha=alpha, tiles_per_seq=seq // STREAM_TILE),
        grid=(m // rows,),
        in_specs=[row_spec, next_spec, _resident_layer(w_in.shape, layer),
                  _resident(conv_w.shape),
                  _resident(conv_b.shape), _resident_layer(w_gate.shape, layer),
                  _resident(b_a.shape), _resident(b_x.shape), _resident(lam.shape),
                  _resident_layer(w_out.shape, layer), _resident(g.shape), _resident(b.shape)],
        out_specs=row_spec,
        out_shape=jax.ShapeDtypeStruct((m, d), _F32),
        scratch_shapes=[pltpu.VMEM((halo + STREAM_TILE, width), _F32),
                        pltpu.VMEM((halo + STREAM_TILE, width), _F32),
                        pltpu.VMEM((4, STREAM_TILE, width), _F32),
                        pltpu.VMEM((4, STREAM_TILE, width), _F32),
                        pltpu.VMEM((halo, width), _F32),
                        tile_f32,
                        tile_f32,
                        tile_f32,
                        tile_f32,
                        pltpu.VMEM((1, width), _F32)],
        compiler_params=_compiler_params(("arbitrary",)),
        name="rglru_block_ln",
    )(h, h, w_in, conv_w, conv_b, w_gate, b_a, b_x, lam, w_out, g, b)


def kernel(x, attn_w_qkv, attn_w_o, lru_w_in, lru_conv_w, lru_conv_b, lru_w_a, lru_b_a,
           lru_w_x, lru_b_x, lru_lambda, lru_w_out, ffn_w_in, ffn_w_out, ln_g, ln_b):
    batch, seq, d = x.shape
    depth = ffn_w_in.shape[0]
    rows = batch * seq
    assert d % N_HEADS == 0 and N_HEADS % MOBA_HEADS_PER_STEP == 0
    assert seq % ROW_TILE == 0 and ROW_TILE % MOBA_BLOCK == 0
    assert seq % (2 * STREAM_TILE) == 0
    assert rows % FFN_ROW_TILE == 0 and FFN_ROW_TILE % FFN_SUB_ROWS == 0
    assert rows % OUT_PROJ_ROW_TILE == 0 and ffn_w_out.shape[1] % FFN_CHUNK == 0
    alpha = (2 * depth) ** 0.25
    head_dim = d // N_HEADS
    n_blk = seq // MOBA_BLOCK
    slopes = jnp.exp2(-8.0 * (jnp.arange(N_HEADS, dtype=_F32) + 1.0) / N_HEADS)
    row = lambda v: v.reshape(1, -1)

    w_qkv, w_o = attn_w_qkv.astype(_BF16), attn_w_o.astype(_BF16)
    w_lru_in, w_lru_out = lru_w_in.astype(_BF16), lru_w_out.astype(_BF16)
    w_gate = jnp.concatenate([lru_w_a, lru_w_x], axis=-1).astype(_BF16)
    w_ffn_in, w_ffn_out = ffn_w_in.astype(_BF16), ffn_w_out.astype(_BF16)

    h = _to_stream_order(x)
    for layer in range(depth):
        j = layer // 2
        g0, b0 = row(ln_g[layer, 0]), row(ln_b[layer, 0])
        g1, b1 = row(ln_g[layer, 1]), row(ln_b[layer, 1])
        if layer % 2 == 0:
            qk, vt, kmean = _qkv_projection(h, w_qkv, j, head_dim ** -0.5 * LOG2_E, batch, seq)
            o = _moba_attention(qk, vt, kmean.reshape(batch, n_blk, d), slopes, batch, seq)
            h = _proj_residual_ln(o, w_o, j, h, g0, b0, alpha)
        else:
            h = _lru_layer(h, batch, seq, j, w_lru_in, lru_conv_w[j], row(lru_conv_b[j]),
                           w_gate, row(lru_b_a[j]), row(lru_b_x[j]), row(lru_lambda[j]),
                           w_lru_out, g0, b0, alpha)
        h = _ffn_layer(h, w_ffn_in, w_ffn_out, layer, g1, b1, alpha)
    return _from_stream_order(h, batch, seq)
```

```python
import functools

import jax
import jax.numpy as jnp
from jax import lax
from jax.experimental import pallas as pl
from jax.experimental.pallas import tpu as pltpu

N_HEADS = 8
MOBA_BLOCK = 256
MOBA_TOPK = 3
NEG_INF = -1e30
LRU_BLOCKS = 8
CONV_WIDTH = 4
LRU_C = 8.0
LN_EPS = 1e-5
LOG2_E = 1.4426950408889634

V7X_SUBLANES = 8
V7X_LANES = 128
V7X_VMEM_LIMIT_BYTES = 56 * 1024 * 1024

ROW_TILE = 1024
OUT_PROJ_ROW_TILE = 1024
STREAM_TILE = MOBA_BLOCK
FFN_CHUNK = 256
FFN_ROW_TILE = 1024
FFN_SUB_ROWS = 512
LRU_TILES = 2
MOBA_HEADS_PER_STEP = 4
_F32 = jnp.float32
_BF16 = jnp.bfloat16
_NT = (((1,), (1,)), ((), ()))


def _compiler_params(semantics):
    return pltpu.CompilerParams(dimension_semantics=semantics,
                                vmem_limit_bytes=V7X_VMEM_LIMIT_BYTES)


def _resident(shape):
    zeros = (0,) * len(shape)
    return pl.BlockSpec(shape, lambda *_: zeros, pipeline_mode=pl.Buffered(1))


def _resident_layer(stacked_shape, layer):
    index = (layer,) + (0,) * (len(stacked_shape) - 1)
    return pl.BlockSpec((None,) + tuple(stacked_shape[1:]), lambda *_: index,
                        pipeline_mode=pl.Buffered(1))


def _stream_rows(natural_group):
    steps = STREAM_TILE // V7X_SUBLANES
    chunk, first_step = divmod(natural_group * V7X_SUBLANES, steps)
    return pl.ds(first_step * V7X_SUBLANES + chunk, V7X_SUBLANES, stride=V7X_SUBLANES)


def _natural_to_stream(src_ref, slab):
    tiles, lane_groups = slab.shape[0], slab.shape[1]
    for t in range(tiles):
        for g in range(lane_groups):
            lanes = slice(g * V7X_LANES, (g + 1) * V7X_LANES)
            for group in range(STREAM_TILE // V7X_SUBLANES):
                row0 = t * STREAM_TILE + group * V7X_SUBLANES
                slab[t, g, _stream_rows(group), :] = src_ref[row0:row0 + V7X_SUBLANES, lanes]
    return jnp.concatenate(
        [jnp.concatenate([slab[t, g] for g in range(lane_groups)], axis=-1)
         for t in range(tiles)], axis=0)


def _stream_to_natural(y, slab, dst_ref, dst_row0):
    lane_groups = slab.shape[1]
    for t in range(y.shape[0] // STREAM_TILE):
        for g in range(lane_groups):
            lanes = slice(g * V7X_LANES, (g + 1) * V7X_LANES)
            slab[t, g] = y[t * STREAM_TILE:(t + 1) * STREAM_TILE, lanes]
            for group in range(STREAM_TILE // V7X_SUBLANES):
                row0 = dst_row0 + t * STREAM_TILE + group * V7X_SUBLANES
                dst_ref[row0:row0 + V7X_SUBLANES, lanes] = slab[t, g, _stream_rows(group), :]


def _reorder_slab(rows, d):
    return pltpu.VMEM((rows // STREAM_TILE, d // V7X_LANES, STREAM_TILE, V7X_LANES), _F32)


def _tile_time(idx):
    steps = STREAM_TILE // V7X_SUBLANES
    shift = V7X_SUBLANES.bit_length() - 1
    return (idx & (V7X_SUBLANES - 1)) * steps + (idx >> shift)


def _layer_norm(y, g, b):
    mu = jnp.mean(y, axis=-1, keepdims=True)
    yc = y - mu
    var = jnp.mean(yc * yc, axis=-1, keepdims=True)
    return yc * lax.rsqrt(var + LN_EPS) * g + b


def _dot(a, b):
    return jnp.dot(a, b, preferred_element_type=_F32)


def _qkv_kernel(x_ref, w_ref, qk_ref, vt_ref, kmean_ref, *stream_out, d_model, scale):
    if stream_out:
        stream_ref, slab = stream_out
        x = _natural_to_stream(x_ref, slab)
        stream_ref[...] = x
    else:
        x = x_ref[...]
    xb = x.astype(_BF16)
    rows = xb.shape[0]
    chunk = 512
    for c in range(0, 3 * d_model, chunk):
        acc = _dot(xb, w_ref[:, c:c + chunk])
        if c >= 2 * d_model:
            vt_ref[c - 2 * d_model:c - 2 * d_model + chunk, :] = acc.T.astype(_BF16)
            continue
        if c < d_model:
            acc = acc * scale
        else:
            for r in range(rows // MOBA_BLOCK):
                blk = acc[r * MOBA_BLOCK:(r + 1) * MOBA_BLOCK, :]
                kmean_ref[0, r:r + 1, c - d_model:c - d_model + chunk] = (
                    jnp.sum(blk, axis=0, keepdims=True) * (1.0 / MOBA_BLOCK))
        qk_ref[:, c:c + chunk] = acc.astype(_BF16)


def _qkv_projection(h, w_qkv, layer, scale, batch, seq, time_ordered_input=False):
    m, d = h.shape
    blocks_per_tile = ROW_TILE // MOBA_BLOCK
    tiles_per_seq = seq // ROW_TILE
    row_spec = pl.BlockSpec((ROW_TILE, d), lambda i: (i, 0))
    out_specs = [pl.BlockSpec((ROW_TILE, 2 * d), lambda i: (i, 0)),
                 pl.BlockSpec((None, d, ROW_TILE),
                              lambda i: (i // tiles_per_seq, 0, i % tiles_per_seq)),
                 pl.BlockSpec((1, blocks_per_tile, d), lambda i: (i, 0, 0))]
    out_shape = [jax.ShapeDtypeStruct((m, 2 * d), _BF16),
                 jax.ShapeDtypeStruct((batch, d, seq), _BF16),
                 jax.ShapeDtypeStruct((m // ROW_TILE, blocks_per_tile, d), _F32)]
    scratch = []
    if time_ordered_input:
        out_specs.append(row_spec)
        out_shape.append(jax.ShapeDtypeStruct((m, d), _F32))
        scratch.append(_reorder_slab(ROW_TILE, d))
    return pl.pallas_call(
        functools.partial(_qkv_kernel, d_model=d, scale=scale),
        grid=(m // ROW_TILE,),
        in_specs=[row_spec, _resident_layer(w_qkv.shape, layer)],
        out_specs=out_specs,
        out_shape=out_shape,
        scratch_shapes=scratch,
        compiler_params=_compiler_params(("arbitrary",)),
        name="qkv_projection",
    )(h, w_qkv)


def _split_bf16(x, parts):
    terms = []
    for _ in range(parts):
        t = x.astype(_BF16)
        terms.append(t)
        x = x - t.astype(_F32)
    return terms


def _moba_kernel(slopes_ref, q_ref, k_ref, vt_ref, kmean_ref, o_ref,
                 k_aug, vt_aug, causal, s_scr, p_scr, *, n_blk):
    blk = MOBA_BLOCK
    n_heads, seq, dh2 = k_aug.shape
    dh = dh2 // 2
    bias_lanes = 3
    head_cols = lambda hd: slice(hd * dh, (hd + 1) * dh)

    @pl.when((pl.program_id(0) == 0) & (pl.program_id(1) == 0))
    def _():
        key_time = _tile_time(lax.broadcasted_iota(jnp.int32, (seq, dh), 0) & (blk - 1))
        key_lane = lax.broadcasted_iota(jnp.int32, (seq, dh), 1)
        extra = vt_aug.shape[1] - dh
        ones_row = lax.broadcasted_iota(jnp.int32, (extra, seq), 0) == 0
        for hd in range(n_heads):
            k_aug[hd, :, dh:2 * dh] = jnp.where(key_lane < bias_lanes, key_time, 0).astype(_BF16)
            vt_aug[hd, dh:dh + extra, :] = jnp.where(ones_row, 1.0, 0.0).astype(_BF16)
        key_t = _tile_time(lax.broadcasted_iota(jnp.int32, (blk, blk), 0))
        query_t = _tile_time(lax.broadcasted_iota(jnp.int32, (blk, blk), 1))
        causal[...] = jnp.where(key_t > query_t, NEG_INF, 0.0)

    lane = lax.broadcasted_iota(jnp.int32, (blk, dh), 1)
    block_id = lax.broadcasted_iota(jnp.int32, (n_blk, blk), 0)
    slope2, q_bias, km_split = [], [], []
    for hd in range(n_heads):
        k_aug[hd, :, 0:dh] = k_ref[:, head_cols(hd)]
        vt_aug[hd, 0:dh, :] = vt_ref[head_cols(hd), :]
        slope2.append(slopes_ref[pl.program_id(1) * n_heads + hd] * LOG2_E)
        bias = jnp.zeros((blk, dh), _F32)
        for i, term in enumerate(_split_bf16(jnp.full((blk, dh), slope2[hd], _F32), bias_lanes)):
            bias = jnp.where(lane == i, term.astype(_F32), bias)
        q_bias.append(bias.astype(_BF16))
        km_split.append(_split_bf16(kmean_ref[:, head_cols(hd)], 2))

    def score_stage(hd, qi):
        width = (qi + 1) * blk
        q = q_ref[qi * blk:(qi + 1) * blk, head_cols(hd)]
        s_scr[hd, qi % 2, 0:width, :] = lax.dot_general(
            k_aug[hd, 0:width, :], jnp.concatenate([q, q_bias[hd]], axis=-1), _NT,
            preferred_element_type=_F32)
        if qi <= MOBA_TOPK:
            return None
        km_hi, km_lo = km_split[hd]
        gate = (lax.dot_general(km_hi, q, _NT, preferred_element_type=_F32)
                + lax.dot_general(km_lo, q, _NT, preferred_element_type=_F32))
        ahead = jnp.zeros((n_blk, blk), _F32)
        for m in range(qi):
            g_m = gate[m:m + 1, :]
            wins = (g_m > gate) | ((g_m == gate) & (block_id > m))
            ahead = ahead + jnp.where(wins, 1.0, 0.0)
        return (ahead < float(MOBA_TOPK)) & (block_id < qi)

    def softmax_stage(hd, qi, picked):
        s_buf, p_buf = s_scr.at[hd, qi % 2], p_scr.at[hd, qi % 2]

        def block_scores(n):
            t = s_buf[n * blk:(n + 1) * blk, :]
            return t + causal[...] if n == qi else t

        query_term = [slope2[hd] * float(-blk * (qi - n)) for n in range(qi)]
        if picked is not None:
            query_term = [jnp.where(picked[n:n + 1, :], query_term[n], NEG_INF)
                          for n in range(qi)]
        m_run = jnp.max(block_scores(qi), axis=0, keepdims=True)
        for n in range(qi):
            m_run = jnp.maximum(m_run,
                                jnp.max(block_scores(n), axis=0, keepdims=True) + query_term[n])
        for n in range(qi + 1):
            shift = m_run if n == qi else m_run - query_term[n]
            p_buf[n * blk:(n + 1) * blk, :] = jnp.exp2(block_scores(n) - shift).astype(_BF16)

    def output_stage(hd, qi):
        width = (qi + 1) * blk
        out = _dot(vt_aug[hd, :, 0:width], p_scr[hd, qi % 2, 0:width, :])
        o_ref[qi * blk:(qi + 1) * blk, head_cols(hd)] = (
            out[0:dh, :] / out[dh:dh + 1, :]).T.astype(o_ref.dtype)

    heads = range(n_heads)
    picked = [score_stage(hd, 0) for hd in heads]
    for qi in range(n_blk):
        picked_next = [score_stage(hd, qi + 1) if qi + 1 < n_blk else None for hd in heads]
        for hd in heads:
            softmax_stage(hd, qi, picked[hd])
        for hd in heads:
            output_stage(hd, qi)
        picked = picked_next


def _moba_attention(qk, vt, kmean, slopes, batch, seq):
    m, d2 = qk.shape
    d = d2 // 2
    dh = d // N_HEADS
    n_blk = seq // MOBA_BLOCK
    bf16_rows = 2 * V7X_SUBLANES
    hps = MOBA_HEADS_PER_STEP
    groups = N_HEADS // hps
    wide = hps * dh
    return pl.pallas_call(
        functools.partial(_moba_kernel, n_blk=n_blk),
        grid=(batch, groups),
        in_specs=[pl.BlockSpec(memory_space=pltpu.SMEM),
                  pl.BlockSpec((seq, wide), lambda b, g: (b, g)),
                  pl.BlockSpec((seq, wide), lambda b, g: (b, groups + g)),
                  pl.BlockSpec((None, wide, seq), lambda b, g: (b, g, 0)),
                  pl.BlockSpec((None, n_blk, wide), lambda b, g: (b, 0, g))],
        out_specs=pl.BlockSpec((seq, wide), lambda b, g: (b, g)),
        out_shape=jax.ShapeDtypeStruct((m, d), _BF16),
        scratch_shapes=[pltpu.VMEM((hps, seq, 2 * dh), _BF16),
                        pltpu.VMEM((hps, dh + bf16_rows, seq), _BF16),
                        pltpu.VMEM((MOBA_BLOCK, MOBA_BLOCK), _F32),
                        pltpu.VMEM((hps, 2, seq, MOBA_BLOCK), _F32),
                        pltpu.VMEM((hps, 2, seq, MOBA_BLOCK), _BF16)],
        compiler_params=_compiler_params(("arbitrary", "arbitrary")),
        name="moba_attention",
    )(slopes, qk, qk, vt, kmean)


def _proj_ln_kernel(a_ref, w_ref, res_ref, g_ref, b_ref, out_ref, *, alpha):
    y = _dot(a_ref[...], w_ref[...]) + alpha * res_ref[...]
    out_ref[...] = _layer_norm(y, g_ref[...], b_ref[...])


def _proj_residual_ln(a, w, layer, res, g, b, alpha):
    m, d = res.shape
    k = a.shape[1]
    row_spec = lambda width: pl.BlockSpec((OUT_PROJ_ROW_TILE, width), lambda i: (i, 0))
    return pl.pallas_call(
        functools.partial(_proj_ln_kernel, alpha=alpha),
        grid=(m // OUT_PROJ_ROW_TILE,),
        in_specs=[row_spec(k), _resident_layer(w.shape, layer), row_spec(d),
                  _resident((1, d)), _resident((1, d))],
        out_specs=row_spec(d),
        out_shape=jax.ShapeDtypeStruct((m, d), _F32),
        compiler_params=_compiler_params(("arbitrary",)),
        name="out_projection_ln",
    )(a, w, res, g, b)


def _ffn_kernel(x_ref, w_in_ref, w_out_ref, g_ref, b_ref, out_ref, *maybe_slab, hidden, alpha):
    for r in range(0, x_ref.shape[0], FFN_SUB_ROWS):
        x = x_ref[r:r + FFN_SUB_ROWS, :]
        xb = x.astype(_BF16)
        acc = alpha * x
        for c in range(0, hidden, FFN_CHUNK):
            gate = _dot(xb, w_in_ref[:, c:c + FFN_CHUNK])
            up = _dot(xb, w_in_ref[:, hidden + c:hidden + c + FFN_CHUNK])
            act = (gate * jax.nn.sigmoid(gate) * up).astype(_BF16)
            acc = acc + _dot(act, w_out_ref[c:c + FFN_CHUNK, :])
        y = _layer_norm(acc, g_ref[...], b_ref[...])
        if maybe_slab:
            _stream_to_natural(y, maybe_slab[0], out_ref, r)
        else:
            out_ref[r:r + FFN_SUB_ROWS, :] = y


def _ffn_layer(h, w_in, w_out, layer, g, b, alpha, time_ordered_output=False):
    m, d = h.shape
    hidden = w_out.shape[1]
    row_spec = pl.BlockSpec((FFN_ROW_TILE, d), lambda i: (i, 0))
    return pl.pallas_call(
        functools.partial(_ffn_kernel, hidden=hidden, alpha=alpha),
        grid=(m // FFN_ROW_TILE,),
        in_specs=[row_spec, _resident_layer(w_in.shape, layer),
                  _resident_layer(w_out.shape, layer),
                  _resident((1, d)), _resident((1, d))],
        out_specs=row_spec,
        out_shape=jax.ShapeDtypeStruct((m, d), _F32),
        scratch_shapes=[_reorder_slab(FFN_SUB_ROWS, d)] if time_ordered_output else [],
        compiler_params=_compiler_params(("arbitrary",)),
        name="swiglu_ffn_ln",
    )(h, w_in, w_out, g, b)


def _gelu_tanh(x):
    inner = x * (0.7978845608028654 + (0.7978845608028654 * 0.044715) * (x * x))
    half_x = 0.5 * x
    return half_x * jnp.tanh(inner) + half_x


def _softplus(z):
    return jnp.maximum(z, 0.0) + jnp.log1p(jnp.exp(-jnp.abs(z)))


def _lru_kernel(x_ref, x_next_ref, w_in_ref, conv_w_ref, conv_b_ref, w_gate_ref, b_a_ref,
                b_x_ref, lam_ref, w_out_ref, g_ref, b_ref, out_ref,
                xbuf, gate_pre, tail, a_s, b_s, h_s, ac_s, carry, *, alpha, steps_per_seq):
    sub = V7X_SUBLANES
    seq_tile, width = a_s.shape
    n_tiles = x_ref.shape[0] // seq_tile
    steps = seq_tile // sub
    halo = (CONV_WIDTH - 1) * sub
    group = width // LRU_BLOCKS
    step = pl.program_id(0)
    cur, nxt = step % 2, (step + 1) % 2

    def in_projection(src_ref, slot, t):
        xb = src_ref[t * seq_tile:(t + 1) * seq_tile, :].astype(_BF16)
        xbuf[slot, t, halo:halo + seq_tile, :] = _dot(xb, w_in_ref[:, 0:width])
        gate_pre[slot, t] = _dot(xb, w_in_ref[:, width:2 * width])

    @pl.when(step == 0)
    def _():
        for t in range(n_tiles):
            in_projection(x_ref, 0, t)

    @pl.when(step % steps_per_seq == 0)
    def _():
        tail[...] = jnp.zeros_like(tail)
        carry[...] = jnp.zeros_like(carry)

    half_rate = (-0.5 * LRU_C * LOG2_E) * _softplus(-lam_ref[...])

    def conv(t):
        first_sublane = lax.broadcasted_iota(jnp.int32, (sub, width), 0) == 0
        for g in range(CONV_WIDTH - 1):
            rows = slice(g * sub, (g + 1) * sub)
            last = xbuf[cur, t, seq_tile + g * sub:seq_tile + (g + 1) * sub, :]
            xbuf[cur, t, rows, :] = jnp.where(first_sublane,
                                              pltpu.roll(tail[rows, :], 1, axis=0),
                                              pltpu.roll(last, 1, axis=0))
            tail[rows, :] = last
        xc = conv_b_ref[...]
        for tap in range(CONV_WIDTH):
            start = halo - (CONV_WIDTH - 1 - tap) * sub
            xc = xc + xbuf[cur, t, start:start + seq_tile, :] * conv_w_ref[tap:tap + 1, :]
        return xc

    def gate_projection(xc):
        xcb = xc.astype(_BF16)
        pre_a, pre_x = [], []
        for gidx in range(LRU_BLOCKS):
            both = _dot(xcb[:, gidx * group:(gidx + 1) * group], w_gate_ref[gidx])
            pre_a.append(both[:, 0:group])
            pre_x.append(both[:, group:2 * group])
        return jnp.concatenate(pre_a, axis=-1), jnp.concatenate(pre_x, axis=-1)

    def recurrence(xc, pre_a, pre_x):
        tanh_a = jnp.tanh(0.5 * (pre_a + b_a_ref[...]))
        tanh_x = jnp.tanh(0.5 * (pre_x + b_x_ref[...]))
        a = jnp.exp2(tanh_a * half_rate + half_rate)
        half_xc = 0.5 * xc
        gated_in = tanh_x * half_xc + half_xc
        y = 1.0 - a * a
        a_s[...] = a
        b_s[...] = jnp.where(y > 0.0, y * lax.rsqrt(y), 0.0) * gated_in

        h_loc = jnp.zeros((sub, width), _F32)
        a_cum = jnp.ones((sub, width), _F32)
        for j in range(steps):
            rows = slice(j * sub, (j + 1) * sub)
            a_j = a_s[rows, :]
            h_loc = a_j * h_loc + b_s[rows, :]
            a_cum = a_j * a_cum
            h_s[rows, :] = h_loc
            ac_s[rows, :] = a_cum

        state = carry[...]
        starts = []
        for c in range(sub):
            starts.append(state)
            state = h_loc[c:c + 1, :] + a_cum[c:c + 1, :] * state
        carry[...] = state
        start = jnp.tile(jnp.concatenate(starts, axis=0), (steps, 1))
        return h_s[...] + ac_s[...] * start

    def out_projection(t, h):
        rows = slice(t * seq_tile, (t + 1) * seq_tile)
        gated = (h * _gelu_tanh(gate_pre[cur, t])).astype(_BF16)
        y = _dot(gated, w_out_ref[...]) + alpha * x_ref[rows, :]
        out_ref[rows, :] = _layer_norm(y, g_ref[...], b_ref[...])

    convolved = [conv(t) for t in range(n_tiles)]
    gates = [gate_projection(convolved[0])]
    for t in range(n_tiles):
        in_projection(x_next_ref, nxt, t)
        if t + 1 < n_tiles:
            gates.append(gate_projection(convolved[t + 1]))
        out_projection(t, recurrence(convolved[t], *gates[t]))


def _lru_layer(h, batch, seq, layer, w_in, conv_w, conv_b, w_gate, b_a, b_x, lam, w_out, g, b,
               alpha):
    m, d = h.shape
    width = w_out.shape[1]
    rows = LRU_TILES * STREAM_TILE
    steps_per_seq = seq // rows
    n_steps = m // rows
    halo = (CONV_WIDTH - 1) * V7X_SUBLANES
    tile_f32 = pltpu.VMEM((STREAM_TILE, width), _F32)
    row_spec = pl.BlockSpec((rows, d), lambda i: (i, 0))
    next_spec = pl.BlockSpec((rows, d), lambda i: (jnp.minimum(i + 1, n_steps - 1), 0))
    return pl.pallas_call(
        functools.partial(_lru_kernel, alpha=alpha, steps_per_seq=steps_per_seq),
        grid=(n_steps,),
        in_specs=[row_spec, next_spec, _resident_layer(w_in.shape, layer),
                  _resident(conv_w.shape),
                  _resident(conv_b.shape), _resident_layer(w_gate.shape, layer),
                  _resident(b_a.shape), _resident(b_x.shape), _resident(lam.shape),
                  _resident_layer(w_out.shape, layer), _resident(g.shape), _resident(b.shape)],
        out_specs=row_spec,
        out_shape=jax.ShapeDtypeStruct((m, d), _F32),
        scratch_shapes=[pltpu.VMEM((2, LRU_TILES, halo + STREAM_TILE, width), _F32),
                        pltpu.VMEM((2, LRU_TILES, STREAM_TILE, width), _F32),
                        pltpu.VMEM((halo, width), _F32),
                        tile_f32,
                        tile_f32,
                        tile_f32,
                        tile_f32,
                        pltpu.VMEM((1, width), _F32)],
        compiler_params=_compiler_params(("arbitrary",)),
        name="rglru_block_ln",
    )(h, h, w_in, conv_w, conv_b, w_gate, b_a, b_x, lam, w_out, g, b)


def kernel(x, attn_w_qkv, attn_w_o, lru_w_in, lru_conv_w, lru_conv_b, lru_w_a, lru_b_a,
           lru_w_x, lru_b_x, lru_lambda, lru_w_out, ffn_w_in, ffn_w_out, ln_g, ln_b):
    batch, seq, d = x.shape
    depth = ffn_w_in.shape[0]
    rows = batch * seq
    assert d % N_HEADS == 0 and N_HEADS % MOBA_HEADS_PER_STEP == 0
    assert seq % ROW_TILE == 0 and ROW_TILE % MOBA_BLOCK == 0
    assert seq % (LRU_TILES * STREAM_TILE) == 0
    assert rows % FFN_ROW_TILE == 0 and FFN_ROW_TILE % FFN_SUB_ROWS == 0
    assert rows % OUT_PROJ_ROW_TILE == 0 and ffn_w_out.shape[1] % FFN_CHUNK == 0
    alpha = (2 * depth) ** 0.25
    head_dim = d // N_HEADS
    n_blk = seq // MOBA_BLOCK
    slopes = jnp.exp2(-8.0 * (jnp.arange(N_HEADS, dtype=_F32) + 1.0) / N_HEADS)
    row = lambda v: v.reshape(1, -1)

    w_qkv, w_o = attn_w_qkv.astype(_BF16), attn_w_o.astype(_BF16)
    w_lru_in, w_lru_out = lru_w_in.astype(_BF16), lru_w_out.astype(_BF16)
    w_gate = jnp.concatenate([lru_w_a, lru_w_x], axis=-1).astype(_BF16)
    w_ffn_in, w_ffn_out = ffn_w_in.astype(_BF16), ffn_w_out.astype(_BF16)

    h = x.reshape(rows, d)
    for layer in range(depth):
        j = layer // 2
        g0, b0 = row(ln_g[layer, 0]), row(ln_b[layer, 0])
        g1, b1 = row(ln_g[layer, 1]), row(ln_b[layer, 1])
        if layer % 2 == 0:
            qk, vt, kmean, *stream = _qkv_projection(h, w_qkv, j, head_dim ** -0.5 * LOG2_E,
                                                     batch, seq, time_ordered_input=layer == 0)
            if stream:
                h, = stream
            o = _moba_attention(qk, vt, kmean.reshape(batch, n_blk, d), slopes, batch, seq)
            h = _proj_residual_ln(o, w_o, j, h, g0, b0, alpha)
        else:
            h = _lru_layer(h, batch, seq, j, w_lru_in, lru_conv_w[j], row(lru_conv_b[j]),
                           w_gate, row(lru_b_a[j]), row(lru_b_x[j]), row(lru_lambda[j]),
                           w_lru_out, g0, b0, alpha)
        h = _ffn_layer(h, w_ffn_in, w_ffn_out, layer, g1, b1, alpha,
                       time_ordered_output=layer == depth - 1)
    return h.reshape(batch, seq, d)
```

```python
import functools

import jax
import jax.numpy as jnp
from jax import lax
from jax.experimental import pallas as pl
from jax.experimental.pallas import tpu as pltpu

N_HEADS = 8
MOBA_BLOCK = 256
MOBA_TOPK = 3
NEG_INF = -1e30
LRU_BLOCKS = 8
CONV_WIDTH = 4
LRU_C = 8.0
LN_EPS = 1e-5
LOG2_E = 1.4426950408889634

V7X_SUBLANES = 8
V7X_LANES = 128
V7X_VMEM_LIMIT_BYTES = 56 * 1024 * 1024

ROW_TILE = 1024
STREAM_TILE = MOBA_BLOCK
FFN_CHUNK = 256
FFN_ROW_TILE = 1024
FFN_SUB_ROWS = 512
LRU_TILES = 2
MOBA_HEADS_PER_STEP = 4
_F32 = jnp.float32
_BF16 = jnp.bfloat16
_NT = (((1,), (1,)), ((), ()))


def _compiler_params(semantics):
    return pltpu.CompilerParams(dimension_semantics=semantics,
                                vmem_limit_bytes=V7X_VMEM_LIMIT_BYTES)


def _resident(shape):
    zeros = (0,) * len(shape)
    return pl.BlockSpec(shape, lambda *_: zeros, pipeline_mode=pl.Buffered(1))


def _resident_layer(stacked_shape, layer):
    index = (layer,) + (0,) * (len(stacked_shape) - 1)
    return pl.BlockSpec((None,) + tuple(stacked_shape[1:]), lambda *_: index,
                        pipeline_mode=pl.Buffered(1))


def _stream_rows(natural_group):
    steps = STREAM_TILE // V7X_SUBLANES
    chunk, first_step = divmod(natural_group * V7X_SUBLANES, steps)
    return pl.ds(first_step * V7X_SUBLANES + chunk, V7X_SUBLANES, stride=V7X_SUBLANES)


def _natural_to_stream(src_ref, slab):
    tiles, lane_groups = slab.shape[0], slab.shape[1]
    for t in range(tiles):
        for g in range(lane_groups):
            lanes = slice(g * V7X_LANES, (g + 1) * V7X_LANES)
            for group in range(STREAM_TILE // V7X_SUBLANES):
                row0 = t * STREAM_TILE + group * V7X_SUBLANES
                slab[t, g, _stream_rows(group), :] = src_ref[row0:row0 + V7X_SUBLANES, lanes]
    return jnp.concatenate(
        [jnp.concatenate([slab[t, g] for g in range(lane_groups)], axis=-1)
         for t in range(tiles)], axis=0)


def _stream_to_natural(y, slab, dst_ref, dst_row0):
    lane_groups = slab.shape[1]
    for t in range(y.shape[0] // STREAM_TILE):
        for g in range(lane_groups):
            lanes = slice(g * V7X_LANES, (g + 1) * V7X_LANES)
            slab[t, g] = y[t * STREAM_TILE:(t + 1) * STREAM_TILE, lanes]
            for group in range(STREAM_TILE // V7X_SUBLANES):
                row0 = dst_row0 + t * STREAM_TILE + group * V7X_SUBLANES
                dst_ref[row0:row0 + V7X_SUBLANES, lanes] = slab[t, g, _stream_rows(group), :]


def _reorder_slab(rows, d):
    return pltpu.VMEM((rows // STREAM_TILE, d // V7X_LANES, STREAM_TILE, V7X_LANES), _F32)


def _tile_time(idx):
    steps = STREAM_TILE // V7X_SUBLANES
    shift = V7X_SUBLANES.bit_length() - 1
    return (idx & (V7X_SUBLANES - 1)) * steps + (idx >> shift)


def _layer_norm(y, g, b):
    mu = jnp.mean(y, axis=-1, keepdims=True)
    yc = y - mu
    var = jnp.mean(yc * yc, axis=-1, keepdims=True)
    return yc * lax.rsqrt(var + LN_EPS) * g + b


def _dot(a, b):
    return jnp.dot(a, b, preferred_element_type=_F32)


def _qkv_kernel(x_ref, w_ref, qk_ref, vt_ref, kmean_ref, *stream_out, d_model, scale):
    if stream_out:
        stream_ref, slab = stream_out
        x = _natural_to_stream(x_ref, slab)
        stream_ref[...] = x
    else:
        x = x_ref[...]
    xb = x.astype(_BF16)
    rows = xb.shape[0]
    chunk = 512
    for c in range(0, 3 * d_model, chunk):
        acc = _dot(xb, w_ref[:, c:c + chunk])
        if c >= 2 * d_model:
            vt_ref[c - 2 * d_model:c - 2 * d_model + chunk, :] = acc.T.astype(_BF16)
            continue
        if c < d_model:
            acc = acc * scale
        else:
            for r in range(rows // MOBA_BLOCK):
                blk = acc[r * MOBA_BLOCK:(r + 1) * MOBA_BLOCK, :]
                kmean_ref[0, r:r + 1, c - d_model:c - d_model + chunk] = (
                    jnp.sum(blk, axis=0, keepdims=True) * (1.0 / MOBA_BLOCK))
        qk_ref[:, c:c + chunk] = acc.astype(_BF16)


def _qkv_projection(h, w_qkv, layer, scale, batch, seq, time_ordered_input=False):
    m, d = h.shape
    blocks_per_tile = ROW_TILE // MOBA_BLOCK
    tiles_per_seq = seq // ROW_TILE
    row_spec = pl.BlockSpec((ROW_TILE, d), lambda i: (i, 0))
    out_specs = [pl.BlockSpec((ROW_TILE, 2 * d), lambda i: (i, 0)),
                 pl.BlockSpec((None, d, ROW_TILE),
                              lambda i: (i // tiles_per_seq, 0, i % tiles_per_seq)),
                 pl.BlockSpec((1, blocks_per_tile, d), lambda i: (i, 0, 0))]
    out_shape = [jax.ShapeDtypeStruct((m, 2 * d), _BF16),
                 jax.ShapeDtypeStruct((batch, d, seq), _BF16),
                 jax.ShapeDtypeStruct((m // ROW_TILE, blocks_per_tile, d), _F32)]
    scratch = []
    if time_ordered_input:
        out_specs.append(row_spec)
        out_shape.append(jax.ShapeDtypeStruct((m, d), _F32))
        scratch.append(_reorder_slab(ROW_TILE, d))
    return pl.pallas_call(
        functools.partial(_qkv_kernel, d_model=d, scale=scale),
        grid=(m // ROW_TILE,),
        in_specs=[row_spec, _resident_layer(w_qkv.shape, layer)],
        out_specs=out_specs,
        out_shape=out_shape,
        scratch_shapes=scratch,
        compiler_params=_compiler_params(("arbitrary",)),
        name="qkv_projection",
    )(h, w_qkv)


def _split_bf16(x, parts):
    terms = []
    for _ in range(parts):
        t = x.astype(_BF16)
        terms.append(t)
        x = x - t.astype(_F32)
    return terms


def _moba_kernel(slopes_ref, q_ref, k_ref, vt_ref, kmean_ref, o_ref,
                 k_aug, vt_aug, causal, s_scr, p_scr, *, n_blk):
    blk = MOBA_BLOCK
    n_heads, seq, dh2 = k_aug.shape
    dh = dh2 // 2
    bias_lanes = 3
    head_cols = lambda hd: slice(hd * dh, (hd + 1) * dh)

    @pl.when((pl.program_id(0) == 0) & (pl.program_id(1) == 0))
    def _():
        key_time = _tile_time(lax.broadcasted_iota(jnp.int32, (seq, dh), 0) & (blk - 1))
        key_lane = lax.broadcasted_iota(jnp.int32, (seq, dh), 1)
        extra = vt_aug.shape[1] - dh
        ones_row = lax.broadcasted_iota(jnp.int32, (extra, seq), 0) == 0
        for hd in range(n_heads):
            k_aug[hd, :, dh:2 * dh] = jnp.where(key_lane < bias_lanes, key_time, 0).astype(_BF16)
            vt_aug[hd, dh:dh + extra, :] = jnp.where(ones_row, 1.0, 0.0).astype(_BF16)
        key_t = _tile_time(lax.broadcasted_iota(jnp.int32, (blk, blk), 0))
        query_t = _tile_time(lax.broadcasted_iota(jnp.int32, (blk, blk), 1))
        causal[...] = jnp.where(key_t > query_t, NEG_INF, 0.0)

    lane = lax.broadcasted_iota(jnp.int32, (blk, dh), 1)
    block_id = lax.broadcasted_iota(jnp.int32, (n_blk, blk), 0)
    slope2, q_bias, km_split = [], [], []
    for hd in range(n_heads):
        k_aug[hd, :, 0:dh] = k_ref[:, head_cols(hd)]
        vt_aug[hd, 0:dh, :] = vt_ref[head_cols(hd), :]
        slope2.append(slopes_ref[pl.program_id(1) * n_heads + hd] * LOG2_E)
        bias = jnp.zeros((blk, dh), _F32)
        for i, term in enumerate(_split_bf16(jnp.full((blk, dh), slope2[hd], _F32), bias_lanes)):
            bias = jnp.where(lane == i, term.astype(_F32), bias)
        q_bias.append(bias.astype(_BF16))
        km_split.append(_split_bf16(kmean_ref[:, head_cols(hd)], 2))

    def score_stage(hd, qi):
        width = (qi + 1) * blk
        q = q_ref[qi * blk:(qi + 1) * blk, head_cols(hd)]
        s_scr[hd, qi % 2, 0:width, :] = lax.dot_general(
            k_aug[hd, 0:width, :], jnp.concatenate([q, q_bias[hd]], axis=-1), _NT,
            preferred_element_type=_F32)
        if qi <= MOBA_TOPK:
            return None
        km_hi, km_lo = km_split[hd]
        gate = (lax.dot_general(km_hi, q, _NT, preferred_element_type=_F32)
                + lax.dot_general(km_lo, q, _NT, preferred_element_type=_F32))
        ahead = jnp.zeros((n_blk, blk), _F32)
        for m in range(qi):
            g_m = gate[m:m + 1, :]
            wins = (g_m > gate) | ((g_m == gate) & (block_id > m))
            ahead = ahead + jnp.where(wins, 1.0, 0.0)
        return (ahead < float(MOBA_TOPK)) & (block_id < qi)

    def softmax_stage(hd, qi, picked):
        s_buf, p_buf = s_scr.at[hd, qi % 2], p_scr.at[hd, qi % 2]

        def block_scores(n):
            t = s_buf[n * blk:(n + 1) * blk, :]
            return t + causal[...] if n == qi else t

        query_term = [slope2[hd] * float(-blk * (qi - n)) for n in range(qi)]
        if picked is not None:
            query_term = [jnp.where(picked[n:n + 1, :], query_term[n], NEG_INF)
                          for n in range(qi)]
        m_run = jnp.max(block_scores(qi), axis=0, keepdims=True)
        for n in range(qi):
            m_run = jnp.maximum(m_run,
                                jnp.max(block_scores(n), axis=0, keepdims=True) + query_term[n])
        for n in range(qi + 1):
            shift = m_run if n == qi else m_run - query_term[n]
            p_buf[n * blk:(n + 1) * blk, :] = jnp.exp2(block_scores(n) - shift).astype(_BF16)

    def output_stage(hd, qi):
        width = (qi + 1) * blk
        out = _dot(vt_aug[hd, :, 0:width], p_scr[hd, qi % 2, 0:width, :])
        o_ref[qi * blk:(qi + 1) * blk, head_cols(hd)] = (
            out[0:dh, :] / out[dh:dh + 1, :]).T.astype(o_ref.dtype)

    heads = range(n_heads)
    picked = [score_stage(hd, 0) for hd in heads]
    for qi in range(n_blk):
        picked_next = [score_stage(hd, qi + 1) if qi + 1 < n_blk else None for hd in heads]
        for hd in heads:
            softmax_stage(hd, qi, picked[hd])
        for hd in heads:
            output_stage(hd, qi)
        picked = picked_next


def _moba_attention(qk, vt, kmean, slopes, batch, seq):
    m, d2 = qk.shape
    d = d2 // 2
    dh = d // N_HEADS
    n_blk = seq // MOBA_BLOCK
    bf16_rows = 2 * V7X_SUBLANES
    hps = MOBA_HEADS_PER_STEP
    groups = N_HEADS // hps
    wide = hps * dh
    return pl.pallas_call(
        functools.partial(_moba_kernel, n_blk=n_blk),
        grid=(batch, groups),
        in_specs=[pl.BlockSpec(memory_space=pltpu.SMEM),
                  pl.BlockSpec((seq, wide), lambda b, g: (b, g)),
                  pl.BlockSpec((seq, wide), lambda b, g: (b, groups + g)),
                  pl.BlockSpec((None, wide, seq), lambda b, g: (b, g, 0)),
                  pl.BlockSpec((None, n_blk, wide), lambda b, g: (b, 0, g))],
        out_specs=pl.BlockSpec((seq, wide), lambda b, g: (b, g)),
        out_shape=jax.ShapeDtypeStruct((m, d), _BF16),
        scratch_shapes=[pltpu.VMEM((hps, seq, 2 * dh), _BF16),
                        pltpu.VMEM((hps, dh + bf16_rows, seq), _BF16),
                        pltpu.VMEM((MOBA_BLOCK, MOBA_BLOCK), _F32),
                        pltpu.VMEM((hps, 2, seq, MOBA_BLOCK), _F32),
                        pltpu.VMEM((hps, 2, seq, MOBA_BLOCK), _BF16)],
        compiler_params=_compiler_params(("arbitrary", "arbitrary")),
        name="moba_attention",
    )(slopes, qk, qk, vt, kmean)


def _ffn_kernel(*refs, hidden, alpha, mixer_projection, time_ordered_output):
    if mixer_projection:
        o_ref, w_o_ref, g0_ref, b0_ref, *refs = refs
    x_ref, w_in_ref, w_out_ref, g_ref, b_ref, out_ref, *slab = refs
    groups = range(0, x_ref.shape[0], FFN_SUB_ROWS)
    inputs = [x_ref[r:r + FFN_SUB_ROWS, :] for r in groups]
    if mixer_projection:
        inputs = [_layer_norm(_dot(o_ref[r:r + FFN_SUB_ROWS, :], w_o_ref[...]) + alpha * x,
                              g0_ref[...], b0_ref[...]) for r, x in zip(groups, inputs)]
    for r, x in zip(groups, inputs):
        xb = x.astype(_BF16)
        acc = alpha * x
        for c in range(0, hidden, FFN_CHUNK):
            gate = _dot(xb, w_in_ref[:, c:c + FFN_CHUNK])
            up = _dot(xb, w_in_ref[:, hidden + c:hidden + c + FFN_CHUNK])
            act = (gate * jax.nn.sigmoid(gate) * up).astype(_BF16)
            acc = acc + _dot(act, w_out_ref[c:c + FFN_CHUNK, :])
        y = _layer_norm(acc, g_ref[...], b_ref[...])
        if time_ordered_output:
            _stream_to_natural(y, slab[0], out_ref, r)
        else:
            out_ref[r:r + FFN_SUB_ROWS, :] = y


def _ffn_layer(h, w_in, w_out, layer, g, b, alpha, mixer_projection=None,
               time_ordered_output=False):
    m, d = h.shape
    hidden = w_out.shape[1]
    row_spec = lambda width: pl.BlockSpec((FFN_ROW_TILE, width), lambda i: (i, 0))
    operands = [h, w_in, w_out, g, b]
    in_specs = [row_spec(d), _resident_layer(w_in.shape, layer),
                _resident_layer(w_out.shape, layer), _resident((1, d)), _resident((1, d))]
    if mixer_projection is not None:
        o, w_o, o_layer, g0, b0 = mixer_projection
        operands = [o, w_o, g0, b0] + operands
        in_specs = [row_spec(o.shape[1]), _resident_layer(w_o.shape, o_layer),
                    _resident((1, d)), _resident((1, d))] + in_specs
    return pl.pallas_call(
        functools.partial(_ffn_kernel, hidden=hidden, alpha=alpha,
                          mixer_projection=mixer_projection is not None,
                          time_ordered_output=time_ordered_output),
        grid=(m // FFN_ROW_TILE,),
        in_specs=in_specs,
        out_specs=row_spec(d),
        out_shape=jax.ShapeDtypeStruct((m, d), _F32),
        scratch_shapes=[_reorder_slab(FFN_SUB_ROWS, d)] if time_ordered_output else [],
        compiler_params=_compiler_params(("arbitrary",)),
        name="swiglu_ffn_ln",
    )(*operands)


def _gelu_tanh(x):
    inner = x * (0.7978845608028654 + (0.7978845608028654 * 0.044715) * (x * x))
    half_x = 0.5 * x
    return half_x * jnp.tanh(inner) + half_x


def _softplus(z):
    return jnp.maximum(z, 0.0) + jnp.log1p(jnp.exp(-jnp.abs(z)))


def _lru_kernel(x_ref, x_next_ref, w_in_ref, conv_w_ref, conv_b_ref, w_gate_ref, b_a_ref,
                b_x_ref, lam_ref, w_out_ref, g_ref, b_ref, out_ref,
                xbuf, gate_pre, tail, a_s, b_s, h_s, ac_s, carry, *, alpha, steps_per_seq):
    sub = V7X_SUBLANES
    seq_tile, width = a_s.shape
    n_tiles = x_ref.shape[0] // seq_tile
    steps = seq_tile // sub
    halo = (CONV_WIDTH - 1) * sub
    group = width // LRU_BLOCKS
    step = pl.program_id(0)
    cur, nxt = step % 2, (step + 1) % 2

    def in_projection(src_ref, slot, t):
        xb = src_ref[t * seq_tile:(t + 1) * seq_tile, :].astype(_BF16)
        xbuf[slot, t, halo:halo + seq_tile, :] = _dot(xb, w_in_ref[:, 0:width])
        gate_pre[slot, t] = _dot(xb, w_in_ref[:, width:2 * width])

    @pl.when(step == 0)
    def _():
        for t in range(n_tiles):
            in_projection(x_ref, 0, t)

    @pl.when(step % steps_per_seq == 0)
    def _():
        tail[...] = jnp.zeros_like(tail)
        carry[...] = jnp.zeros_like(carry)

    half_rate = (-0.5 * LRU_C * LOG2_E) * _softplus(-lam_ref[...])

    def conv(t):
        first_sublane = lax.broadcasted_iota(jnp.int32, (sub, width), 0) == 0
        for g in range(CONV_WIDTH - 1):
            rows = slice(g * sub, (g + 1) * sub)
            last = xbuf[cur, t, seq_tile + g * sub:seq_tile + (g + 1) * sub, :]
            xbuf[cur, t, rows, :] = jnp.where(first_sublane,
                                              pltpu.roll(tail[rows, :], 1, axis=0),
                                              pltpu.roll(last, 1, axis=0))
            tail[rows, :] = last
        xc = conv_b_ref[...]
        for tap in range(CONV_WIDTH):
            start = halo - (CONV_WIDTH - 1 - tap) * sub
            xc = xc + xbuf[cur, t, start:start + seq_tile, :] * conv_w_ref[tap:tap + 1, :]
        return xc

    def gate_projection(xc):
        xcb = xc.astype(_BF16)
        pre_a, pre_x = [], []
        for gidx in range(LRU_BLOCKS):
            both = _dot(xcb[:, gidx * group:(gidx + 1) * group], w_gate_ref[gidx])
            pre_a.append(both[:, 0:group])
            pre_x.append(both[:, group:2 * group])
        return jnp.concatenate(pre_a, axis=-1), jnp.concatenate(pre_x, axis=-1)

    def recurrence(xc, pre_a, pre_x):
        tanh_a = jnp.tanh(0.5 * (pre_a + b_a_ref[...]))
        tanh_x = jnp.tanh(0.5 * (pre_x + b_x_ref[...]))
        a = jnp.exp2(tanh_a * half_rate + half_rate)
        half_xc = 0.5 * xc
        gated_in = tanh_x * half_xc + half_xc
        y = 1.0 - a * a
        a_s[...] = a
        b_s[...] = jnp.where(y > 0.0, y * lax.rsqrt(y), 0.0) * gated_in

        h_loc = jnp.zeros((sub, width), _F32)
        a_cum = jnp.ones((sub, width), _F32)
        for j in range(steps):
            rows = slice(j * sub, (j + 1) * sub)
            a_j = a_s[rows, :]
            h_loc = a_j * h_loc + b_s[rows, :]
            a_cum = a_j * a_cum
            h_s[rows, :] = h_loc
            ac_s[rows, :] = a_cum

        state = carry[...]
        starts = []
        for c in range(sub):
            starts.append(state)
            state = h_loc[c:c + 1, :] + a_cum[c:c + 1, :] * state
        carry[...] = state
        start = jnp.tile(jnp.concatenate(starts, axis=0), (steps, 1))
        return h_s[...] + ac_s[...] * start

    def out_projection(t, h):
        rows = slice(t * seq_tile, (t + 1) * seq_tile)
        gated = (h * _gelu_tanh(gate_pre[cur, t])).astype(_BF16)
        y = _dot(gated, w_out_ref[...]) + alpha * x_ref[rows, :]
        out_ref[rows, :] = _layer_norm(y, g_ref[...], b_ref[...])

    convolved = [conv(t) for t in range(n_tiles)]
    gates = [gate_projection(convolved[0])]
    for t in range(n_tiles):
        in_projection(x_next_ref, nxt, t)
        if t + 1 < n_tiles:
            gates.append(gate_projection(convolved[t + 1]))
        out_projection(t, recurrence(convolved[t], *gates[t]))


def _lru_layer(h, batch, seq, layer, w_in, conv_w, conv_b, w_gate, b_a, b_x, lam, w_out, g, b,
               alpha):
    m, d = h.shape
    width = w_out.shape[1]
    rows = LRU_TILES * STREAM_TILE
    steps_per_seq = seq // rows
    n_steps = m // rows
    halo = (CONV_WIDTH - 1) * V7X_SUBLANES
    tile_f32 = pltpu.VMEM((STREAM_TILE, width), _F32)
    row_spec = pl.BlockSpec((rows, d), lambda i: (i, 0))
    next_spec = pl.BlockSpec((rows, d), lambda i: (jnp.minimum(i + 1, n_steps - 1), 0))
    return pl.pallas_call(
        functools.partial(_lru_kernel, alpha=alpha, steps_per_seq=steps_per_seq),
        grid=(n_steps,),
        in_specs=[row_spec, next_spec, _resident_layer(w_in.shape, layer),
                  _resident(conv_w.shape),
                  _resident(conv_b.shape), _resident_layer(w_gate.shape, layer),
                  _resident(b_a.shape), _resident(b_x.shape), _resident(lam.shape),
                  _resident_layer(w_out.shape, layer), _resident(g.shape), _resident(b.shape)],
        out_specs=row_spec,
        out_shape=jax.ShapeDtypeStruct((m, d), _F32),
        scratch_shapes=[pltpu.VMEM((2, LRU_TILES, halo + STREAM_TILE, width), _F32),
                        pltpu.VMEM((2, LRU_TILES, STREAM_TILE, width), _F32),
                        pltpu.VMEM((halo, width), _F32),
                        tile_f32,
                        tile_f32,
                        tile_f32,
                        tile_f32,
                        pltpu.VMEM((1, width), _F32)],
        compiler_params=_compiler_params(("arbitrary",)),
        name="rglru_block_ln",
    )(h, h, w_in, conv_w, conv_b, w_gate, b_a, b_x, lam, w_out, g, b)


def kernel(x, attn_w_qkv, attn_w_o, lru_w_in, lru_conv_w, lru_conv_b, lru_w_a, lru_b_a,
           lru_w_x, lru_b_x, lru_lambda, lru_w_out, ffn_w_in, ffn_w_out, ln_g, ln_b):
    batch, seq, d = x.shape
    depth = ffn_w_in.shape[0]
    rows = batch * seq
    assert d % N_HEADS == 0 and N_HEADS % MOBA_HEADS_PER_STEP == 0
    assert seq % ROW_TILE == 0 and ROW_TILE % MOBA_BLOCK == 0
    assert seq % (LRU_TILES * STREAM_TILE) == 0
    assert rows % FFN_ROW_TILE == 0 and FFN_ROW_TILE % FFN_SUB_ROWS == 0
    assert ffn_w_out.shape[1] % FFN_CHUNK == 0
    alpha = (2 * depth) ** 0.25
    head_dim = d // N_HEADS
    n_blk = seq // MOBA_BLOCK
    slopes = jnp.exp2(-8.0 * (jnp.arange(N_HEADS, dtype=_F32) + 1.0) / N_HEADS)
    row = lambda v: v.reshape(1, -1)

    w_qkv, w_o = attn_w_qkv.astype(_BF16), attn_w_o.astype(_BF16)
    w_lru_in, w_lru_out = lru_w_in.astype(_BF16), lru_w_out.astype(_BF16)
    w_gate = jnp.concatenate([lru_w_a, lru_w_x], axis=-1).astype(_BF16)
    w_ffn_in, w_ffn_out = ffn_w_in.astype(_BF16), ffn_w_out.astype(_BF16)

    h = x.reshape(rows, d)
    for layer in range(depth):
        j = layer // 2
        g0, b0 = row(ln_g[layer, 0]), row(ln_b[layer, 0])
        g1, b1 = row(ln_g[layer, 1]), row(ln_b[layer, 1])
        if layer % 2 == 0:
            qk, vt, kmean, *stream = _qkv_projection(h, w_qkv, j, head_dim ** -0.5 * LOG2_E,
                                                     batch, seq, time_ordered_input=layer == 0)
            if stream:
                h, = stream
            o = _moba_attention(qk, vt, kmean.reshape(batch, n_blk, d), slopes, batch, seq)
            mixer_projection = (o, w_o, j, g0, b0)
        else:
            h = _lru_layer(h, batch, seq, j, w_lru_in, lru_conv_w[j], row(lru_conv_b[j]),
                           w_gate, row(lru_b_a[j]), row(lru_b_x[j]), row(lru_lambda[j]),
                           w_lru_out, g0, b0, alpha)
            mixer_projection = None
        h = _ffn_layer(h, w_ffn_in, w_ffn_out, layer, g1, b1, alpha, mixer_projection,
                       time_ordered_output=layer == depth - 1)
    return h.reshape(batch, seq, d)
```

```python
import functools

import jax
import jax.numpy as jnp
from jax import lax
from jax.experimental import pallas as pl
from jax.experimental.pallas import tpu as pltpu

N_HEADS = 8
MOBA_BLOCK = 256
MOBA_TOPK = 3
NEG_INF = -1e30
LRU_BLOCKS = 8
CONV_WIDTH = 4
LRU_C = 8.0
LN_EPS = 1e-5
LOG2_E = 1.4426950408889634

V7X_SUBLANES = 8
V7X_LANES = 128
V7X_VMEM_LIMIT_BYTES = 56 * 1024 * 1024

ROW_TILE = 1024
STREAM_TILE = MOBA_BLOCK
FFN_CHUNK = 256
FFN_ROW_TILE = 1024
FFN_SUB_ROWS = 512
LRU_TILES = 4
MOBA_HEADS_PER_STEP = 4
_F32 = jnp.float32
_BF16 = jnp.bfloat16
_NT = (((1,), (1,)), ((), ()))


def _compiler_params(semantics):
    return pltpu.CompilerParams(dimension_semantics=semantics,
                                vmem_limit_bytes=V7X_VMEM_LIMIT_BYTES)


def _resident(shape):
    zeros = (0,) * len(shape)
    return pl.BlockSpec(shape, lambda *_: zeros, pipeline_mode=pl.Buffered(1))


def _resident_layer(stacked_shape, layer):
    index = (layer,) + (0,) * (len(stacked_shape) - 1)
    return pl.BlockSpec((None,) + tuple(stacked_shape[1:]), lambda *_: index,
                        pipeline_mode=pl.Buffered(1))


def _stream_rows(natural_group):
    steps = STREAM_TILE // V7X_SUBLANES
    chunk, first_step = divmod(natural_group * V7X_SUBLANES, steps)
    return pl.ds(first_step * V7X_SUBLANES + chunk, V7X_SUBLANES, stride=V7X_SUBLANES)


def _natural_to_stream(src_ref, slab):
    tiles, lane_groups = slab.shape[0], slab.shape[1]
    for t in range(tiles):
        for g in range(lane_groups):
            lanes = slice(g * V7X_LANES, (g + 1) * V7X_LANES)
            for group in range(STREAM_TILE // V7X_SUBLANES):
                row0 = t * STREAM_TILE + group * V7X_SUBLANES
                slab[t, g, _stream_rows(group), :] = src_ref[row0:row0 + V7X_SUBLANES, lanes]
    return jnp.concatenate(
        [jnp.concatenate([slab[t, g] for g in range(lane_groups)], axis=-1)
         for t in range(tiles)], axis=0)


def _stream_to_natural(y, slab, dst_ref, dst_row0):
    lane_groups = slab.shape[1]
    for t in range(y.shape[0] // STREAM_TILE):
        for g in range(lane_groups):
            lanes = slice(g * V7X_LANES, (g + 1) * V7X_LANES)
            slab[t, g] = y[t * STREAM_TILE:(t + 1) * STREAM_TILE, lanes]
            for group in range(STREAM_TILE // V7X_SUBLANES):
                row0 = dst_row0 + t * STREAM_TILE + group * V7X_SUBLANES
                dst_ref[row0:row0 + V7X_SUBLANES, lanes] = slab[t, g, _stream_rows(group), :]


def _reorder_slab(rows, d):
    return pltpu.VMEM((rows // STREAM_TILE, d // V7X_LANES, STREAM_TILE, V7X_LANES), _F32)


def _tile_time(idx):
    steps = STREAM_TILE // V7X_SUBLANES
    shift = V7X_SUBLANES.bit_length() - 1
    return (idx & (V7X_SUBLANES - 1)) * steps + (idx >> shift)


def _layer_norm(y, g, b):
    mu = jnp.mean(y, axis=-1, keepdims=True)
    yc = y - mu
    var = jnp.mean(yc * yc, axis=-1, keepdims=True)
    return yc * lax.rsqrt(var + LN_EPS) * g + b


def _dot(a, b):
    return jnp.dot(a, b, preferred_element_type=_F32)


def _qkv_kernel(x_ref, w_ref, qk_ref, vt_ref, kmean_ref, *stream_out, d_model, scale):
    if stream_out:
        stream_ref, slab = stream_out
        x = _natural_to_stream(x_ref, slab)
        stream_ref[...] = x
    else:
        x = x_ref[...]
    xb = x.astype(_BF16)
    rows = xb.shape[0]
    chunk = 512
    for c in range(0, 3 * d_model, chunk):
        acc = _dot(xb, w_ref[:, c:c + chunk])
        if c >= 2 * d_model:
            vt_ref[c - 2 * d_model:c - 2 * d_model + chunk, :] = acc.T.astype(_BF16)
            continue
        if c < d_model:
            acc = acc * scale
        else:
            for r in range(rows // MOBA_BLOCK):
                blk = acc[r * MOBA_BLOCK:(r + 1) * MOBA_BLOCK, :]
                kmean_ref[0, r:r + 1, c - d_model:c - d_model + chunk] = (
                    jnp.sum(blk, axis=0, keepdims=True) * (1.0 / MOBA_BLOCK))
        qk_ref[:, c:c + chunk] = acc.astype(_BF16)


def _qkv_projection(h, w_qkv, layer, scale, batch, seq, time_ordered_input=False):
    m, d = h.shape
    blocks_per_tile = ROW_TILE // MOBA_BLOCK
    tiles_per_seq = seq // ROW_TILE
    row_spec = pl.BlockSpec((ROW_TILE, d), lambda i: (i, 0))
    out_specs = [pl.BlockSpec((ROW_TILE, 2 * d), lambda i: (i, 0)),
                 pl.BlockSpec((None, d, ROW_TILE),
                              lambda i: (i // tiles_per_seq, 0, i % tiles_per_seq)),
                 pl.BlockSpec((1, blocks_per_tile, d), lambda i: (i, 0, 0))]
    out_shape = [jax.ShapeDtypeStruct((m, 2 * d), _BF16),
                 jax.ShapeDtypeStruct((batch, d, seq), _BF16),
                 jax.ShapeDtypeStruct((m // ROW_TILE, blocks_per_tile, d), _F32)]
    scratch = []
    if time_ordered_input:
        out_specs.append(row_spec)
        out_shape.append(jax.ShapeDtypeStruct((m, d), _F32))
        scratch.append(_reorder_slab(ROW_TILE, d))
    return pl.pallas_call(
        functools.partial(_qkv_kernel, d_model=d, scale=scale),
        grid=(m // ROW_TILE,),
        in_specs=[row_spec, _resident_layer(w_qkv.shape, layer)],
        out_specs=out_specs,
        out_shape=out_shape,
        scratch_shapes=scratch,
        compiler_params=_compiler_params(("arbitrary",)),
        name="qkv_projection",
    )(h, w_qkv)


def _split_bf16(x, parts):
    terms = []
    for _ in range(parts):
        t = x.astype(_BF16)
        terms.append(t)
        x = x - t.astype(_F32)
    return terms


def _moba_kernel(slopes_ref, q_ref, k_ref, vt_ref, kmean_ref, o_ref,
                 k_aug, vt_aug, causal, s_scr, p_scr, *, n_blk):
    blk = MOBA_BLOCK
    n_heads, seq, dh2 = k_aug.shape
    dh = dh2 // 2
    bias_lanes = 3
    head_cols = lambda hd: slice(hd * dh, (hd + 1) * dh)

    @pl.when((pl.program_id(0) == 0) & (pl.program_id(1) == 0))
    def _():
        key_time = _tile_time(lax.broadcasted_iota(jnp.int32, (seq, dh), 0) & (blk - 1))
        key_lane = lax.broadcasted_iota(jnp.int32, (seq, dh), 1)
        extra = vt_aug.shape[1] - dh
        ones_row = lax.broadcasted_iota(jnp.int32, (extra, seq), 0) == 0
        for hd in range(n_heads):
            k_aug[hd, :, dh:2 * dh] = jnp.where(key_lane < bias_lanes, key_time, 0).astype(_BF16)
            vt_aug[hd, dh:dh + extra, :] = jnp.where(ones_row, 1.0, 0.0).astype(_BF16)
        key_t = _tile_time(lax.broadcasted_iota(jnp.int32, (blk, blk), 0))
        query_t = _tile_time(lax.broadcasted_iota(jnp.int32, (blk, blk), 1))
        causal[...] = jnp.where(key_t > query_t, NEG_INF, 0.0)

    lane = lax.broadcasted_iota(jnp.int32, (blk, dh), 1)
    block_id = lax.broadcasted_iota(jnp.int32, (n_blk, blk), 0)
    slope2, q_bias, km_split = [], [], []
    for hd in range(n_heads):
        k_aug[hd, :, 0:dh] = k_ref[:, head_cols(hd)]
        vt_aug[hd, 0:dh, :] = vt_ref[head_cols(hd), :]
        slope2.append(slopes_ref[pl.program_id(1) * n_heads + hd] * LOG2_E)
        bias = jnp.zeros((blk, dh), _F32)
        for i, term in enumerate(_split_bf16(jnp.full((blk, dh), slope2[hd], _F32), bias_lanes)):
            bias = jnp.where(lane == i, term.astype(_F32), bias)
        q_bias.append(bias.astype(_BF16))
        km_split.append(_split_bf16(kmean_ref[:, head_cols(hd)], 2))

    def score_stage(hd, qi):
        width = (qi + 1) * blk
        q = q_ref[qi * blk:(qi + 1) * blk, head_cols(hd)]
        s_scr[hd, qi % 2, 0:width, :] = lax.dot_general(
            k_aug[hd, 0:width, :], jnp.concatenate([q, q_bias[hd]], axis=-1), _NT,
            preferred_element_type=_F32)
        if qi <= MOBA_TOPK:
            return None
        km_hi, km_lo = km_split[hd]
        gate = (lax.dot_general(km_hi, q, _NT, preferred_element_type=_F32)
                + lax.dot_general(km_lo, q, _NT, preferred_element_type=_F32))
        ahead = jnp.zeros((n_blk, blk), _F32)
        for m in range(qi):
            g_m = gate[m:m + 1, :]
            wins = (g_m > gate) | ((g_m == gate) & (block_id > m))
            ahead = ahead + jnp.where(wins, 1.0, 0.0)
        return (ahead < float(MOBA_TOPK)) & (block_id < qi)

    def softmax_stage(hd, qi, picked):
        s_buf, p_buf = s_scr.at[hd, qi % 2], p_scr.at[hd, qi % 2]

        def block_scores(n):
            t = s_buf[n * blk:(n + 1) * blk, :]
            return t + causal[...] if n == qi else t

        query_term = [slope2[hd] * float(-blk * (qi - n)) for n in range(qi)]
        if picked is not None:
            query_term = [jnp.where(picked[n:n + 1, :], query_term[n], NEG_INF)
                          for n in range(qi)]
        m_run = jnp.max(block_scores(qi), axis=0, keepdims=True)
        for n in range(qi):
            m_run = jnp.maximum(m_run,
                                jnp.max(block_scores(n), axis=0, keepdims=True) + query_term[n])
        for n in range(qi + 1):
            shift = m_run if n == qi else m_run - query_term[n]
            p_buf[n * blk:(n + 1) * blk, :] = jnp.exp2(block_scores(n) - shift).astype(_BF16)

    def output_stage(hd, qi):
        width = (qi + 1) * blk
        out = _dot(vt_aug[hd, :, 0:width], p_scr[hd, qi % 2, 0:width, :])
        o_ref[qi * blk:(qi + 1) * blk, head_cols(hd)] = (
            out[0:dh, :] / out[dh:dh + 1, :]).T.astype(o_ref.dtype)

    heads = range(n_heads)
    picked = [score_stage(hd, 0) for hd in heads]
    for qi in range(n_blk):
        picked_next = [score_stage(hd, qi + 1) if qi + 1 < n_blk else None for hd in heads]
        for hd in heads:
            softmax_stage(hd, qi, picked[hd])
        for hd in heads:
            output_stage(hd, qi)
        picked = picked_next


def _moba_attention(qk, vt, kmean, slopes, batch, seq):
    m, d2 = qk.shape
    d = d2 // 2
    dh = d // N_HEADS
    n_blk = seq // MOBA_BLOCK
    bf16_rows = 2 * V7X_SUBLANES
    hps = MOBA_HEADS_PER_STEP
    groups = N_HEADS // hps
    wide = hps * dh
    return pl.pallas_call(
        functools.partial(_moba_kernel, n_blk=n_blk),
        grid=(batch, groups),
        in_specs=[pl.BlockSpec(memory_space=pltpu.SMEM),
                  pl.BlockSpec((seq, wide), lambda b, g: (b, g)),
                  pl.BlockSpec((seq, wide), lambda b, g: (b, groups + g)),
                  pl.BlockSpec((None, wide, seq), lambda b, g: (b, g, 0)),
                  pl.BlockSpec((None, n_blk, wide), lambda b, g: (b, 0, g))],
        out_specs=pl.BlockSpec((seq, wide), lambda b, g: (b, g)),
        out_shape=jax.ShapeDtypeStruct((m, d), _BF16),
        scratch_shapes=[pltpu.VMEM((hps, seq, 2 * dh), _BF16),
                        pltpu.VMEM((hps, dh + bf16_rows, seq), _BF16),
                        pltpu.VMEM((MOBA_BLOCK, MOBA_BLOCK), _F32),
                        pltpu.VMEM((hps, 2, seq, MOBA_BLOCK), _F32),
                        pltpu.VMEM((hps, 2, seq, MOBA_BLOCK), _BF16)],
        compiler_params=_compiler_params(("arbitrary", "arbitrary")),
        name="moba_attention",
    )(slopes, qk, qk, vt, kmean)


def _ffn_kernel(*refs, hidden, alpha, mixer_projection, time_ordered_output):
    if mixer_projection:
        o_ref, w_o_ref, g0_ref, b0_ref, *refs = refs
    x_ref, w_in_ref, w_out_ref, g_ref, b_ref, out_ref, *slab = refs
    groups = range(0, x_ref.shape[0], FFN_SUB_ROWS)
    inputs = [x_ref[r:r + FFN_SUB_ROWS, :] for r in groups]
    if mixer_projection:
        inputs = [_layer_norm(_dot(o_ref[r:r + FFN_SUB_ROWS, :], w_o_ref[...]) + alpha * x,
                              g0_ref[...], b0_ref[...]) for r, x in zip(groups, inputs)]
    for r, x in zip(groups, inputs):
        xb = x.astype(_BF16)
        acc = alpha * x
        for c in range(0, hidden, FFN_CHUNK):
            gate = _dot(xb, w_in_ref[:, c:c + FFN_CHUNK])
            up = _dot(xb, w_in_ref[:, hidden + c:hidden + c + FFN_CHUNK])
            act = (gate * jax.nn.sigmoid(gate) * up).astype(_BF16)
            acc = acc + _dot(act, w_out_ref[c:c + FFN_CHUNK, :])
        y = _layer_norm(acc, g_ref[...], b_ref[...])
        if time_ordered_output:
            _stream_to_natural(y, slab[0], out_ref, r)
        else:
            out_ref[r:r + FFN_SUB_ROWS, :] = y


def _ffn_layer(h, w_in, w_out, layer, g, b, alpha, mixer_projection=None,
               time_ordered_output=False):
    m, d = h.shape
    hidden = w_out.shape[1]
    row_spec = lambda width: pl.BlockSpec((FFN_ROW_TILE, width), lambda i: (i, 0))
    operands = [h, w_in, w_out, g, b]
    in_specs = [row_spec(d), _resident_layer(w_in.shape, layer),
                _resident_layer(w_out.shape, layer), _resident((1, d)), _resident((1, d))]
    if mixer_projection is not None:
        o, w_o, o_layer, g0, b0 = mixer_projection
        operands = [o, w_o, g0, b0] + operands
        in_specs = [row_spec(o.shape[1]), _resident_layer(w_o.shape, o_layer),
                    _resident((1, d)), _resident((1, d))] + in_specs
    return pl.pallas_call(
        functools.partial(_ffn_kernel, hidden=hidden, alpha=alpha,
                          mixer_projection=mixer_projection is not None,
                          time_ordered_output=time_ordered_output),
        grid=(m // FFN_ROW_TILE,),
        in_specs=in_specs,
        out_specs=row_spec(d),
        out_shape=jax.ShapeDtypeStruct((m, d), _F32),
        scratch_shapes=[_reorder_slab(FFN_SUB_ROWS, d)] if time_ordered_output else [],
        compiler_params=_compiler_params(("arbitrary",)),
        name="swiglu_ffn_ln",
    )(*operands)


def _gelu_tanh(x):
    inner = x * (0.7978845608028654 + (0.7978845608028654 * 0.044715) * (x * x))
    half_x = 0.5 * x
    return half_x * jnp.tanh(inner) + half_x


def _softplus(z):
    return jnp.maximum(z, 0.0) + jnp.log1p(jnp.exp(-jnp.abs(z)))


def _lru_kernel(x_ref, w_in_ref, conv_w_ref, conv_b_ref, w_gate_ref, b_a_ref,
                b_x_ref, lam_ref, w_out_ref, g_ref, b_ref, out_ref,
                xbuf, gate_pre, tail, a_s, b_s, h_s, ac_s, carry, *, alpha, steps_per_seq):
    sub = V7X_SUBLANES
    seq_tile, width = a_s.shape
    n_tiles = x_ref.shape[0] // seq_tile
    steps = seq_tile // sub
    halo = (CONV_WIDTH - 1) * sub
    group = width // LRU_BLOCKS
    step = pl.program_id(0)

    def in_projection(t):
        xb = x_ref[t * seq_tile:(t + 1) * seq_tile, :].astype(_BF16)
        xbuf[t % 2, halo:halo + seq_tile, :] = _dot(xb, w_in_ref[:, 0:width])
        gate_pre[t % 2] = _dot(xb, w_in_ref[:, width:2 * width])

    @pl.when(step % steps_per_seq == 0)
    def _():
        tail[...] = jnp.zeros_like(tail)
        carry[...] = jnp.zeros_like(carry)

    half_rate = (-0.5 * LRU_C * LOG2_E) * _softplus(-lam_ref[...])

    def conv(t):
        first_sublane = lax.broadcasted_iota(jnp.int32, (sub, width), 0) == 0
        for g in range(CONV_WIDTH - 1):
            rows = slice(g * sub, (g + 1) * sub)
            last = xbuf[t % 2, seq_tile + g * sub:seq_tile + (g + 1) * sub, :]
            xbuf[t % 2, rows, :] = jnp.where(first_sublane,
                                             pltpu.roll(tail[rows, :], 1, axis=0),
                                             pltpu.roll(last, 1, axis=0))
            tail[rows, :] = last
        xc = conv_b_ref[...]
        for tap in range(CONV_WIDTH):
            start = halo - (CONV_WIDTH - 1 - tap) * sub
            xc = xc + xbuf[t % 2, start:start + seq_tile, :] * conv_w_ref[tap:tap + 1, :]
        return xc

    def gate_projection(xc):
        xcb = xc.astype(_BF16)
        pre_a, pre_x = [], []
        for gidx in range(LRU_BLOCKS):
            both = _dot(xcb[:, gidx * group:(gidx + 1) * group], w_gate_ref[gidx])
            pre_a.append(both[:, 0:group])
            pre_x.append(both[:, group:2 * group])
        return jnp.concatenate(pre_a, axis=-1), jnp.concatenate(pre_x, axis=-1)

    def recurrence(xc, pre_a, pre_x):
        tanh_a = jnp.tanh(0.5 * (pre_a + b_a_ref[...]))
        tanh_x = jnp.tanh(0.5 * (pre_x + b_x_ref[...]))
        a = jnp.exp2(tanh_a * half_rate + half_rate)
        half_xc = 0.5 * xc
        gated_in = tanh_x * half_xc + half_xc
        y = 1.0 - a * a
        a_s[...] = a
        b_s[...] = jnp.where(y > 0.0, y * lax.rsqrt(y), 0.0) * gated_in

        h_loc = jnp.zeros((sub, width), _F32)
        a_cum = jnp.ones((sub, width), _F32)
        for j in range(steps):
            rows = slice(j * sub, (j + 1) * sub)
            a_j = a_s[rows, :]
            h_loc = a_j * h_loc + b_s[rows, :]
            a_cum = a_j * a_cum
            h_s[rows, :] = h_loc
            ac_s[rows, :] = a_cum

        state = carry[...]
        starts = []
        for c in range(sub):
            starts.append(state)
            state = h_loc[c:c + 1, :] + a_cum[c:c + 1, :] * state
        carry[...] = state
        start = jnp.tile(jnp.concatenate(starts, axis=0), (steps, 1))
        return h_s[...] + ac_s[...] * start

    def out_projection(t, h):
        rows = slice(t * seq_tile, (t + 1) * seq_tile)
        gated = (h * _gelu_tanh(gate_pre[t % 2])).astype(_BF16)
        y = _dot(gated, w_out_ref[...]) + alpha * x_ref[rows, :]
        out_ref[rows, :] = _layer_norm(y, g_ref[...], b_ref[...])

    in_projection(0)
    for t in range(n_tiles):
        xc = conv(t)
        pre_a, pre_x = gate_projection(xc)
        if t + 1 < n_tiles:
            in_projection(t + 1)
        out_projection(t, recurrence(xc, pre_a, pre_x))


def _lru_layer(h, batch, seq, layer, w_in, conv_w, conv_b, w_gate, b_a, b_x, lam, w_out, g, b,
               alpha):
    m, d = h.shape
    width = w_out.shape[1]
    rows = LRU_TILES * STREAM_TILE
    steps_per_seq = seq // rows
    n_steps = m // rows
    halo = (CONV_WIDTH - 1) * V7X_SUBLANES
    tile_f32 = pltpu.VMEM((STREAM_TILE, width), _F32)
    row_spec = pl.BlockSpec((rows, d), lambda i: (i, 0))
    return pl.pallas_call(
        functools.partial(_lru_kernel, alpha=alpha, steps_per_seq=steps_per_seq),
        grid=(n_steps,),
        in_specs=[row_spec, _resident_layer(w_in.shape, layer),
                  _resident(conv_w.shape),
                  _resident(conv_b.shape), _resident_layer(w_gate.shape, layer),
                  _resident(b_a.shape), _resident(b_x.shape), _resident(lam.shape),
                  _resident_layer(w_out.shape, layer), _resident(g.shape), _resident(b.shape)],
        out_specs=row_spec,
        out_shape=jax.ShapeDtypeStruct((m, d), _F32),
        scratch_shapes=[pltpu.VMEM((2, halo + STREAM_TILE, width), _F32),
                        pltpu.VMEM((2, STREAM_TILE, width), _F32),
                        pltpu.VMEM((halo, width), _F32),
                        tile_f32,
                        tile_f32,
                        tile_f32,
                        tile_f32,
                        pltpu.VMEM((1, width), _F32)],
        compiler_params=_compiler_params(("arbitrary",)),
        name="rglru_block_ln",
    )(h, w_in, conv_w, conv_b, w_gate, b_a, b_x, lam, w_out, g, b)


def kernel(x, attn_w_qkv, attn_w_o, lru_w_in, lru_conv_w, lru_conv_b, lru_w_a, lru_b_a,
           lru_w_x, lru_b_x, lru_lambda, lru_w_out, ffn_w_in, ffn_w_out, ln_g, ln_b):
    batch, seq, d = x.shape
    depth = ffn_w_in.shape[0]
    rows = batch * seq
    assert d % N_HEADS == 0 and N_HEADS % MOBA_HEADS_PER_STEP == 0
    assert seq % ROW_TILE == 0 and ROW_TILE % MOBA_BLOCK == 0
    assert seq % (LRU_TILES * STREAM_TILE) == 0
    assert rows % FFN_ROW_TILE == 0 and FFN_ROW_TILE % FFN_SUB_ROWS == 0
    assert ffn_w_out.shape[1] % FFN_CHUNK == 0
    alpha = (2 * depth) ** 0.25
    head_dim = d // N_HEADS
    n_blk = seq // MOBA_BLOCK
    slopes = jnp.exp2(-8.0 * (jnp.arange(N_HEADS, dtype=_F32) + 1.0) / N_HEADS)
    row = lambda v: v.reshape(1, -1)

    w_qkv, w_o = attn_w_qkv.astype(_BF16), attn_w_o.astype(_BF16)
    w_lru_in, w_lru_out = lru_w_in.astype(_BF16), lru_w_out.astype(_BF16)
    w_gate = jnp.concatenate([lru_w_a, lru_w_x], axis=-1).astype(_BF16)
    w_ffn_in, w_ffn_out = ffn_w_in.astype(_BF16), ffn_w_out.astype(_BF16)

    h = x.reshape(rows, d)
    for layer in range(depth):
        j = layer // 2
        g0, b0 = row(ln_g[layer, 0]), row(ln_b[layer, 0])
        g1, b1 = row(ln_g[layer, 1]), row(ln_b[layer, 1])
        if layer % 2 == 0:
            qk, vt, kmean, *stream = _qkv_projection(h, w_qkv, j, head_dim ** -0.5 * LOG2_E,
                                                     batch, seq, time_ordered_input=layer == 0)
            if stream:
                h, = stream
            o = _moba_attention(qk, vt, kmean.reshape(batch, n_blk, d), slopes, batch, seq)
            mixer_projection = (o, w_o, j, g0, b0)
        else:
            h = _lru_layer(h, batch, seq, j, w_lru_in, lru_conv_w[j], row(lru_conv_b[j]),
                           w_gate, row(lru_b_a[j]), row(lru_b_x[j]), row(lru_lambda[j]),
                           w_lru_out, g0, b0, alpha)
            mixer_projection = None
        h = _ffn_layer(h, w_ffn_in, w_ffn_out, layer, g1, b1, alpha, mixer_projection,
                       time_ordered_output=layer == depth - 1)
    return h.reshape(batch, seq, d)
```

```python
import functools

import jax
import jax.numpy as jnp
from jax import lax
from jax.experimental import pallas as pl
from jax.experimental.pallas import tpu as pltpu

N_HEADS = 8
MOBA_BLOCK = 256
MOBA_TOPK = 3
NEG_INF = -1e30
LRU_BLOCKS = 8
CONV_WIDTH = 4
LRU_C = 8.0
LN_EPS = 1e-5
LOG2_E = 1.4426950408889634

V7X_SUBLANES = 8
V7X_LANES = 128
V7X_VMEM_LIMIT_BYTES = 56 * 1024 * 1024

ROW_TILE = 1024
STREAM_TILE = MOBA_BLOCK
FFN_CHUNK = 256
FFN_ROW_TILE = 1024
FFN_SUB_ROWS = 512
LRU_TILES = 2
MOBA_HEADS_PER_STEP = 4
_F32 = jnp.float32
_BF16 = jnp.bfloat16
_NT = (((1,), (1,)), ((), ()))


def _compiler_params(semantics):
    return pltpu.CompilerParams(dimension_semantics=semantics,
                                vmem_limit_bytes=V7X_VMEM_LIMIT_BYTES)


def _resident(shape):
    zeros = (0,) * len(shape)
    return pl.BlockSpec(shape, lambda *_: zeros, pipeline_mode=pl.Buffered(1))


def _resident_layer(stacked_shape, layer):
    index = (layer,) + (0,) * (len(stacked_shape) - 1)
    return pl.BlockSpec((None,) + tuple(stacked_shape[1:]), lambda *_: index,
                        pipeline_mode=pl.Buffered(1))


def _stream_rows(natural_group):
    steps = STREAM_TILE // V7X_SUBLANES
    chunk, first_step = divmod(natural_group * V7X_SUBLANES, steps)
    return pl.ds(first_step * V7X_SUBLANES + chunk, V7X_SUBLANES, stride=V7X_SUBLANES)


def _natural_to_stream(src_ref, slab):
    tiles, lane_groups = slab.shape[0], slab.shape[1]
    for t in range(tiles):
        for g in range(lane_groups):
            lanes = slice(g * V7X_LANES, (g + 1) * V7X_LANES)
            for group in range(STREAM_TILE // V7X_SUBLANES):
                row0 = t * STREAM_TILE + group * V7X_SUBLANES
                slab[t, g, _stream_rows(group), :] = src_ref[row0:row0 + V7X_SUBLANES, lanes]
    return jnp.concatenate(
        [jnp.concatenate([slab[t, g] for g in range(lane_groups)], axis=-1)
         for t in range(tiles)], axis=0)


def _stream_to_natural(y, slab, dst_ref, dst_row0):
    lane_groups = slab.shape[1]
    for t in range(y.shape[0] // STREAM_TILE):
        for g in range(lane_groups):
            lanes = slice(g * V7X_LANES, (g + 1) * V7X_LANES)
            slab[t, g] = y[t * STREAM_TILE:(t + 1) * STREAM_TILE, lanes]
            for group in range(STREAM_TILE // V7X_SUBLANES):
                row0 = dst_row0 + t * STREAM_TILE + group * V7X_SUBLANES
                dst_ref[row0:row0 + V7X_SUBLANES, lanes] = slab[t, g, _stream_rows(group), :]


def _reorder_slab(rows, d):
    return pltpu.VMEM((rows // STREAM_TILE, d // V7X_LANES, STREAM_TILE, V7X_LANES), _F32)


def _tile_time(idx):
    steps = STREAM_TILE // V7X_SUBLANES
    shift = V7X_SUBLANES.bit_length() - 1
    return (idx & (V7X_SUBLANES - 1)) * steps + (idx >> shift)


def _layer_norm(y, g, b):
    mu = jnp.mean(y, axis=-1, keepdims=True)
    yc = y - mu
    var = jnp.mean(yc * yc, axis=-1, keepdims=True)
    return yc * lax.rsqrt(var + LN_EPS) * g + b


def _dot(a, b):
    return jnp.dot(a, b, preferred_element_type=_F32)


def _qkv_kernel(x_ref, w_ref, qk_ref, vt_ref, kmean_ref, *stream_out, d_model, scale):
    if stream_out:
        stream_ref, slab = stream_out
        x = _natural_to_stream(x_ref, slab)
        stream_ref[...] = x
    else:
        x = x_ref[...]
    xb = x.astype(_BF16)
    rows = xb.shape[0]
    chunk = 512
    for c in range(0, 3 * d_model, chunk):
        acc = _dot(xb, w_ref[:, c:c + chunk])
        if c >= 2 * d_model:
            vt_ref[c - 2 * d_model:c - 2 * d_model + chunk, :] = acc.T.astype(_BF16)
            continue
        if c < d_model:
            acc = acc * scale
        else:
            for r in range(rows // MOBA_BLOCK):
                blk = acc[r * MOBA_BLOCK:(r + 1) * MOBA_BLOCK, :]
                kmean_ref[0, r:r + 1, c - d_model:c - d_model + chunk] = (
                    jnp.sum(blk, axis=0, keepdims=True) * (1.0 / MOBA_BLOCK))
        qk_ref[:, c:c + chunk] = acc.astype(_BF16)


def _qkv_projection(h, w_qkv, layer, scale, batch, seq, time_ordered_input=False):
    m, d = h.shape
    blocks_per_tile = ROW_TILE // MOBA_BLOCK
    tiles_per_seq = seq // ROW_TILE
    row_spec = pl.BlockSpec((ROW_TILE, d), lambda i: (i, 0))
    out_specs = [pl.BlockSpec((ROW_TILE, 2 * d), lambda i: (i, 0)),
                 pl.BlockSpec((None, d, ROW_TILE),
                              lambda i: (i // tiles_per_seq, 0, i % tiles_per_seq)),
                 pl.BlockSpec((1, blocks_per_tile, d), lambda i: (i, 0, 0))]
    out_shape = [jax.ShapeDtypeStruct((m, 2 * d), _BF16),
                 jax.ShapeDtypeStruct((batch, d, seq), _BF16),
                 jax.ShapeDtypeStruct((m // ROW_TILE, blocks_per_tile, d), _F32)]
    scratch = []
    if time_ordered_input:
        out_specs.append(row_spec)
        out_shape.append(jax.ShapeDtypeStruct((m, d), _F32))
        scratch.append(_reorder_slab(ROW_TILE, d))
    return pl.pallas_call(
        functools.partial(_qkv_kernel, d_model=d, scale=scale),
        grid=(m // ROW_TILE,),
        in_specs=[row_spec, _resident_layer(w_qkv.shape, layer)],
        out_specs=out_specs,
        out_shape=out_shape,
        scratch_shapes=scratch,
        compiler_params=_compiler_params(("arbitrary",)),
        name="qkv_projection",
    )(h, w_qkv)


def _split_bf16(x, parts):
    terms = []
    for _ in range(parts):
        t = x.astype(_BF16)
        terms.append(t)
        x = x - t.astype(_F32)
    return terms


def _moba_kernel(slopes_ref, q_ref, k_ref, vt_ref, kmean_ref, o_ref,
                 k_aug, vt_aug, causal, s_scr, p_scr, *, n_blk):
    blk = MOBA_BLOCK
    n_heads, seq, dh2 = k_aug.shape
    dh = dh2 // 2
    bias_lanes = 3
    head_cols = lambda hd: slice(hd * dh, (hd + 1) * dh)

    @pl.when((pl.program_id(0) == 0) & (pl.program_id(1) == 0))
    def _():
        key_time = _tile_time(lax.broadcasted_iota(jnp.int32, (seq, dh), 0) & (blk - 1))
        key_lane = lax.broadcasted_iota(jnp.int32, (seq, dh), 1)
        extra = vt_aug.shape[1] - dh
        ones_row = lax.broadcasted_iota(jnp.int32, (extra, seq), 0) == 0
        for hd in range(n_heads):
            k_aug[hd, :, dh:2 * dh] = jnp.where(key_lane < bias_lanes, key_time, 0).astype(_BF16)
            vt_aug[hd, dh:dh + extra, :] = jnp.where(ones_row, 1.0, 0.0).astype(_BF16)
        key_t = _tile_time(lax.broadcasted_iota(jnp.int32, (blk, blk), 0))
        query_t = _tile_time(lax.broadcasted_iota(jnp.int32, (blk, blk), 1))
        causal[...] = jnp.where(key_t > query_t, NEG_INF, 0.0)

    lane = lax.broadcasted_iota(jnp.int32, (blk, dh), 1)
    block_id = lax.broadcasted_iota(jnp.int32, (n_blk, blk), 0)
    slope2, q_bias, km_split = [], [], []
    for hd in range(n_heads):
        k_aug[hd, :, 0:dh] = k_ref[:, head_cols(hd)]
        vt_aug[hd, 0:dh, :] = vt_ref[head_cols(hd), :]
        slope2.append(slopes_ref[pl.program_id(1) * n_heads + hd] * LOG2_E)
        bias = jnp.zeros((blk, dh), _F32)
        for i, term in enumerate(_split_bf16(jnp.full((blk, dh), slope2[hd], _F32), bias_lanes)):
            bias = jnp.where(lane == i, term.astype(_F32), bias)
        q_bias.append(bias.astype(_BF16))
        km_split.append(_split_bf16(kmean_ref[:, head_cols(hd)], 2))

    def score_stage(hd, qi):
        width = (qi + 1) * blk
        q = q_ref[qi * blk:(qi + 1) * blk, head_cols(hd)]
        s_scr[hd, qi % 2, 0:width, :] = lax.dot_general(
            k_aug[hd, 0:width, :], jnp.concatenate([q, q_bias[hd]], axis=-1), _NT,
            preferred_element_type=_F32)
        if qi <= MOBA_TOPK:
            return None
        km_hi, km_lo = km_split[hd]
        gate = (lax.dot_general(km_hi, q, _NT, preferred_element_type=_F32)
                + lax.dot_general(km_lo, q, _NT, preferred_element_type=_F32))
        ahead = jnp.zeros((n_blk, blk), _F32)
        for m in range(qi):
            g_m = gate[m:m + 1, :]
            wins = (g_m > gate) | ((g_m == gate) & (block_id > m))
            ahead = ahead + jnp.where(wins, 1.0, 0.0)
        return (ahead < float(MOBA_TOPK)) & (block_id < qi)

    def softmax_stage(hd, qi, picked):
        s_buf, p_buf = s_scr.at[hd, qi % 2], p_scr.at[hd, qi % 2]

        def block_scores(n):
            t = s_buf[n * blk:(n + 1) * blk, :]
            return t + causal[...] if n == qi else t

        query_term = [slope2[hd] * float(-blk * (qi - n)) for n in range(qi)]
        if picked is not None:
            query_term = [jnp.where(picked[n:n + 1, :], query_term[n], NEG_INF)
                          for n in range(qi)]
        m_run = jnp.max(block_scores(qi), axis=0, keepdims=True)
        for n in range(qi):
            m_run = jnp.maximum(m_run,
                                jnp.max(block_scores(n), axis=0, keepdims=True) + query_term[n])
        for n in range(qi + 1):
            shift = m_run if n == qi else m_run - query_term[n]
            p_buf[n * blk:(n + 1) * blk, :] = jnp.exp2(block_scores(n) - shift).astype(_BF16)

    def output_stage(hd, qi):
        width = (qi + 1) * blk
        out = _dot(vt_aug[hd, :, 0:width], p_scr[hd, qi % 2, 0:width, :])
        o_ref[qi * blk:(qi + 1) * blk, head_cols(hd)] = (
            out[0:dh, :] / out[dh:dh + 1, :]).T.astype(o_ref.dtype)

    heads = range(n_heads)
    picked = [score_stage(hd, 0) for hd in heads]
    for qi in range(n_blk):
        picked_next = []
        for hd in heads:
            picked_next.append(score_stage(hd, qi + 1) if qi + 1 < n_blk else None)
            softmax_stage(hd, qi, picked[hd])
            if hd > 0:
                output_stage(hd - 1, qi)
        output_stage(n_heads - 1, qi)
        picked = picked_next


def _moba_attention(qk, vt, kmean, slopes, batch, seq):
    m, d2 = qk.shape
    d = d2 // 2
    dh = d // N_HEADS
    n_blk = seq // MOBA_BLOCK
    bf16_rows = 2 * V7X_SUBLANES
    hps = MOBA_HEADS_PER_STEP
    groups = N_HEADS // hps
    wide = hps * dh
    return pl.pallas_call(
        functools.partial(_moba_kernel, n_blk=n_blk),
        grid=(batch, groups),
        in_specs=[pl.BlockSpec(memory_space=pltpu.SMEM),
                  pl.BlockSpec((seq, wide), lambda b, g: (b, g)),
                  pl.BlockSpec((seq, wide), lambda b, g: (b, groups + g)),
                  pl.BlockSpec((None, wide, seq), lambda b, g: (b, g, 0)),
                  pl.BlockSpec((None, n_blk, wide), lambda b, g: (b, 0, g))],
        out_specs=pl.BlockSpec((seq, wide), lambda b, g: (b, g)),
        out_shape=jax.ShapeDtypeStruct((m, d), _BF16),
        scratch_shapes=[pltpu.VMEM((hps, seq, 2 * dh), _BF16),
                        pltpu.VMEM((hps, dh + bf16_rows, seq), _BF16),
                        pltpu.VMEM((MOBA_BLOCK, MOBA_BLOCK), _F32),
                        pltpu.VMEM((hps, 2, seq, MOBA_BLOCK), _F32),
                        pltpu.VMEM((hps, 2, seq, MOBA_BLOCK), _BF16)],
        compiler_params=_compiler_params(("arbitrary", "arbitrary")),
        name="moba_attention",
    )(slopes, qk, qk, vt, kmean)


def _ffn_kernel(*refs, hidden, alpha, mixer_projection, time_ordered_output):
    if mixer_projection:
        o_ref, w_o_ref, g0_ref, b0_ref, *refs = refs
    x_ref, w_in_ref, w_out_ref, g_ref, b_ref, out_ref, *slab = refs
    groups = range(0, x_ref.shape[0], FFN_SUB_ROWS)
    inputs = [x_ref[r:r + FFN_SUB_ROWS, :] for r in groups]
    if mixer_projection:
        inputs = [_layer_norm(_dot(o_ref[r:r + FFN_SUB_ROWS, :], w_o_ref[...]) + alpha * x,
                              g0_ref[...], b0_ref[...]) for r, x in zip(groups, inputs)]
    for r, x in zip(groups, inputs):
        xb = x.astype(_BF16)
        acc = alpha * x
        for c in range(0, hidden, FFN_CHUNK):
            gate = _dot(xb, w_in_ref[:, c:c + FFN_CHUNK])
            up = _dot(xb, w_in_ref[:, hidden + c:hidden + c + FFN_CHUNK])
            act = (gate * jax.nn.sigmoid(gate) * up).astype(_BF16)
            acc = acc + _dot(act, w_out_ref[c:c + FFN_CHUNK, :])
        y = _layer_norm(acc, g_ref[...], b_ref[...])
        if time_ordered_output:
            _stream_to_natural(y, slab[0], out_ref, r)
        else:
            out_ref[r:r + FFN_SUB_ROWS, :] = y


def _ffn_layer(h, w_in, w_out, layer, g, b, alpha, mixer_projection=None,
               time_ordered_output=False):
    m, d = h.shape
    hidden = w_out.shape[1]
    row_spec = lambda width: pl.BlockSpec((FFN_ROW_TILE, width), lambda i: (i, 0))
    operands = [h, w_in, w_out, g, b]
    in_specs = [row_spec(d), _resident_layer(w_in.shape, layer),
                _resident_layer(w_out.shape, layer), _resident((1, d)), _resident((1, d))]
    if mixer_projection is not None:
        o, w_o, o_layer, g0, b0 = mixer_projection
        operands = [o, w_o, g0, b0] + operands
        in_specs = [row_spec(o.shape[1]), _resident_layer(w_o.shape, o_layer),
                    _resident((1, d)), _resident((1, d))] + in_specs
    return pl.pallas_call(
        functools.partial(_ffn_kernel, hidden=hidden, alpha=alpha,
                          mixer_projection=mixer_projection is not None,
                          time_ordered_output=time_ordered_output),
        grid=(m // FFN_ROW_TILE,),
        in_specs=in_specs,
        out_specs=row_spec(d),
        out_shape=jax.ShapeDtypeStruct((m, d), _F32),
        scratch_shapes=[_reorder_slab(FFN_SUB_ROWS, d)] if time_ordered_output else [],
        compiler_params=_compiler_params(("arbitrary",)),
        name="swiglu_ffn_ln",
    )(*operands)


def _gelu_tanh(x):
    inner = x * (0.7978845608028654 + (0.7978845608028654 * 0.044715) * (x * x))
    half_x = 0.5 * x
    return half_x * jnp.tanh(inner) + half_x


def _softplus(z):
    return jnp.maximum(z, 0.0) + jnp.log1p(jnp.exp(-jnp.abs(z)))


def _lru_kernel(x_ref, x_next_ref, w_in_ref, conv_w_ref, conv_b_ref, w_gate_ref, b_a_ref,
                b_x_ref, lam_ref, w_out_ref, g_ref, b_ref, out_ref,
                xbuf, gate_pre, tail, a_s, b_s, h_s, ac_s, carry, *, alpha, steps_per_seq):
    sub = V7X_SUBLANES
    seq_tile, width = a_s.shape
    n_tiles = x_ref.shape[0] // seq_tile
    steps = seq_tile // sub
    halo = (CONV_WIDTH - 1) * sub
    group = width // LRU_BLOCKS
    step = pl.program_id(0)
    cur, nxt = step % 2, (step + 1) % 2

    def in_projection(src_ref, slot, t):
        xb = src_ref[t * seq_tile:(t + 1) * seq_tile, :].astype(_BF16)
        xbuf[slot, t, halo:halo + seq_tile, :] = _dot(xb, w_in_ref[:, 0:width])
        gate_pre[slot, t] = _dot(xb, w_in_ref[:, width:2 * width])

    @pl.when(step == 0)
    def _():
        for t in range(n_tiles):
            in_projection(x_ref, 0, t)

    @pl.when(step % steps_per_seq == 0)
    def _():
        tail[...] = jnp.zeros_like(tail)
        carry[...] = jnp.zeros_like(carry)

    half_rate = (-0.5 * LRU_C * LOG2_E) * _softplus(-lam_ref[...])

    def conv(t):
        first_sublane = lax.broadcasted_iota(jnp.int32, (sub, width), 0) == 0
        for g in range(CONV_WIDTH - 1):
            rows = slice(g * sub, (g + 1) * sub)
            last = xbuf[cur, t, seq_tile + g * sub:seq_tile + (g + 1) * sub, :]
            xbuf[cur, t, rows, :] = jnp.where(first_sublane,
                                              pltpu.roll(tail[rows, :], 1, axis=0),
                                              pltpu.roll(last, 1, axis=0))
            tail[rows, :] = last
        xc = conv_b_ref[...]
        for tap in range(CONV_WIDTH):
            start = halo - (CONV_WIDTH - 1 - tap) * sub
            xc = xc + xbuf[cur, t, start:start + seq_tile, :] * conv_w_ref[tap:tap + 1, :]
        return xc

    def gate_projection(xc):
        xcb = xc.astype(_BF16)
        pre_a, pre_x = [], []
        for gidx in range(LRU_BLOCKS):
            both = _dot(xcb[:, gidx * group:(gidx + 1) * group], w_gate_ref[gidx])
            pre_a.append(both[:, 0:group])
            pre_x.append(both[:, group:2 * group])
        return jnp.concatenate(pre_a, axis=-1), jnp.concatenate(pre_x, axis=-1)

    def recurrence(xc, pre_a, pre_x):
        tanh_a = jnp.tanh(0.5 * (pre_a + b_a_ref[...]))
        tanh_x = jnp.tanh(0.5 * (pre_x + b_x_ref[...]))
        a = jnp.exp2(tanh_a * half_rate + half_rate)
        half_xc = 0.5 * xc
        gated_in = tanh_x * half_xc + half_xc
        y = 1.0 - a * a
        a_s[...] = a
        b_s[...] = jnp.where(y > 0.0, y * lax.rsqrt(y), 0.0) * gated_in

        h_loc = jnp.zeros((sub, width), _F32)
        a_cum = jnp.ones((sub, width), _F32)
        for j in range(steps):
            rows = slice(j * sub, (j + 1) * sub)
            a_j = a_s[rows, :]
            h_loc = a_j * h_loc + b_s[rows, :]
            a_cum = a_j * a_cum
            h_s[rows, :] = h_loc
            ac_s[rows, :] = a_cum

        state = carry[...]
        starts = []
        for c in range(sub):
            starts.append(state)
            state = h_loc[c:c + 1, :] + a_cum[c:c + 1, :] * state
        carry[...] = state
        start = jnp.tile(jnp.concatenate(starts, axis=0), (steps, 1))
        return h_s[...] + ac_s[...] * start

    def out_projection(t, h):
        rows = slice(t * seq_tile, (t + 1) * seq_tile)
        gated = (h * _gelu_tanh(gate_pre[cur, t])).astype(_BF16)
        y = _dot(gated, w_out_ref[...]) + alpha * x_ref[rows, :]
        out_ref[rows, :] = _layer_norm(y, g_ref[...], b_ref[...])

    convolved = [conv(t) for t in range(n_tiles)]
    gates = [gate_projection(convolved[0])]
    for t in range(n_tiles):
        in_projection(x_next_ref, nxt, t)
        if t + 1 < n_tiles:
            gates.append(gate_projection(convolved[t + 1]))
        out_projection(t, recurrence(convolved[t], *gates[t]))


def _lru_layer(h, batch, seq, layer, w_in, conv_w, conv_b, w_gate, b_a, b_x, lam, w_out, g, b,
               alpha):
    m, d = h.shape
    width = w_out.shape[1]
    rows = LRU_TILES * STREAM_TILE
    steps_per_seq = seq // rows
    n_steps = m // rows
    halo = (CONV_WIDTH - 1) * V7X_SUBLANES
    tile_f32 = pltpu.VMEM((STREAM_TILE, width), _F32)
    row_spec = pl.BlockSpec((rows, d), lambda i: (i, 0))
    next_spec = pl.BlockSpec((rows, d), lambda i: (jnp.minimum(i + 1, n_steps - 1), 0))
    return pl.pallas_call(
        functools.partial(_lru_kernel, alpha=alpha, steps_per_seq=steps_per_seq),
        grid=(n_steps,),
        in_specs=[row_spec, next_spec, _resident_layer(w_in.shape, layer),
                  _resident(conv_w.shape),
                  _resident(conv_b.shape), _resident_layer(w_gate.shape, layer),
                  _resident(b_a.shape), _resident(b_x.shape), _resident(lam.shape),
                  _resident_layer(w_out.shape, layer), _resident(g.shape), _resident(b.shape)],
        out_specs=row_spec,
        out_shape=jax.ShapeDtypeStruct((m, d), _F32),
        scratch_shapes=[pltpu.VMEM((2, LRU_TILES, halo + STREAM_TILE, width), _F32),
                        pltpu.VMEM((2, LRU_TILES, STREAM_TILE, width), _F32),
                        pltpu.VMEM((halo, width), _F32),
                        tile_f32,
                        tile_f32,
                        tile_f32,
                        tile_f32,
                        pltpu.VMEM((1, width), _F32)],
        compiler_params=_compiler_params(("arbitrary",)),
        name="rglru_block_ln",
    )(h, h, w_in, conv_w, conv_b, w_gate, b_a, b_x, lam, w_out, g, b)


def kernel(x, attn_w_qkv, attn_w_o, lru_w_in, lru_conv_w, lru_conv_b, lru_w_a, lru_b_a,
           lru_w_x, lru_b_x, lru_lambda, lru_w_out, ffn_w_in, ffn_w_out, ln_g, ln_b):
    batch, seq, d = x.shape
    depth = ffn_w_in.shape[0]
    rows = batch * seq
    assert d % N_HEADS == 0 and N_HEADS % MOBA_HEADS_PER_STEP == 0
    assert seq % ROW_TILE == 0 and ROW_TILE % MOBA_BLOCK == 0
    assert seq % (LRU_TILES * STREAM_TILE) == 0
    assert rows % FFN_ROW_TILE == 0 and FFN_ROW_TILE % FFN_SUB_ROWS == 0
    assert ffn_w_out.shape[1] % FFN_CHUNK == 0
    alpha = (2 * depth) ** 0.25
    head_dim = d // N_HEADS
    n_blk = seq // MOBA_BLOCK
    slopes = jnp.exp2(-8.0 * (jnp.arange(N_HEADS, dtype=_F32) + 1.0) / N_HEADS)
    row = lambda v: v.reshape(1, -1)

    w_qkv, w_o = attn_w_qkv.astype(_BF16), attn_w_o.astype(_BF16)
    w_lru_in, w_lru_out = lru_w_in.astype(_BF16), lru_w_out.astype(_BF16)
    w_gate = jnp.concatenate([lru_w_a, lru_w_x], axis=-1).astype(_BF16)
    w_ffn_in, w_ffn_out = ffn_w_in.astype(_BF16), ffn_w_out.astype(_BF16)

    h = x.reshape(rows, d)
    for layer in range(depth):
        j = layer // 2
        g0, b0 = row(ln_g[layer, 0]), row(ln_b[layer, 0])
        g1, b1 = row(ln_g[layer, 1]), row(ln_b[layer, 1])
        if layer % 2 == 0:
            qk, vt, kmean, *stream = _qkv_projection(h, w_qkv, j, head_dim ** -0.5 * LOG2_E,
                                                     batch, seq, time_ordered_input=layer == 0)
            if stream:
                h, = stream
            o = _moba_attention(qk, vt, kmean.reshape(batch, n_blk, d), slopes, batch, seq)
            mixer_projection = (o, w_o, j, g0, b0)
        else:
            h = _lru_layer(h, batch, seq, j, w_lru_in, lru_conv_w[j], row(lru_conv_b[j]),
                           w_gate, row(lru_b_a[j]), row(lru_b_x[j]), row(lru_lambda[j]),
                           w_lru_out, g0, b0, alpha)
            mixer_projection = None
        h = _ffn_layer(h, w_ffn_in, w_ffn_out, layer, g1, b1, alpha, mixer_projection,
                       time_ordered_output=layer == depth - 1)
    return h.reshape(batch, seq, d)
```

```python
import functools

import jax
import jax.numpy as jnp
from jax import lax
from jax.experimental import pallas as pl
from jax.experimental.pallas import tpu as pltpu

N_HEADS = 8
MOBA_BLOCK = 256
MOBA_TOPK = 3
NEG_INF = -1e30
LRU_BLOCKS = 8
CONV_WIDTH = 4
LRU_C = 8.0
LN_EPS = 1e-5
LOG2_E = 1.4426950408889634

V7X_SUBLANES = 8
V7X_LANES = 128
V7X_VMEM_LIMIT_BYTES = 56 * 1024 * 1024

ROW_TILE = 1024
QKV_COL_CHUNK = 512
STREAM_TILE = MOBA_BLOCK
FFN_CHUNK = 256
FFN_ROW_TILE = 1024
FFN_SUB_ROWS = 512
LRU_TILES = 2
MOBA_HEADS_PER_STEP = 4

_F32 = jnp.float32
_BF16 = jnp.bfloat16
_NT = (((1,), (1,)), ((), ()))


def _compiler_params(semantics):
    return pltpu.CompilerParams(dimension_semantics=semantics,
                                vmem_limit_bytes=V7X_VMEM_LIMIT_BYTES)


def _resident(shape):
    zeros = (0,) * len(shape)
    return pl.BlockSpec(shape, lambda *_: zeros, pipeline_mode=pl.Buffered(1))


def _resident_layer(stacked_shape, layer):
    index = (layer,) + (0,) * (len(stacked_shape) - 1)
    return pl.BlockSpec((None,) + tuple(stacked_shape[1:]), lambda *_: index,
                        pipeline_mode=pl.Buffered(1))


def _stream_rows(natural_group):
    steps = STREAM_TILE // V7X_SUBLANES
    chunk, first_step = divmod(natural_group * V7X_SUBLANES, steps)
    return pl.ds(first_step * V7X_SUBLANES + chunk, V7X_SUBLANES, stride=V7X_SUBLANES)


def _natural_to_stream(src_ref, slab):
    tiles, lane_groups = slab.shape[0], slab.shape[1]
    for t in range(tiles):
        for g in range(lane_groups):
            lanes = slice(g * V7X_LANES, (g + 1) * V7X_LANES)
            for group in range(STREAM_TILE // V7X_SUBLANES):
                row0 = t * STREAM_TILE + group * V7X_SUBLANES
                slab[t, g, _stream_rows(group), :] = src_ref[row0:row0 + V7X_SUBLANES, lanes]
    return jnp.concatenate(
        [jnp.concatenate([slab[t, g] for g in range(lane_groups)], axis=-1)
         for t in range(tiles)], axis=0)


def _stream_to_natural(y, slab, dst_ref, dst_row0):
    lane_groups = slab.shape[1]
    for t in range(y.shape[0] // STREAM_TILE):
        for g in range(lane_groups):
            lanes = slice(g * V7X_LANES, (g + 1) * V7X_LANES)
            slab[t, g] = y[t * STREAM_TILE:(t + 1) * STREAM_TILE, lanes]
            for group in range(STREAM_TILE // V7X_SUBLANES):
                row0 = dst_row0 + t * STREAM_TILE + group * V7X_SUBLANES
                dst_ref[row0:row0 + V7X_SUBLANES, lanes] = slab[t, g, _stream_rows(group), :]


def _reorder_slab(rows, d):
    return pltpu.VMEM((rows // STREAM_TILE, d // V7X_LANES, STREAM_TILE, V7X_LANES), _F32)


def _tile_time(idx):
    steps = STREAM_TILE // V7X_SUBLANES
    shift = V7X_SUBLANES.bit_length() - 1
    return (idx & (V7X_SUBLANES - 1)) * steps + (idx >> shift)


def _layer_norm(y, g, b):
    mu = jnp.mean(y, axis=-1, keepdims=True)
    yc = y - mu
    var = jnp.mean(yc * yc, axis=-1, keepdims=True)
    return yc * lax.rsqrt(var + LN_EPS) * g + b


def _dot(a, b):
    return jnp.dot(a, b, preferred_element_type=_F32)


def _qkv_kernel(x_ref, w_ref, qk_ref, vt_ref, kmean_ref, *stream_out, d_model, scale):
    if stream_out:
        stream_ref, slab = stream_out
        x = _natural_to_stream(x_ref, slab)
        stream_ref[...] = x
    else:
        x = x_ref[...]
    xb = x.astype(_BF16)
    rows = xb.shape[0]
    chunk = QKV_COL_CHUNK
    for c in range(0, 3 * d_model, chunk):
        acc = _dot(xb, w_ref[:, c:c + chunk])
        if c >= 2 * d_model:
            vt_ref[c - 2 * d_model:c - 2 * d_model + chunk, :] = acc.T.astype(_BF16)
            continue
        if c < d_model:
            acc = acc * scale
        else:
            for r in range(rows // MOBA_BLOCK):
                blk = acc[r * MOBA_BLOCK:(r + 1) * MOBA_BLOCK, :]
                kmean_ref[0, r:r + 1, c - d_model:c - d_model + chunk] = (
                    jnp.sum(blk, axis=0, keepdims=True) * (1.0 / MOBA_BLOCK))
        qk_ref[:, c:c + chunk] = acc.astype(_BF16)


def _qkv_projection(h, w_qkv, layer, scale, batch, seq, time_ordered_input=False):
    m, d = h.shape
    blocks_per_tile = ROW_TILE // MOBA_BLOCK
    tiles_per_seq = seq // ROW_TILE
    row_spec = pl.BlockSpec((ROW_TILE, d), lambda i: (i, 0))
    out_specs = [pl.BlockSpec((ROW_TILE, 2 * d), lambda i: (i, 0)),
                 pl.BlockSpec((None, d, ROW_TILE),
                              lambda i: (i // tiles_per_seq, 0, i % tiles_per_seq)),
                 pl.BlockSpec((1, blocks_per_tile, d), lambda i: (i, 0, 0))]
    out_shape = [jax.ShapeDtypeStruct((m, 2 * d), _BF16),
                 jax.ShapeDtypeStruct((batch, d, seq), _BF16),
                 jax.ShapeDtypeStruct((m // ROW_TILE, blocks_per_tile, d), _F32)]
    scratch = []
    if time_ordered_input:
        out_specs.append(row_spec)
        out_shape.append(jax.ShapeDtypeStruct((m, d), _F32))
        scratch.append(_reorder_slab(ROW_TILE, d))
    return pl.pallas_call(
        functools.partial(_qkv_kernel, d_model=d, scale=scale),
        grid=(m // ROW_TILE,),
        in_specs=[row_spec, _resident_layer(w_qkv.shape, layer)],
        out_specs=out_specs,
        out_shape=out_shape,
        scratch_shapes=scratch,
        compiler_params=_compiler_params(("arbitrary",)),
        name="qkv_projection",
    )(h, w_qkv)


def _split_bf16(x, parts):
    terms = []
    for _ in range(parts):
        t = x.astype(_BF16)
        terms.append(t)
        x = x - t.astype(_F32)
    return terms


def _moba_kernel(slopes_ref, q_ref, k_ref, vt_ref, kmean_ref, o_ref,
                 k_aug, vt_aug, causal, s_scr, p_scr, *, n_blk):
    blk = MOBA_BLOCK
    n_heads, seq, dh2 = k_aug.shape
    dh = dh2 // 2
    bias_lanes = 3
    head_cols = lambda hd: slice(hd * dh, (hd + 1) * dh)

    @pl.when((pl.program_id(0) == 0) & (pl.program_id(1) == 0))
    def _():
        key_time = _tile_time(lax.broadcasted_iota(jnp.int32, (seq, dh), 0) & (blk - 1))
        key_lane = lax.broadcasted_iota(jnp.int32, (seq, dh), 1)
        extra = vt_aug.shape[1] - dh
        ones_row = lax.broadcasted_iota(jnp.int32, (extra, seq), 0) == 0
        for hd in range(n_heads):
            k_aug[hd, :, dh:2 * dh] = jnp.where(key_lane < bias_lanes, key_time, 0).astype(_BF16)
            vt_aug[hd, dh:dh + extra, :] = jnp.where(ones_row, 1.0, 0.0).astype(_BF16)
        key_t = _tile_time(lax.broadcasted_iota(jnp.int32, (blk, blk), 0))
        query_t = _tile_time(lax.broadcasted_iota(jnp.int32, (blk, blk), 1))
        causal[...] = jnp.where(key_t > query_t, NEG_INF, 0.0)

    lane = lax.broadcasted_iota(jnp.int32, (blk, dh), 1)
    block_id = lax.broadcasted_iota(jnp.int32, (n_blk, blk), 0)
    slope2, q_bias, km_split = [], [], []
    for hd in range(n_heads):
        k_aug[hd, :, 0:dh] = k_ref[:, head_cols(hd)]
        vt_aug[hd, 0:dh, :] = vt_ref[head_cols(hd), :]
        slope2.append(slopes_ref[pl.program_id(1) * n_heads + hd] * LOG2_E)
        bias = jnp.zeros((blk, dh), _F32)
        for i, term in enumerate(_split_bf16(jnp.full((blk, dh), slope2[hd], _F32), bias_lanes)):
            bias = jnp.where(lane == i, term.astype(_F32), bias)
        q_bias.append(bias.astype(_BF16))
        km_split.append(_split_bf16(kmean_ref[:, head_cols(hd)], 2))

    def score_stage(hd, qi):
        width = (qi + 1) * blk
        q = q_ref[qi * blk:(qi + 1) * blk, head_cols(hd)]
        s_scr[hd, qi % 2, 0:width, :] = lax.dot_general(
            k_aug[hd, 0:width, :], jnp.concatenate([q, q_bias[hd]], axis=-1), _NT,
            preferred_element_type=_F32)
        if qi <= MOBA_TOPK:
            return None
        km_hi, km_lo = km_split[hd]
        gate = (lax.dot_general(km_hi, q, _NT, preferred_element_type=_F32)
                + lax.dot_general(km_lo, q, _NT, preferred_element_type=_F32))
        ahead = jnp.zeros((n_blk, blk), _F32)
        for m in range(qi):
            g_m = gate[m:m + 1, :]
            wins = (g_m > gate) | ((g_m == gate) & (block_id > m))
            ahead = ahead + jnp.where(wins, 1.0, 0.0)
        return (ahead < float(MOBA_TOPK)) & (block_id < qi)

    def softmax_stage(hd, qi, picked):
        s_buf, p_buf = s_scr.at[hd, qi % 2], p_scr.at[hd, qi % 2]

        def block_scores(n):
            t = s_buf[n * blk:(n + 1) * blk, :]
            return t + causal[...] if n == qi else t

        query_term = [slope2[hd] * float(-blk * (qi - n)) for n in range(qi)]
        if picked is not None:
            query_term = [jnp.where(picked[n:n + 1, :], query_term[n], NEG_INF)
                          for n in range(qi)]
        m_run = jnp.max(block_scores(qi), axis=0, keepdims=True)
        for n in range(qi):
            m_run = jnp.maximum(m_run,
                                jnp.max(block_scores(n), axis=0, keepdims=True) + query_term[n])
        for n in range(qi + 1):
            shift = m_run if n == qi else m_run - query_term[n]
            p_buf[n * blk:(n + 1) * blk, :] = jnp.exp2(block_scores(n) - shift).astype(_BF16)

    def output_stage(hd, qi):
        width = (qi + 1) * blk
        out = _dot(vt_aug[hd, :, 0:width], p_scr[hd, qi % 2, 0:width, :])
        o_ref[qi * blk:(qi + 1) * blk, head_cols(hd)] = (
            out[0:dh, :] / out[dh:dh + 1, :]).T.astype(o_ref.dtype)

    heads = range(n_heads)
    picked = [score_stage(hd, 0) for hd in heads]
    for qi in range(n_blk):
        picked_next = []
        for hd in heads:
            picked_next.append(score_stage(hd, qi + 1) if qi + 1 < n_blk else None)
            softmax_stage(hd, qi, picked[hd])
            if hd > 0:
                output_stage(hd - 1, qi)
        output_stage(n_heads - 1, qi)
        picked = picked_next


def _moba_attention(qk, vt, kmean, slopes, batch, seq):
    m, d2 = qk.shape
    d = d2 // 2
    dh = d // N_HEADS
    n_blk = seq // MOBA_BLOCK
    bf16_rows = 2 * V7X_SUBLANES
    hps = MOBA_HEADS_PER_STEP
    groups = N_HEADS // hps
    wide = hps * dh
    return pl.pallas_call(
        functools.partial(_moba_kernel, n_blk=n_blk),
        grid=(batch, groups),
        in_specs=[pl.BlockSpec(memory_space=pltpu.SMEM),
                  pl.BlockSpec((seq, wide), lambda b, g: (b, g)),
                  pl.BlockSpec((seq, wide), lambda b, g: (b, groups + g)),
                  pl.BlockSpec((None, wide, seq), lambda b, g: (b, g, 0)),
                  pl.BlockSpec((None, n_blk, wide), lambda b, g: (b, 0, g))],
        out_specs=pl.BlockSpec((seq, wide), lambda b, g: (b, g)),
        out_shape=jax.ShapeDtypeStruct((m, d), _BF16),
        scratch_shapes=[pltpu.VMEM((hps, seq, 2 * dh), _BF16),
                        pltpu.VMEM((hps, dh + bf16_rows, seq), _BF16),
                        pltpu.VMEM((MOBA_BLOCK, MOBA_BLOCK), _F32),
                        pltpu.VMEM((hps, 2, seq, MOBA_BLOCK), _F32),
                        pltpu.VMEM((hps, 2, seq, MOBA_BLOCK), _BF16)],
        compiler_params=_compiler_params(("arbitrary", "arbitrary")),
        name="moba_attention",
    )(slopes, qk, qk, vt, kmean)


def _ffn_kernel(*refs, hidden, alpha, mixer_projection, time_ordered_output):
    if mixer_projection:
        o_ref, w_o_ref, g0_ref, b0_ref, *refs = refs
    x_ref, w_in_ref, w_out_ref, g_ref, b_ref, out_ref, *slab = refs
    groups = range(0, x_ref.shape[0], FFN_SUB_ROWS)
    inputs = [x_ref[r:r + FFN_SUB_ROWS, :] for r in groups]
    if mixer_projection:
        inputs = [_layer_norm(_dot(o_ref[r:r + FFN_SUB_ROWS, :], w_o_ref[...]) + alpha * x,
                              g0_ref[...], b0_ref[...]) for r, x in zip(groups, inputs)]
    for r, x in zip(groups, inputs):
        xb = x.astype(_BF16)
        acc = alpha * x
        for c in range(0, hidden, FFN_CHUNK):
            gate = _dot(xb, w_in_ref[:, c:c + FFN_CHUNK])
            up = _dot(xb, w_in_ref[:, hidden + c:hidden + c + FFN_CHUNK])
            act = (gate * jax.nn.sigmoid(gate) * up).astype(_BF16)
            acc = acc + _dot(act, w_out_ref[c:c + FFN_CHUNK, :])
        y = _layer_norm(acc, g_ref[...], b_ref[...])
        if time_ordered_output:
            _stream_to_natural(y, slab[0], out_ref, r)
        else:
            out_ref[r:r + FFN_SUB_ROWS, :] = y


def _ffn_layer(h, w_in, w_out, layer, g, b, alpha, mixer_projection=None,
               time_ordered_output=False):
    m, d = h.shape
    hidden = w_out.shape[1]
    row_spec = lambda width: pl.BlockSpec((FFN_ROW_TILE, width), lambda i: (i, 0))
    operands = [h, w_in, w_out, g, b]
    in_specs = [row_spec(d), _resident_layer(w_in.shape, layer),
                _resident_layer(w_out.shape, layer), _resident((1, d)), _resident((1, d))]
    if mixer_projection is not None:
        o, w_o, o_layer, g0, b0 = mixer_projection
        operands = [o, w_o, g0, b0] + operands
        in_specs = [row_spec(o.shape[1]), _resident_layer(w_o.shape, o_layer),
                    _resident((1, d)), _resident((1, d))] + in_specs
    return pl.pallas_call(
        functools.partial(_ffn_kernel, hidden=hidden, alpha=alpha,
                          mixer_projection=mixer_projection is not None,
                          time_ordered_output=time_ordered_output),
        grid=(m // FFN_ROW_TILE,),
        in_specs=in_specs,
        out_specs=row_spec(d),
        out_shape=jax.ShapeDtypeStruct((m, d), _F32),
        scratch_shapes=[_reorder_slab(FFN_SUB_ROWS, d)] if time_ordered_output else [],
        compiler_params=_compiler_params(("arbitrary",)),
        name="swiglu_ffn_ln",
    )(*operands)


def _gelu_tanh(x):
    inner = x * (0.7978845608028654 + (0.7978845608028654 * 0.044715) * (x * x))
    half_x = 0.5 * x
    return half_x * jnp.tanh(inner) + half_x


def _softplus(z):
    return jnp.maximum(z, 0.0) + jnp.log1p(jnp.exp(-jnp.abs(z)))


def _lru_kernel(x_ref, x_next_ref, w_in_ref, conv_w_ref, conv_b_ref, w_gate_ref, b_a_ref,
                b_x_ref, lam_ref, w_out_ref, g_ref, b_ref, out_ref,
                xbuf, gate_pre, tail, a_s, b_s, h_s, ac_s, carry, *, alpha, steps_per_seq):
    sub = V7X_SUBLANES
    seq_tile, width = a_s.shape
    n_tiles = x_ref.shape[0] // seq_tile
    steps = seq_tile // sub
    halo = (CONV_WIDTH - 1) * sub
    group = width // LRU_BLOCKS
    step = pl.program_id(0)
    cur, nxt = step % 2, (step + 1) % 2

    def in_projection(src_ref, slot, t):
        xb = src_ref[t * seq_tile:(t + 1) * seq_tile, :].astype(_BF16)
        xbuf[slot, t, halo:halo + seq_tile, :] = _dot(xb, w_in_ref[:, 0:width])
        gate_pre[slot, t] = _dot(xb, w_in_ref[:, width:2 * width])

    @pl.when(step == 0)
    def _():
        for t in range(n_tiles):
            in_projection(x_ref, 0, t)

    @pl.when(step % steps_per_seq == 0)
    def _():
        tail[...] = jnp.zeros_like(tail)
        carry[...] = jnp.zeros_like(carry)

    half_rate = (-0.5 * LRU_C * LOG2_E) * _softplus(-lam_ref[...])

    def conv(t):
        first_sublane = lax.broadcasted_iota(jnp.int32, (sub, width), 0) == 0
        for g in range(CONV_WIDTH - 1):
            rows = slice(g * sub, (g + 1) * sub)
            last = xbuf[cur, t, seq_tile + g * sub:seq_tile + (g + 1) * sub, :]
            xbuf[cur, t, rows, :] = jnp.where(first_sublane,
                                              pltpu.roll(tail[rows, :], 1, axis=0),
                                              pltpu.roll(last, 1, axis=0))
            tail[rows, :] = last
        xc = conv_b_ref[...]
        for tap in range(CONV_WIDTH):
            start = halo - (CONV_WIDTH - 1 - tap) * sub
            xc = xc + xbuf[cur, t, start:start + seq_tile, :] * conv_w_ref[tap:tap + 1, :]
        return xc

    def gate_projection(xc):
        xcb = xc.astype(_BF16)
        pre_a, pre_x = [], []
        for gidx in range(LRU_BLOCKS):
            both = _dot(xcb[:, gidx * group:(gidx + 1) * group], w_gate_ref[gidx])
            pre_a.append(both[:, 0:group])
            pre_x.append(both[:, group:2 * group])
        return jnp.concatenate(pre_a, axis=-1), jnp.concatenate(pre_x, axis=-1)

    def recurrence(xc, pre_a, pre_x):
        tanh_a = jnp.tanh(0.5 * (pre_a + b_a_ref[...]))
        tanh_x = jnp.tanh(0.5 * (pre_x + b_x_ref[...]))
        a = jnp.exp2(tanh_a * half_rate + half_rate)
        half_xc = 0.5 * xc
        gated_in = tanh_x * half_xc + half_xc
        y = 1.0 - a * a
        a_s[...] = a
        b_s[...] = jnp.where(y > 0.0, y * lax.rsqrt(y), 0.0) * gated_in

        h_loc = jnp.zeros((sub, width), _F32)
        a_cum = jnp.ones((sub, width), _F32)
        for j in range(steps):
            rows = slice(j * sub, (j + 1) * sub)
            a_j = a_s[rows, :]
            h_loc = a_j * h_loc + b_s[rows, :]
            a_cum = a_j * a_cum
            h_s[rows, :] = h_loc
            ac_s[rows, :] = a_cum

        state = carry[...]
        starts = []
        for c in range(sub):
            starts.append(state)
            state = h_loc[c:c + 1, :] + a_cum[c:c + 1, :] * state
        carry[...] = state
        start = jnp.tile(jnp.concatenate(starts, axis=0), (steps, 1))
        return h_s[...] + ac_s[...] * start

    def out_projection(t, h):
        rows = slice(t * seq_tile, (t + 1) * seq_tile)
        gated = (h * _gelu_tanh(gate_pre[cur, t])).astype(_BF16)
        y = _dot(gated, w_out_ref[...]) + alpha * x_ref[rows, :]
        out_ref[rows, :] = _layer_norm(y, g_ref[...], b_ref[...])

    convolved = [conv(t) for t in range(n_tiles)]
    gates = [gate_projection(convolved[0])]
    for t in range(n_tiles):
        in_projection(x_next_ref, nxt, t)
        if t + 1 < n_tiles:
            gates.append(gate_projection(convolved[t + 1]))
        out_projection(t, recurrence(convolved[t], *gates[t]))


def _lru_layer(h, batch, seq, layer, w_in, conv_w, conv_b, w_gate, b_a, b_x, lam, w_out, g, b,
               alpha):
    m, d = h.shape
    width = w_out.shape[1]
    rows = LRU_TILES * STREAM_TILE
    steps_per_seq = seq // rows
    n_steps = m // rows
    halo = (CONV_WIDTH - 1) * V7X_SUBLANES
    tile_f32 = pltpu.VMEM((STREAM_TILE, width), _F32)
    row_spec = pl.BlockSpec((rows, d), lambda i: (i, 0))
    next_spec = pl.BlockSpec((rows, d), lambda i: (jnp.minimum(i + 1, n_steps - 1), 0))
    return pl.pallas_call(
        functools.partial(_lru_kernel, alpha=alpha, steps_per_seq=steps_per_seq),
        grid=(n_steps,),
        in_specs=[row_spec, next_spec, _resident_layer(w_in.shape, layer),
                  _resident(conv_w.shape),
                  _resident(conv_b.shape), _resident_layer(w_gate.shape, layer),
                  _resident(b_a.shape), _resident(b_x.shape), _resident(lam.shape),
                  _resident_layer(w_out.shape, layer), _resident(g.shape), _resident(b.shape)],
        out_specs=row_spec,
        out_shape=jax.ShapeDtypeStruct((m, d), _F32),
        scratch_shapes=[pltpu.VMEM((2, LRU_TILES, halo + STREAM_TILE, width), _F32),
                        pltpu.VMEM((2, LRU_TILES, STREAM_TILE, width), _F32),
                        pltpu.VMEM((halo, width), _F32),
                        tile_f32,
                        tile_f32,
                        tile_f32,
                        tile_f32,
                        pltpu.VMEM((1, width), _F32)],
        compiler_params=_compiler_params(("arbitrary",)),
        name="rglru_block_ln",
    )(h, h, w_in, conv_w, conv_b, w_gate, b_a, b_x, lam, w_out, g, b)


def kernel(x, attn_w_qkv, attn_w_o, lru_w_in, lru_conv_w, lru_conv_b, lru_w_a, lru_b_a,
           lru_w_x, lru_b_x, lru_lambda, lru_w_out, ffn_w_in, ffn_w_out, ln_g, ln_b):
    batch, seq, d = x.shape
    depth = ffn_w_in.shape[0]
    rows = batch * seq
    assert d % N_HEADS == 0 and N_HEADS % MOBA_HEADS_PER_STEP == 0
    assert seq % ROW_TILE == 0 and ROW_TILE % MOBA_BLOCK == 0 and d % QKV_COL_CHUNK == 0
    assert seq % (LRU_TILES * STREAM_TILE) == 0
    assert rows % FFN_ROW_TILE == 0 and FFN_ROW_TILE % FFN_SUB_ROWS == 0
    assert FFN_SUB_ROWS % STREAM_TILE == 0 and ffn_w_out.shape[1] % FFN_CHUNK == 0
    alpha = (2 * depth) ** 0.25
    head_dim = d // N_HEADS
    n_blk = seq // MOBA_BLOCK
    slopes = jnp.exp2(-8.0 * (jnp.arange(N_HEADS, dtype=_F32) + 1.0) / N_HEADS)
    row = lambda v: v.reshape(1, -1)

    w_qkv, w_o = attn_w_qkv.astype(_BF16), attn_w_o.astype(_BF16)
    w_lru_in, w_lru_out = lru_w_in.astype(_BF16), lru_w_out.astype(_BF16)
    w_gate = jnp.concatenate([lru_w_a, lru_w_x], axis=-1).astype(_BF16)
    w_ffn_in, w_ffn_out = ffn_w_in.astype(_BF16), ffn_w_out.astype(_BF16)

    h = x.reshape(rows, d)
    for layer in range(depth):
        j = layer // 2
        g0, b0 = row(ln_g[layer, 0]), row(ln_b[layer, 0])
        g1, b1 = row(ln_g[layer, 1]), row(ln_b[layer, 1])
        if layer % 2 == 0:
            qk, vt, kmean, *stream = _qkv_projection(h, w_qkv, j, head_dim ** -0.5 * LOG2_E,
                                                     batch, seq, time_ordered_input=layer == 0)
            if stream:
                h, = stream
            o = _moba_attention(qk, vt, kmean.reshape(batch, n_blk, d), slopes, batch, seq)
            mixer_projection = (o, w_o, j, g0, b0)
        else:
            h = _lru_layer(h, batch, seq, j, w_lru_in, lru_conv_w[j], row(lru_conv_b[j]),
                           w_gate, row(lru_b_a[j]), row(lru_b_x[j]), row(lru_lambda[j]),
                           w_lru_out, g0, b0, alpha)
            mixer_projection = None
        h = _ffn_layer(h, w_ffn_in, w_ffn_out, layer, g1, b1, alpha, mixer_projection,
                       time_ordered_output=layer == depth - 1)
    return h.reshape(batch, seq, d)
```

```python
import functools

import jax
import jax.numpy as jnp
from jax import lax
from jax.experimental import pallas as pl
from jax.experimental.pallas import tpu as pltpu

N_HEADS = 8
MOBA_BLOCK = 256
MOBA_TOPK = 3
NEG_INF = -1e30
LRU_BLOCKS = 8
CONV_WIDTH = 4
LRU_C = 8.0
LN_EPS = 1e-5
LOG2_E = 1.4426950408889634

V7X_SUBLANES = 8
V7X_LANES = 128
V7X_VMEM_LIMIT_BYTES = 56 * 1024 * 1024

ROW_TILE = 1024
QKV_COL_CHUNK = 512
STREAM_TILE = MOBA_BLOCK
FFN_CHUNK = 256
FFN_ROW_TILE = 1024
FFN_SUB_ROWS = 512
LRU_TILES = 2
MOBA_HEADS_PER_STEP = 4

_F32 = jnp.float32
_BF16 = jnp.bfloat16
_NT = (((1,), (1,)), ((), ()))


def _compiler_params(semantics):
    return pltpu.CompilerParams(dimension_semantics=semantics,
                                vmem_limit_bytes=V7X_VMEM_LIMIT_BYTES)


def _resident(shape):
    zeros = (0,) * len(shape)
    return pl.BlockSpec(shape, lambda *_: zeros, pipeline_mode=pl.Buffered(1))


def _resident_layer(stacked_shape, layer):
    index = (layer,) + (0,) * (len(stacked_shape) - 1)
    return pl.BlockSpec((None,) + tuple(stacked_shape[1:]), lambda *_: index,
                        pipeline_mode=pl.Buffered(1))


def _stream_rows(natural_group):
    steps = STREAM_TILE // V7X_SUBLANES
    chunk, first_step = divmod(natural_group * V7X_SUBLANES, steps)
    return pl.ds(first_step * V7X_SUBLANES + chunk, V7X_SUBLANES, stride=V7X_SUBLANES)


def _natural_to_stream(src_ref, slab):
    tiles, lane_groups = slab.shape[0], slab.shape[1]
    for t in range(tiles):
        for g in range(lane_groups):
            lanes = slice(g * V7X_LANES, (g + 1) * V7X_LANES)
            for group in range(STREAM_TILE // V7X_SUBLANES):
                row0 = t * STREAM_TILE + group * V7X_SUBLANES
                slab[t, g, _stream_rows(group), :] = src_ref[row0:row0 + V7X_SUBLANES, lanes]
    return jnp.concatenate(
        [jnp.concatenate([slab[t, g] for g in range(lane_groups)], axis=-1)
         for t in range(tiles)], axis=0)


def _stream_to_natural(y, slab, dst_ref, dst_row0):
    lane_groups = slab.shape[1]
    for t in range(y.shape[0] // STREAM_TILE):
        for g in range(lane_groups):
            lanes = slice(g * V7X_LANES, (g + 1) * V7X_LANES)
            slab[t, g] = y[t * STREAM_TILE:(t + 1) * STREAM_TILE, lanes]
            for group in range(STREAM_TILE // V7X_SUBLANES):
                row0 = dst_row0 + t * STREAM_TILE + group * V7X_SUBLANES
                dst_ref[row0:row0 + V7X_SUBLANES, lanes] = slab[t, g, _stream_rows(group), :]


def _reorder_slab(rows, d):
    return pltpu.VMEM((rows // STREAM_TILE, d // V7X_LANES, STREAM_TILE, V7X_LANES), _F32)


def _tile_time(idx):
    steps = STREAM_TILE // V7X_SUBLANES
    shift = V7X_SUBLANES.bit_length() - 1
    return (idx & (V7X_SUBLANES - 1)) * steps + (idx >> shift)


def _layer_norm(y, g, b):
    mu = jnp.mean(y, axis=-1, keepdims=True)
    yc = y - mu
    var = jnp.mean(yc * yc, axis=-1, keepdims=True)
    return yc * lax.rsqrt(var + LN_EPS) * g + b


def _dot(a, b):
    return jnp.dot(a, b, preferred_element_type=_F32)


def _qkv_kernel(x_ref, w_ref, qk_ref, vt_ref, kmean_ref, *stream_out, d_model, scale):
    if stream_out:
        stream_ref, slab = stream_out
        x = _natural_to_stream(x_ref, slab)
        stream_ref[...] = x
    else:
        x = x_ref[...]
    xb = x.astype(_BF16)
    rows = xb.shape[0]
    chunk = QKV_COL_CHUNK
    for c in range(0, 3 * d_model, chunk):
        acc = _dot(xb, w_ref[:, c:c + chunk])
        if c >= 2 * d_model:
            vt_ref[c - 2 * d_model:c - 2 * d_model + chunk, :] = acc.T.astype(_BF16)
            continue
        if c < d_model:
            acc = acc * scale
        else:
            for r in range(rows // MOBA_BLOCK):
                blk = acc[r * MOBA_BLOCK:(r + 1) * MOBA_BLOCK, :]
                kmean_ref[0, r:r + 1, c - d_model:c - d_model + chunk] = (
                    jnp.sum(blk, axis=0, keepdims=True) * (1.0 / MOBA_BLOCK))
        qk_ref[:, c:c + chunk] = acc.astype(_BF16)


def _qkv_projection(h, w_qkv, layer, scale, batch, seq, time_ordered_input=False):
    m, d = h.shape
    blocks_per_tile = ROW_TILE // MOBA_BLOCK
    tiles_per_seq = seq // ROW_TILE
    row_spec = pl.BlockSpec((ROW_TILE, d), lambda i: (i, 0))
    out_specs = [pl.BlockSpec((ROW_TILE, 2 * d), lambda i: (i, 0)),
                 pl.BlockSpec((None, d, ROW_TILE),
                              lambda i: (i // tiles_per_seq, 0, i % tiles_per_seq)),
                 pl.BlockSpec((1, blocks_per_tile, d), lambda i: (i, 0, 0))]
    out_shape = [jax.ShapeDtypeStruct((m, 2 * d), _BF16),
                 jax.ShapeDtypeStruct((batch, d, seq), _BF16),
                 jax.ShapeDtypeStruct((m // ROW_TILE, blocks_per_tile, d), _F32)]
    scratch = []
    if time_ordered_input:
        out_specs.append(row_spec)
        out_shape.append(jax.ShapeDtypeStruct((m, d), _F32))
        scratch.append(_reorder_slab(ROW_TILE, d))
    return pl.pallas_call(
        functools.partial(_qkv_kernel, d_model=d, scale=scale),
        grid=(m // ROW_TILE,),
        in_specs=[row_spec, _resident_layer(w_qkv.shape, layer)],
        out_specs=out_specs,
        out_shape=out_shape,
        scratch_shapes=scratch,
        compiler_params=_compiler_params(("arbitrary",)),
        name="qkv_projection",
    )(h, w_qkv)


def _split_bf16(x, parts):
    terms = []
    for _ in range(parts):
        t = x.astype(_BF16)
        terms.append(t)
        x = x - t.astype(_F32)
    return terms


def _moba_kernel(slopes_ref, q_ref, k_ref, vt_ref, kmean_ref, o_ref,
                 k_aug, vt_aug, causal, s_scr, p_scr, *, n_blk):
    blk = MOBA_BLOCK
    n_heads, seq, dh2 = k_aug.shape
    dh = dh2 // 2
    bias_lanes = 3
    head_cols = lambda hd: slice(hd * dh, (hd + 1) * dh)

    @pl.when((pl.program_id(0) == 0) & (pl.program_id(1) == 0))
    def _():
        key_time = _tile_time(lax.broadcasted_iota(jnp.int32, (seq, dh), 0) & (blk - 1))
        key_lane = lax.broadcasted_iota(jnp.int32, (seq, dh), 1)
        extra = vt_aug.shape[1] - dh
        ones_row = lax.broadcasted_iota(jnp.int32, (extra, seq), 0) == 0
        for hd in range(n_heads):
            k_aug[hd, :, dh:2 * dh] = jnp.where(key_lane < bias_lanes, key_time, 0).astype(_BF16)
            vt_aug[hd, dh:dh + extra, :] = jnp.where(ones_row, 1.0, 0.0).astype(_BF16)
        key_t = _tile_time(lax.broadcasted_iota(jnp.int32, (blk, blk), 0))
        query_t = _tile_time(lax.broadcasted_iota(jnp.int32, (blk, blk), 1))
        causal[...] = jnp.where(key_t > query_t, NEG_INF, 0.0)

    lane = lax.broadcasted_iota(jnp.int32, (blk, dh), 1)
    block_id = lax.broadcasted_iota(jnp.int32, (n_blk, blk), 0)
    slope2, q_bias, km_split = [], [], []
    for hd in range(n_heads):
        k_aug[hd, :, 0:dh] = k_ref[:, head_cols(hd)]
        vt_aug[hd, 0:dh, :] = vt_ref[head_cols(hd), :]
        slope2.append(slopes_ref[pl.program_id(1) * n_heads + hd] * LOG2_E)
        bias = jnp.zeros((blk, dh), _F32)
        for i, term in enumerate(_split_bf16(jnp.full((blk, dh), slope2[hd], _F32), bias_lanes)):
            bias = jnp.where(lane == i, term.astype(_F32), bias)
        q_bias.append(bias.astype(_BF16))
        km_split.append(_split_bf16(kmean_ref[:, head_cols(hd)], 2))

    def score_stage(hd, qi):
        width = (qi + 1) * blk
        q = q_ref[qi * blk:(qi + 1) * blk, head_cols(hd)]
        s_scr[hd, qi % 2, 0:width, :] = lax.dot_general(
            k_aug[hd, 0:width, :], jnp.concatenate([q, q_bias[hd]], axis=-1), _NT,
            preferred_element_type=_F32)
        if qi <= MOBA_TOPK:
            return None
        km_hi, km_lo = km_split[hd]
        gate = (lax.dot_general(km_hi, q, _NT, preferred_element_type=_F32)
                + lax.dot_general(km_lo, q, _NT, preferred_element_type=_F32))
        ahead = jnp.zeros((n_blk, blk), _F32)
        for m in range(qi):
            g_m = gate[m:m + 1, :]
            wins = (g_m > gate) | ((g_m == gate) & (block_id > m))
            ahead = ahead + jnp.where(wins, 1.0, 0.0)
        return (ahead < float(MOBA_TOPK)) & (block_id < qi)

    def softmax_stage(hd, qi, picked):
        s_buf, p_buf = s_scr.at[hd, qi % 2], p_scr.at[hd, qi % 2]

        def block_scores(n):
            t = s_buf[n * blk:(n + 1) * blk, :]
            return t + causal[...] if n == qi else t

        query_term = [slope2[hd] * float(-blk * (qi - n)) for n in range(qi)]
        if picked is not None:
            query_term = [jnp.where(picked[n:n + 1, :], query_term[n], NEG_INF)
                          for n in range(qi)]
        m_run = jnp.max(block_scores(qi), axis=0, keepdims=True)
        for n in range(qi):
            m_run = jnp.maximum(m_run,
                                jnp.max(block_scores(n), axis=0, keepdims=True) + query_term[n])
        for n in range(qi + 1):
            shift = m_run if n == qi else m_run - query_term[n]
            p_buf[n * blk:(n + 1) * blk, :] = jnp.exp2(block_scores(n) - shift).astype(_BF16)

    def output_stage(hd, qi):
        width = (qi + 1) * blk
        out = _dot(vt_aug[hd, :, 0:width], p_scr[hd, qi % 2, 0:width, :])
        o_ref[qi * blk:(qi + 1) * blk, head_cols(hd)] = (
            out[0:dh, :] / out[dh:dh + 1, :]).T.astype(o_ref.dtype)

    heads = range(n_heads)
    picked = [score_stage(hd, 0) for hd in heads]
    for qi in range(n_blk):
        picked_next = []
        for hd in heads:
            picked_next.append(score_stage(hd, qi + 1) if qi + 1 < n_blk else None)
            softmax_stage(hd, qi, picked[hd])
            if hd > 0:
                output_stage(hd - 1, qi)
        output_stage(n_heads - 1, qi)
        picked = picked_next


def _moba_attention(qk, vt, kmean, slopes, batch, seq):
    m, d2 = qk.shape
    d = d2 // 2
    dh = d // N_HEADS
    n_blk = seq // MOBA_BLOCK
    bf16_rows = 2 * V7X_SUBLANES
    hps = MOBA_HEADS_PER_STEP
    groups = N_HEADS // hps
    wide = hps * dh
    return pl.pallas_call(
        functools.partial(_moba_kernel, n_blk=n_blk),
        grid=(batch, groups),
        in_specs=[pl.BlockSpec(memory_space=pltpu.SMEM),
                  pl.BlockSpec((seq, wide), lambda b, g: (b, g)),
                  pl.BlockSpec((seq, wide), lambda b, g: (b, groups + g)),
                  pl.BlockSpec((None, wide, seq), lambda b, g: (b, g, 0)),
                  pl.BlockSpec((None, n_blk, wide), lambda b, g: (b, 0, g))],
        out_specs=pl.BlockSpec((seq, wide), lambda b, g: (b, g)),
        out_shape=jax.ShapeDtypeStruct((m, d), _BF16),
        scratch_shapes=[pltpu.VMEM((hps, seq, 2 * dh), _BF16),
                        pltpu.VMEM((hps, dh + bf16_rows, seq), _BF16),
                        pltpu.VMEM((MOBA_BLOCK, MOBA_BLOCK), _F32),
                        pltpu.VMEM((hps, 2, seq, MOBA_BLOCK), _F32),
                        pltpu.VMEM((hps, 2, seq, MOBA_BLOCK), _BF16)],
        compiler_params=_compiler_params(("arbitrary", "arbitrary")),
        name="moba_attention",
    )(slopes, qk, qk, vt, kmean)


def _ffn_kernel(*refs, hidden, alpha, mixer_projection, time_ordered_output):
    if mixer_projection:
        o_ref, w_o_ref, g0_ref, b0_ref, *refs = refs
    x_ref, w_in_ref, w_out_ref, g_ref, b_ref, out_ref, *slab = refs
    groups = range(0, x_ref.shape[0], FFN_SUB_ROWS)
    inputs = [x_ref[r:r + FFN_SUB_ROWS, :] for r in groups]
    if mixer_projection:
        inputs = [_layer_norm(_dot(o_ref[r:r + FFN_SUB_ROWS, :], w_o_ref[...]) + alpha * x,
                              g0_ref[...], b0_ref[...]) for r, x in zip(groups, inputs)]
    for r, x in zip(groups, inputs):
        xb = x.astype(_BF16)
        acc = alpha * x
        for c in range(0, hidden, FFN_CHUNK):
            gate = _dot(xb, w_in_ref[:, c:c + FFN_CHUNK])
            up = _dot(xb, w_in_ref[:, hidden + c:hidden + c + FFN_CHUNK])
            act = (gate * jax.nn.sigmoid(gate) * up).astype(_BF16)
            acc = acc + _dot(act, w_out_ref[c:c + FFN_CHUNK, :])
        y = _layer_norm(acc, g_ref[...], b_ref[...])
        if time_ordered_output:
            _stream_to_natural(y, slab[0], out_ref, r)
        else:
            out_ref[r:r + FFN_SUB_ROWS, :] = y


def _ffn_layer(h, w_in, w_out, layer, g, b, alpha, mixer_projection=None,
               time_ordered_output=False):
    m, d = h.shape
    hidden = w_out.shape[1]
    row_spec = lambda width: pl.BlockSpec((FFN_ROW_TILE, width), lambda i: (i, 0))
    operands = [h, w_in, w_out, g, b]
    in_specs = [row_spec(d), _resident_layer(w_in.shape, layer),
                _resident_layer(w_out.shape, layer), _resident((1, d)), _resident((1, d))]
    if mixer_projection is not None:
        o, w_o, o_layer, g0, b0 = mixer_projection
        operands = [o, w_o, g0, b0] + operands
        in_specs = [row_spec(o.shape[1]), _resident_layer(w_o.shape, o_layer),
                    _resident((1, d)), _resident((1, d))] + in_specs
    return pl.pallas_call(
        functools.partial(_ffn_kernel, hidden=hidden, alpha=alpha,
                          mixer_projection=mixer_projection is not None,
                          time_ordered_output=time_ordered_output),
        grid=(m // FFN_ROW_TILE,),
        in_specs=in_specs,
        out_specs=row_spec(d),
        out_shape=jax.ShapeDtypeStruct((m, d), _F32),
        scratch_shapes=[_reorder_slab(FFN_SUB_ROWS, d)] if time_ordered_output else [],
        compiler_params=_compiler_params(("arbitrary",)),
        name="swiglu_ffn_ln",
    )(*operands)


def _gelu_tanh(x):
    inner = x * (0.7978845608028654 + (0.7978845608028654 * 0.044715) * (x * x))
    half_x = 0.5 * x
    return half_x * jnp.tanh(inner) + half_x


def _softplus(z):
    return jnp.maximum(z, 0.0) + jnp.log1p(jnp.exp(-jnp.abs(z)))


def _lru_kernel(x_ref, x_next_ref, w_in_ref, conv_w_ref, conv_b_ref, w_gate_ref, b_a_ref,
                b_x_ref, lam_ref, w_out_ref, g_ref, b_ref, out_ref,
                xbuf, gate_pre, tail, a_s, b_s, h_s, ac_s, carry, *, alpha, steps_per_seq):
    sub = V7X_SUBLANES
    seq_tile, width = a_s.shape
    n_tiles = x_ref.shape[0] // seq_tile
    steps = seq_tile // sub
    halo = (CONV_WIDTH - 1) * sub
    group = width // LRU_BLOCKS
    step = pl.program_id(0)
    cur, nxt = step % 2, (step + 1) % 2

    def in_projection(src_ref, slot, t):
        xb = src_ref[t * seq_tile:(t + 1) * seq_tile, :].astype(_BF16)
        xbuf[slot, t, halo:halo + seq_tile, :] = _dot(xb, w_in_ref[:, 0:width])
        gate_pre[slot, t] = _dot(xb, w_in_ref[:, width:2 * width])

    @pl.when(step == 0)
    def _():
        for t in range(n_tiles):
            in_projection(x_ref, 0, t)

    @pl.when(step % steps_per_seq == 0)
    def _():
        tail[...] = jnp.zeros_like(tail)
        carry[...] = jnp.zeros_like(carry)

    half_rate = (-0.5 * LRU_C * LOG2_E) * _softplus(-lam_ref[...])

    def conv(t):
        first_sublane = lax.broadcasted_iota(jnp.int32, (sub, width), 0) == 0
        for g in range(CONV_WIDTH - 1):
            rows = slice(g * sub, (g + 1) * sub)
            last = xbuf[cur, t, seq_tile + g * sub:seq_tile + (g + 1) * sub, :]
            xbuf[cur, t, rows, :] = jnp.where(first_sublane,
                                              pltpu.roll(tail[rows, :], 1, axis=0),
                                              pltpu.roll(last, 1, axis=0))
            tail[rows, :] = last
        xc = conv_b_ref[...]
        for tap in range(CONV_WIDTH):
            start = halo - (CONV_WIDTH - 1 - tap) * sub
            xc = xc + xbuf[cur, t, start:start + seq_tile, :] * conv_w_ref[tap:tap + 1, :]
        return xc

    def gate_projection(xc):
        xcb = xc.astype(_BF16)
        pre_a, pre_x = [], []
        for gidx in range(LRU_BLOCKS):
            both = _dot(xcb[:, gidx * group:(gidx + 1) * group], w_gate_ref[gidx])
            pre_a.append(both[:, 0:group])
            pre_x.append(both[:, group:2 * group])
        return jnp.concatenate(pre_a, axis=-1), jnp.concatenate(pre_x, axis=-1)

    def recurrence(xc, pre_a, pre_x):
        tanh_a = jnp.tanh(0.5 * (pre_a + b_a_ref[...]))
        tanh_x = jnp.tanh(0.5 * (pre_x + b_x_ref[...]))
        a = jnp.exp2(tanh_a * half_rate + half_rate)
        half_xc = 0.5 * xc
        gated_in = tanh_x * half_xc + half_xc
        y = 1.0 - a * a
        a_s[...] = a
        b_s[...] = jnp.where(y > 0.0, y * lax.rsqrt(y), 0.0) * gated_in

        h_loc = jnp.zeros((sub, width), _F32)
        a_cum = jnp.ones((sub, width), _F32)
        for j in range(steps):
            rows = slice(j * sub, (j + 1) * sub)
            a_j = a_s[rows, :]
            h_loc = a_j * h_loc + b_s[rows, :]
            a_cum = a_j * a_cum
            h_s[rows, :] = h_loc
            ac_s[rows, :] = a_cum

        state = carry[...]
        starts = []
        for c in range(sub):
            starts.append(state)
            state = h_loc[c:c + 1, :] + a_cum[c:c + 1, :] * state
        carry[...] = state
        start = jnp.tile(jnp.concatenate(starts, axis=0), (steps, 1))
        return h_s[...] + ac_s[...] * start

    def out_projection(t, h):
        rows = slice(t * seq_tile, (t + 1) * seq_tile)
        gated = (h * _gelu_tanh(gate_pre[cur, t])).astype(_BF16)
        y = _dot(gated, w_out_ref[...]) + alpha * x_ref[rows, :]
        out_ref[rows, :] = _layer_norm(y, g_ref[...], b_ref[...])

    for t in range(n_tiles):
        xc = conv(t)
        pre_a, pre_x = gate_projection(xc)
        in_projection(x_next_ref, nxt, t)
        out_projection(t, recurrence(xc, pre_a, pre_x))


def _lru_layer(h, batch, seq, layer, w_in, conv_w, conv_b, w_gate, b_a, b_x, lam, w_out, g, b,
               alpha):
    m, d = h.shape
    width = w_out.shape[1]
    rows = LRU_TILES * STREAM_TILE
    steps_per_seq = seq // rows
    n_steps = m // rows
    halo = (CONV_WIDTH - 1) * V7X_SUBLANES
    tile_f32 = pltpu.VMEM((STREAM_TILE, width), _F32)
    row_spec = pl.BlockSpec((rows, d), lambda i: (i, 0))
    next_spec = pl.BlockSpec((rows, d), lambda i: (jnp.minimum(i + 1, n_steps - 1), 0))
    return pl.pallas_call(
        functools.partial(_lru_kernel, alpha=alpha, steps_per_seq=steps_per_seq),
        grid=(n_steps,),
        in_specs=[row_spec, next_spec, _resident_layer(w_in.shape, layer),
                  _resident(conv_w.shape),
                  _resident(conv_b.shape), _resident_layer(w_gate.shape, layer),
                  _resident(b_a.shape), _resident(b_x.shape), _resident(lam.shape),
                  _resident_layer(w_out.shape, layer), _resident(g.shape), _resident(b.shape)],
        out_specs=row_spec,
        out_shape=jax.ShapeDtypeStruct((m, d), _F32),
        scratch_shapes=[pltpu.VMEM((2, LRU_TILES, halo + STREAM_TILE, width), _F32),
                        pltpu.VMEM((2, LRU_TILES, STREAM_TILE, width), _F32),
                        pltpu.VMEM((halo, width), _F32),
                        tile_f32,
                        tile_f32,
                        tile_f32,
                        tile_f32,
                        pltpu.VMEM((1, width), _F32)],
        compiler_params=_compiler_params(("arbitrary",)),
        name="rglru_block_ln",
    )(h, h, w_in, conv_w, conv_b, w_gate, b_a, b_x, lam, w_out, g, b)


def kernel(x, attn_w_qkv, attn_w_o, lru_w_in, lru_conv_w, lru_conv_b, lru_w_a, lru_b_a,
           lru_w_x, lru_b_x, lru_lambda, lru_w_out, ffn_w_in, ffn_w_out, ln_g, ln_b):
    batch, seq, d = x.shape
    depth = ffn_w_in.shape[0]
    rows = batch * seq
    assert d % N_HEADS == 0 and N_HEADS % MOBA_HEADS_PER_STEP == 0
    assert seq % ROW_TILE == 0 and ROW_TILE % MOBA_BLOCK == 0 and d % QKV_COL_CHUNK == 0
    assert seq % (LRU_TILES * STREAM_TILE) == 0
    assert rows % FFN_ROW_TILE == 0 and FFN_ROW_TILE % FFN_SUB_ROWS == 0
    assert FFN_SUB_ROWS % STREAM_TILE == 0 and ffn_w_out.shape[1] % FFN_CHUNK == 0
    alpha = (2 * depth) ** 0.25
    head_dim = d // N_HEADS
    n_blk = seq // MOBA_BLOCK
    slopes = jnp.exp2(-8.0 * (jnp.arange(N_HEADS, dtype=_F32) + 1.0) / N_HEADS)
    row = lambda v: v.reshape(1, -1)

    w_qkv, w_o = attn_w_qkv.astype(_BF16), attn_w_o.astype(_BF16)
    w_lru_in, w_lru_out = lru_w_in.astype(_BF16), lru_w_out.astype(_BF16)
    w_gate = jnp.concatenate([lru_w_a, lru_w_x], axis=-1).astype(_BF16)
    w_ffn_in, w_ffn_out = ffn_w_in.astype(_BF16), ffn_w_out.astype(_BF16)

    h = x.reshape(rows, d)
    for layer in range(depth):
        j = layer // 2
        g0, b0 = row(ln_g[layer, 0]), row(ln_b[layer, 0])
        g1, b1 = row(ln_g[layer, 1]), row(ln_b[layer, 1])
        if layer % 2 == 0:
            qk, vt, kmean, *stream = _qkv_projection(h, w_qkv, j, head_dim ** -0.5 * LOG2_E,
                                                     batch, seq, time_ordered_input=layer == 0)
            if stream:
                h, = stream
            o = _moba_attention(qk, vt, kmean.reshape(batch, n_blk, d), slopes, batch, seq)
            mixer_projection = (o, w_o, j, g0, b0)
        else:
            h = _lru_layer(h, batch, seq, j, w_lru_in, lru_conv_w[j], row(lru_conv_b[j]),
                           w_gate, row(lru_b_a[j]), row(lru_b_x[j]), row(lru_lambda[j]),
                           w_lru_out, g0, b0, alpha)
            mixer_projection = None
        h = _ffn_layer(h, w_ffn_in, w_ffn_out, layer, g1, b1, alpha, mixer_projection,
                       time_ordered_output=layer == depth - 1)
    return h.reshape(batch, seq, d)
```

```python
import functools

import jax
import jax.numpy as jnp
from jax import lax
from jax.experimental import pallas as pl
from jax.experimental.pallas import tpu as pltpu

N_HEADS = 8
MOBA_BLOCK = 256
MOBA_TOPK = 3
NEG_INF = -1e30
LRU_BLOCKS = 8
CONV_WIDTH = 4
LRU_C = 8.0
LN_EPS = 1e-5
LOG2_E = 1.4426950408889634

V7X_SUBLANES = 8
V7X_LANES = 128
V7X_VMEM_LIMIT_BYTES = 56 * 1024 * 1024

ROW_TILE = 1024
QKV_COL_CHUNK = 512
STREAM_TILE = MOBA_BLOCK
FFN_CHUNK = 256
FFN_ROW_TILE = 1024
FFN_SUB_ROWS = 512
LRU_TILES = 4
MOBA_HEADS_PER_STEP = 4

_F32 = jnp.float32
_BF16 = jnp.bfloat16
_NT = (((1,), (1,)), ((), ()))


def _compiler_params(semantics):
    return pltpu.CompilerParams(dimension_semantics=semantics,
                                vmem_limit_bytes=V7X_VMEM_LIMIT_BYTES)


def _resident(shape):
    zeros = (0,) * len(shape)
    return pl.BlockSpec(shape, lambda *_: zeros, pipeline_mode=pl.Buffered(1))


def _resident_layer(stacked_shape, layer):
    index = (layer,) + (0,) * (len(stacked_shape) - 1)
    return pl.BlockSpec((None,) + tuple(stacked_shape[1:]), lambda *_: index,
                        pipeline_mode=pl.Buffered(1))


def _stream_rows(natural_group):
    steps = STREAM_TILE // V7X_SUBLANES
    chunk, first_step = divmod(natural_group * V7X_SUBLANES, steps)
    return pl.ds(first_step * V7X_SUBLANES + chunk, V7X_SUBLANES, stride=V7X_SUBLANES)


def _natural_to_stream(src_ref, slab):
    tiles, lane_groups = slab.shape[0], slab.shape[1]
    for t in range(tiles):
        for g in range(lane_groups):
            lanes = slice(g * V7X_LANES, (g + 1) * V7X_LANES)
            for group in range(STREAM_TILE // V7X_SUBLANES):
                row0 = t * STREAM_TILE + group * V7X_SUBLANES
                slab[t, g, _stream_rows(group), :] = src_ref[row0:row0 + V7X_SUBLANES, lanes]
    return jnp.concatenate(
        [jnp.concatenate([slab[t, g] for g in range(lane_groups)], axis=-1)
         for t in range(tiles)], axis=0)


def _stream_to_natural(y, slab, dst_ref, dst_row0):
    lane_groups = slab.shape[1]
    for t in range(y.shape[0] // STREAM_TILE):
        for g in range(lane_groups):
            lanes = slice(g * V7X_LANES, (g + 1) * V7X_LANES)
            slab[t, g] = y[t * STREAM_TILE:(t + 1) * STREAM_TILE, lanes]
            for group in range(STREAM_TILE // V7X_SUBLANES):
                row0 = dst_row0 + t * STREAM_TILE + group * V7X_SUBLANES
                dst_ref[row0:row0 + V7X_SUBLANES, lanes] = slab[t, g, _stream_rows(group), :]


def _reorder_slab(rows, d):
    return pltpu.VMEM((rows // STREAM_TILE, d // V7X_LANES, STREAM_TILE, V7X_LANES), _F32)


def _tile_time(idx):
    steps = STREAM_TILE // V7X_SUBLANES
    shift = V7X_SUBLANES.bit_length() - 1
    return (idx & (V7X_SUBLANES - 1)) * steps + (idx >> shift)


def _layer_norm(y, g, b):
    mu = jnp.mean(y, axis=-1, keepdims=True)
    yc = y - mu
    var = jnp.mean(yc * yc, axis=-1, keepdims=True)
    return yc * lax.rsqrt(var + LN_EPS) * g + b


def _dot(a, b):
    return jnp.dot(a, b, preferred_element_type=_F32)


def _qkv_kernel(x_ref, w_ref, qk_ref, vt_ref, kmean_ref, *stream_out, d_model, scale):
    if stream_out:
        stream_ref, slab = stream_out
        x = _natural_to_stream(x_ref, slab)
        stream_ref[...] = x
    else:
        x = x_ref[...]
    xb = x.astype(_BF16)
    rows = xb.shape[0]
    chunk = QKV_COL_CHUNK
    for c in range(0, 3 * d_model, chunk):
        acc = _dot(xb, w_ref[:, c:c + chunk])
        if c >= 2 * d_model:
            vt_ref[c - 2 * d_model:c - 2 * d_model + chunk, :] = acc.T.astype(_BF16)
            continue
        if c < d_model:
            acc = acc * scale
        else:
            for r in range(rows // MOBA_BLOCK):
                blk = acc[r * MOBA_BLOCK:(r + 1) * MOBA_BLOCK, :]
                kmean_ref[0, r:r + 1, c - d_model:c - d_model + chunk] = (
                    jnp.sum(blk, axis=0, keepdims=True) * (1.0 / MOBA_BLOCK))
        qk_ref[:, c:c + chunk] = acc.astype(_BF16)


def _qkv_projection(h, w_qkv, layer, scale, batch, seq, time_ordered_input=False):
    m, d = h.shape
    blocks_per_tile = ROW_TILE // MOBA_BLOCK
    tiles_per_seq = seq // ROW_TILE
    row_spec = pl.BlockSpec((ROW_TILE, d), lambda i: (i, 0))
    out_specs = [pl.BlockSpec((ROW_TILE, 2 * d), lambda i: (i, 0)),
                 pl.BlockSpec((None, d, ROW_TILE),
                              lambda i: (i // tiles_per_seq, 0, i % tiles_per_seq)),
                 pl.BlockSpec((1, blocks_per_tile, d), lambda i: (i, 0, 0))]
    out_shape = [jax.ShapeDtypeStruct((m, 2 * d), _BF16),
                 jax.ShapeDtypeStruct((batch, d, seq), _BF16),
                 jax.ShapeDtypeStruct((m // ROW_TILE, blocks_per_tile, d), _F32)]
    scratch = []
    if time_ordered_input:
        out_specs.append(row_spec)
        out_shape.append(jax.ShapeDtypeStruct((m, d), _F32))
        scratch.append(_reorder_slab(ROW_TILE, d))
    return pl.pallas_call(
        functools.partial(_qkv_kernel, d_model=d, scale=scale),
        grid=(m // ROW_TILE,),
        in_specs=[row_spec, _resident_layer(w_qkv.shape, layer)],
        out_specs=out_specs,
        out_shape=out_shape,
        scratch_shapes=scratch,
        compiler_params=_compiler_params(("arbitrary",)),
        name="qkv_projection",
    )(h, w_qkv)


def _split_bf16(x, parts):
    terms = []
    for _ in range(parts):
        t = x.astype(_BF16)
        terms.append(t)
        x = x - t.astype(_F32)
    return terms


def _moba_kernel(slopes_ref, q_ref, k_ref, vt_ref, kmean_ref, o_ref,
                 k_aug, vt_aug, causal, s_scr, p_scr, *, n_blk):
    blk = MOBA_BLOCK
    n_heads, seq, dh2 = k_aug.shape
    dh = dh2 // 2
    bias_lanes = 3
    head_cols = lambda hd: slice(hd * dh, (hd + 1) * dh)

    @pl.when((pl.program_id(0) == 0) & (pl.program_id(1) == 0))
    def _():
        key_time = _tile_time(lax.broadcasted_iota(jnp.int32, (seq, dh), 0) & (blk - 1))
        key_lane = lax.broadcasted_iota(jnp.int32, (seq, dh), 1)
        extra = vt_aug.shape[1] - dh
        ones_row = lax.broadcasted_iota(jnp.int32, (extra, seq), 0) == 0
        for hd in range(n_heads):
            k_aug[hd, :, dh:2 * dh] = jnp.where(key_lane < bias_lanes, key_time, 0).astype(_BF16)
            vt_aug[hd, dh:dh + extra, :] = jnp.where(ones_row, 1.0, 0.0).astype(_BF16)
        key_t = _tile_time(lax.broadcasted_iota(jnp.int32, (blk, blk), 0))
        query_t = _tile_time(lax.broadcasted_iota(jnp.int32, (blk, blk), 1))
        causal[...] = jnp.where(key_t > query_t, NEG_INF, 0.0)

    lane = lax.broadcasted_iota(jnp.int32, (blk, dh), 1)
    block_id = lax.broadcasted_iota(jnp.int32, (n_blk, blk), 0)
    slope2, q_bias, km_split = [], [], []
    for hd in range(n_heads):
        k_aug[hd, :, 0:dh] = k_ref[:, head_cols(hd)]
        vt_aug[hd, 0:dh, :] = vt_ref[head_cols(hd), :]
        slope2.append(slopes_ref[pl.program_id(1) * n_heads + hd] * LOG2_E)
        bias = jnp.zeros((blk, dh), _F32)
        for i, term in enumerate(_split_bf16(jnp.full((blk, dh), slope2[hd], _F32), bias_lanes)):
            bias = jnp.where(lane == i, term.astype(_F32), bias)
        q_bias.append(bias.astype(_BF16))
        km_split.append(_split_bf16(kmean_ref[:, head_cols(hd)], 2))

    def score_stage(hd, qi):
        width = (qi + 1) * blk
        q = q_ref[qi * blk:(qi + 1) * blk, head_cols(hd)]
        s_scr[hd, qi % 2, 0:width, :] = lax.dot_general(
            k_aug[hd, 0:width, :], jnp.concatenate([q, q_bias[hd]], axis=-1), _NT,
            preferred_element_type=_F32)
        if qi <= MOBA_TOPK:
            return None
        km_hi, km_lo = km_split[hd]
        gate = (lax.dot_general(km_hi, q, _NT, preferred_element_type=_F32)
                + lax.dot_general(km_lo, q, _NT, preferred_element_type=_F32))
        ahead = jnp.zeros((n_blk, blk), _F32)
        for m in range(qi):
            g_m = gate[m:m + 1, :]
            wins = (g_m > gate) | ((g_m == gate) & (block_id > m))
            ahead = ahead + jnp.where(wins, 1.0, 0.0)
        return (ahead < float(MOBA_TOPK)) & (block_id < qi)

    def softmax_stage(hd, qi, picked):
        s_buf, p_buf = s_scr.at[hd, qi % 2], p_scr.at[hd, qi % 2]

        def block_scores(n):
            t = s_buf[n * blk:(n + 1) * blk, :]
            return t + causal[...] if n == qi else t

        query_term = [slope2[hd] * float(-blk * (qi - n)) for n in range(qi)]
        if picked is not None:
            query_term = [jnp.where(picked[n:n + 1, :], query_term[n], NEG_INF)
                          for n in range(qi)]
        m_run = jnp.max(block_scores(qi), axis=0, keepdims=True)
        for n in range(qi):
            m_run = jnp.maximum(m_run,
                                jnp.max(block_scores(n), axis=0, keepdims=True) + query_term[n])
        for n in range(qi + 1):
            shift = m_run if n == qi else m_run - query_term[n]
            p_buf[n * blk:(n + 1) * blk, :] = jnp.exp2(block_scores(n) - shift).astype(_BF16)

    def output_stage(hd, qi):
        width = (qi + 1) * blk
        out = _dot(vt_aug[hd, :, 0:width], p_scr[hd, qi % 2, 0:width, :])
        o_ref[qi * blk:(qi + 1) * blk, head_cols(hd)] = (
            out[0:dh, :] / out[dh:dh + 1, :]).T.astype(o_ref.dtype)

    heads = range(n_heads)
    picked = [score_stage(hd, 0) for hd in heads]
    for qi in range(n_blk):
        picked_next = []
        for hd in heads:
            picked_next.append(score_stage(hd, qi + 1) if qi + 1 < n_blk else None)
            softmax_stage(hd, qi, picked[hd])
            if hd > 0:
                output_stage(hd - 1, qi)
        output_stage(n_heads - 1, qi)
        picked = picked_next


def _moba_attention(qk, vt, kmean, slopes, batch, seq):
    m, d2 = qk.shape
    d = d2 // 2
    dh = d // N_HEADS
    n_blk = seq // MOBA_BLOCK
    bf16_rows = 2 * V7X_SUBLANES
    hps = MOBA_HEADS_PER_STEP
    groups = N_HEADS // hps
    wide = hps * dh
    return pl.pallas_call(
        functools.partial(_moba_kernel, n_blk=n_blk),
        grid=(batch, groups),
        in_specs=[pl.BlockSpec(memory_space=pltpu.SMEM),
                  pl.BlockSpec((seq, wide), lambda b, g: (b, g)),
                  pl.BlockSpec((seq, wide), lambda b, g: (b, groups + g)),
                  pl.BlockSpec((None, wide, seq), lambda b, g: (b, g, 0)),
                  pl.BlockSpec((None, n_blk, wide), lambda b, g: (b, 0, g))],
        out_specs=pl.BlockSpec((seq, wide), lambda b, g: (b, g)),
        out_shape=jax.ShapeDtypeStruct((m, d), _BF16),
        scratch_shapes=[pltpu.VMEM((hps, seq, 2 * dh), _BF16),
                        pltpu.VMEM((hps, dh + bf16_rows, seq), _BF16),
                        pltpu.VMEM((MOBA_BLOCK, MOBA_BLOCK), _F32),
                        pltpu.VMEM((hps, 2, seq, MOBA_BLOCK), _F32),
                        pltpu.VMEM((hps, 2, seq, MOBA_BLOCK), _BF16)],
        compiler_params=_compiler_params(("arbitrary", "arbitrary")),
        name="moba_attention",
    )(slopes, qk, qk, vt, kmean)


def _ffn_kernel(*refs, hidden, alpha, mixer_projection, time_ordered_output):
    if mixer_projection:
        o_ref, w_o_ref, g0_ref, b0_ref, *refs = refs
    x_ref, w_in_ref, w_out_ref, g_ref, b_ref, out_ref, *slab = refs
    groups = range(0, x_ref.shape[0], FFN_SUB_ROWS)
    inputs = [x_ref[r:r + FFN_SUB_ROWS, :] for r in groups]
    if mixer_projection:
        inputs = [_layer_norm(_dot(o_ref[r:r + FFN_SUB_ROWS, :], w_o_ref[...]) + alpha * x,
                              g0_ref[...], b0_ref[...]) for r, x in zip(groups, inputs)]
    for r, x in zip(groups, inputs):
        xb = x.astype(_BF16)
        acc = alpha * x
        for c in range(0, hidden, FFN_CHUNK):
            gate = _dot(xb, w_in_ref[:, c:c + FFN_CHUNK])
            up = _dot(xb, w_in_ref[:, hidden + c:hidden + c + FFN_CHUNK])
            act = (gate * jax.nn.sigmoid(gate) * up).astype(_BF16)
            acc = acc + _dot(act, w_out_ref[c:c + FFN_CHUNK, :])
        y = _layer_norm(acc, g_ref[...], b_ref[...])
        if time_ordered_output:
            _stream_to_natural(y, slab[0], out_ref, r)
        else:
            out_ref[r:r + FFN_SUB_ROWS, :] = y


def _ffn_layer(h, w_in, w_out, layer, g, b, alpha, mixer_projection=None,
               time_ordered_output=False):
    m, d = h.shape
    hidden = w_out.shape[1]
    row_spec = lambda width: pl.BlockSpec((FFN_ROW_TILE, width), lambda i: (i, 0))
    operands = [h, w_in, w_out, g, b]
    in_specs = [row_spec(d), _resident_layer(w_in.shape, layer),
                _resident_layer(w_out.shape, layer), _resident((1, d)), _resident((1, d))]
    if mixer_projection is not None:
        o, w_o, o_layer, g0, b0 = mixer_projection
        operands = [o, w_o, g0, b0] + operands
        in_specs = [row_spec(o.shape[1]), _resident_layer(w_o.shape, o_layer),
                    _resident((1, d)), _resident((1, d))] + in_specs
    return pl.pallas_call(
        functools.partial(_ffn_kernel, hidden=hidden, alpha=alpha,
                          mixer_projection=mixer_projection is not None,
                          time_ordered_output=time_ordered_output),
        grid=(m // FFN_ROW_TILE,),
        in_specs=in_specs,
        out_specs=row_spec(d),
        out_shape=jax.ShapeDtypeStruct((m, d), _F32),
        scratch_shapes=[_reorder_slab(FFN_SUB_ROWS, d)] if time_ordered_output else [],
        compiler_params=_compiler_params(("arbitrary",)),
        name="swiglu_ffn_ln",
    )(*operands)


def _gelu_tanh(x):
    inner = x * (0.7978845608028654 + (0.7978845608028654 * 0.044715) * (x * x))
    half_x = 0.5 * x
    return half_x * jnp.tanh(inner) + half_x


def _softplus(z):
    return jnp.maximum(z, 0.0) + jnp.log1p(jnp.exp(-jnp.abs(z)))


def _lru_kernel(x_ref, x_next_ref, w_in_ref, conv_w_ref, conv_b_ref, w_gate_ref, b_a_ref,
                b_x_ref, lam_ref, w_out_ref, g_ref, b_ref, out_ref,
                xbuf, gate_pre, tail, a_s, b_s, h_s, ac_s, carry, *, alpha, steps_per_seq):
    sub = V7X_SUBLANES
    seq_tile, width = a_s.shape
    n_tiles = x_ref.shape[0] // seq_tile
    steps = seq_tile // sub
    halo = (CONV_WIDTH - 1) * sub
    group = width // LRU_BLOCKS
    step = pl.program_id(0)
    cur, nxt = step % 2, (step + 1) % 2

    def in_projection(src_ref, slot, t):
        xb = src_ref[t * seq_tile:(t + 1) * seq_tile, :].astype(_BF16)
        xbuf[slot, t, halo:halo + seq_tile, :] = _dot(xb, w_in_ref[:, 0:width])
        gate_pre[slot, t] = _dot(xb, w_in_ref[:, width:2 * width])

    @pl.when(step == 0)
    def _():
        for t in range(n_tiles):
            in_projection(x_ref, 0, t)

    @pl.when(step % steps_per_seq == 0)
    def _():
        tail[...] = jnp.zeros_like(tail)
        carry[...] = jnp.zeros_like(carry)

    half_rate = (-0.5 * LRU_C * LOG2_E) * _softplus(-lam_ref[...])

    def conv(t):
        first_sublane = lax.broadcasted_iota(jnp.int32, (sub, width), 0) == 0
        for g in range(CONV_WIDTH - 1):
            rows = slice(g * sub, (g + 1) * sub)
            last = xbuf[cur, t, seq_tile + g * sub:seq_tile + (g + 1) * sub, :]
            xbuf[cur, t, rows, :] = jnp.where(first_sublane,
                                              pltpu.roll(tail[rows, :], 1, axis=0),
                                              pltpu.roll(last, 1, axis=0))
            tail[rows, :] = last
        xc = conv_b_ref[...]
        for tap in range(CONV_WIDTH):
            start = halo - (CONV_WIDTH - 1 - tap) * sub
            xc = xc + xbuf[cur, t, start:start + seq_tile, :] * conv_w_ref[tap:tap + 1, :]
        return xc

    def gate_projection(xc):
        xcb = xc.astype(_BF16)
        pre_a, pre_x = [], []
        for gidx in range(LRU_BLOCKS):
            both = _dot(xcb[:, gidx * group:(gidx + 1) * group], w_gate_ref[gidx])
            pre_a.append(both[:, 0:group])
            pre_x.append(both[:, group:2 * group])
        return jnp.concatenate(pre_a, axis=-1), jnp.concatenate(pre_x, axis=-1)

    def recurrence(xc, pre_a, pre_x):
        tanh_a = jnp.tanh(0.5 * (pre_a + b_a_ref[...]))
        tanh_x = jnp.tanh(0.5 * (pre_x + b_x_ref[...]))
        a = jnp.exp2(tanh_a * half_rate + half_rate)
        half_xc = 0.5 * xc
        gated_in = tanh_x * half_xc + half_xc
        y = 1.0 - a * a
        a_s[...] = a
        b_s[...] = jnp.where(y > 0.0, y * lax.rsqrt(y), 0.0) * gated_in

        h_loc = jnp.zeros((sub, width), _F32)
        a_cum = jnp.ones((sub, width), _F32)
        for j in range(steps):
            rows = slice(j * sub, (j + 1) * sub)
            a_j = a_s[rows, :]
            h_loc = a_j * h_loc + b_s[rows, :]
            a_cum = a_j * a_cum
            h_s[rows, :] = h_loc
            ac_s[rows, :] = a_cum

        state = carry[...]
        starts = []
        for c in range(sub):
            starts.append(state)
            state = h_loc[c:c + 1, :] + a_cum[c:c + 1, :] * state
        carry[...] = state
        start = jnp.tile(jnp.concatenate(starts, axis=0), (steps, 1))
        return h_s[...] + ac_s[...] * start

    def out_projection(t, h):
        rows = slice(t * seq_tile, (t + 1) * seq_tile)
        gated = (h * _gelu_tanh(gate_pre[cur, t])).astype(_BF16)
        y = _dot(gated, w_out_ref[...]) + alpha * x_ref[rows, :]
        out_ref[rows, :] = _layer_norm(y, g_ref[...], b_ref[...])

    for t in range(n_tiles):
        xc = conv(t)
        pre_a, pre_x = gate_projection(xc)
        in_projection(x_next_ref, nxt, t)
        out_projection(t, recurrence(xc, pre_a, pre_x))


def _lru_layer(h, batch, seq, layer, w_in, conv_w, conv_b, w_gate, b_a, b_x, lam, w_out, g, b,
               alpha):
    m, d = h.shape
    width = w_out.shape[1]
    rows = LRU_TILES * STREAM_TILE
    steps_per_seq = seq // rows
    n_steps = m // rows
    halo = (CONV_WIDTH - 1) * V7X_SUBLANES
    tile_f32 = pltpu.VMEM((STREAM_TILE, width), _F32)
    row_spec = pl.BlockSpec((rows, d), lambda i: (i, 0))
    next_spec = pl.BlockSpec((rows, d), lambda i: (jnp.minimum(i + 1, n_steps - 1), 0))
    return pl.pallas_call(
        functools.partial(_lru_kernel, alpha=alpha, steps_per_seq=steps_per_seq),
        grid=(n_steps,),
        in_specs=[row_spec, next_spec, _resident_layer(w_in.shape, layer),
                  _resident(conv_w.shape),
                  _resident(conv_b.shape), _resident_layer(w_gate.shape, layer),
                  _resident(b_a.shape), _resident(b_x.shape), _resident(lam.shape),
                  _resident_layer(w_out.shape, layer), _resident(g.shape), _resident(b.shape)],
        out_specs=row_spec,
        out_shape=jax.ShapeDtypeStruct((m, d), _F32),
        scratch_shapes=[pltpu.VMEM((2, LRU_TILES, halo + STREAM_TILE, width), _F32),
                        pltpu.VMEM((2, LRU_TILES, STREAM_TILE, width), _F32),
                        pltpu.VMEM((halo, width), _F32),
                        tile_f32,
                        tile_f32,
                        tile_f32,
                        tile_f32,
                        pltpu.VMEM((1, width), _F32)],
        compiler_params=_compiler_params(("arbitrary",)),
        name="rglru_block_ln",
    )(h, h, w_in, conv_w, conv_b, w_gate, b_a, b_x, lam, w_out, g, b)


def kernel(x, attn_w_qkv, attn_w_o, lru_w_in, lru_conv_w, lru_conv_b, lru_w_a, lru_b_a,
           lru_w_x, lru_b_x, lru_lambda, lru_w_out, ffn_w_in, ffn_w_out, ln_g, ln_b):
    batch, seq, d = x.shape
    depth = ffn_w_in.shape[0]
    rows = batch * seq
    assert d % N_HEADS == 0 and N_HEADS % MOBA_HEADS_PER_STEP == 0
    assert seq % ROW_TILE == 0 and ROW_TILE % MOBA_BLOCK == 0 and d % QKV_COL_CHUNK == 0
    assert seq % (LRU_TILES * STREAM_TILE) == 0
    assert rows % FFN_ROW_TILE == 0 and FFN_ROW_TILE % FFN_SUB_ROWS == 0
    assert FFN_SUB_ROWS % STREAM_TILE == 0 and ffn_w_out.shape[1] % FFN_CHUNK == 0
    alpha = (2 * depth) ** 0.25
    head_dim = d // N_HEADS
    n_blk = seq // MOBA_BLOCK
    slopes = jnp.exp2(-8.0 * (jnp.arange(N_HEADS, dtype=_F32) + 1.0) / N_HEADS)
    row = lambda v: v.reshape(1, -1)

    w_qkv, w_o = attn_w_qkv.astype(_BF16), attn_w_o.astype(_BF16)
    w_lru_in, w_lru_out = lru_w_in.astype(_BF16), lru_w_out.astype(_BF16)
    w_gate = jnp.concatenate([lru_w_a, lru_w_x], axis=-1).astype(_BF16)
    w_ffn_in, w_ffn_out = ffn_w_in.astype(_BF16), ffn_w_out.astype(_BF16)

    h = x.reshape(rows, d)
    for layer in range(depth):
        j = layer // 2
        g0, b0 = row(ln_g[layer, 0]), row(ln_b[layer, 0])
        g1, b1 = row(ln_g[layer, 1]), row(ln_b[layer, 1])
        if layer % 2 == 0:
            qk, vt, kmean, *stream = _qkv_projection(h, w_qkv, j, head_dim ** -0.5 * LOG2_E,
                                                     batch, seq, time_ordered_input=layer == 0)
            if stream:
                h, = stream
            o = _moba_attention(qk, vt, kmean.reshape(batch, n_blk, d), slopes, batch, seq)
            mixer_projection = (o, w_o, j, g0, b0)
        else:
            h = _lru_layer(h, batch, seq, j, w_lru_in, lru_conv_w[j], row(lru_conv_b[j]),
                           w_gate, row(lru_b_a[j]), row(lru_b_x[j]), row(lru_lambda[j]),
                           w_lru_out, g0, b0, alpha)
            mixer_projection = None
        h = _ffn_layer(h, w_ffn_in, w_ffn_out, layer, g1, b1, alpha, mixer_projection,
                       time_ordered_output=layer == depth - 1)
    return h.reshape(batch, seq, d)
```

```python
import functools

import jax
import jax.numpy as jnp
from jax import lax
from jax.experimental import pallas as pl
from jax.experimental.pallas import tpu as pltpu

N_HEADS = 8
MOBA_BLOCK = 256
MOBA_TOPK = 3
NEG_INF = -1e30
LRU_BLOCKS = 8
CONV_WIDTH = 4
LRU_C = 8.0
LN_EPS = 1e-5
LOG2_E = 1.4426950408889634

V7X_SUBLANES = 8
V7X_LANES = 128
V7X_VMEM_LIMIT_BYTES = 56 * 1024 * 1024

ROW_TILE = 1024
QKV_COL_CHUNK = 512
STREAM_TILE = MOBA_BLOCK
FFN_CHUNK = 256
FFN_ROW_TILE = 1024
FFN_SUB_ROWS = 512
LRU_TILES = 2
MOBA_HEADS_PER_STEP = 4
MOBA_BIAS_LANES = 3
MOBA_VT_PAD = 16

_F32 = jnp.float32
_BF16 = jnp.bfloat16
_NT = (((1,), (1,)), ((), ()))


def _compiler_params(semantics):
    return pltpu.CompilerParams(dimension_semantics=semantics,
                                vmem_limit_bytes=V7X_VMEM_LIMIT_BYTES)


def _resident(shape):
    zeros = (0,) * len(shape)
    return pl.BlockSpec(shape, lambda *_: zeros, pipeline_mode=pl.Buffered(1))


def _resident_layer(stacked_shape, layer):
    index = (layer,) + (0,) * (len(stacked_shape) - 1)
    return pl.BlockSpec((None,) + tuple(stacked_shape[1:]), lambda *_: index,
                        pipeline_mode=pl.Buffered(1))


def _stream_rows(natural_group):
    steps = STREAM_TILE // V7X_SUBLANES
    chunk, first_step = divmod(natural_group * V7X_SUBLANES, steps)
    return pl.ds(first_step * V7X_SUBLANES + chunk, V7X_SUBLANES, stride=V7X_SUBLANES)


def _natural_to_stream(src_ref, slab):
    tiles, lane_groups = slab.shape[0], slab.shape[1]
    for t in range(tiles):
        for g in range(lane_groups):
            lanes = slice(g * V7X_LANES, (g + 1) * V7X_LANES)
            for group in range(STREAM_TILE // V7X_SUBLANES):
                row0 = t * STREAM_TILE + group * V7X_SUBLANES
                slab[t, g, _stream_rows(group), :] = src_ref[row0:row0 + V7X_SUBLANES, lanes]
    return jnp.concatenate(
        [jnp.concatenate([slab[t, g] for g in range(lane_groups)], axis=-1)
         for t in range(tiles)], axis=0)


def _stream_to_natural(y, slab, dst_ref, dst_row0):
    lane_groups = slab.shape[1]
    for t in range(y.shape[0] // STREAM_TILE):
        for g in range(lane_groups):
            lanes = slice(g * V7X_LANES, (g + 1) * V7X_LANES)
            slab[t, g] = y[t * STREAM_TILE:(t + 1) * STREAM_TILE, lanes]
            for group in range(STREAM_TILE // V7X_SUBLANES):
                row0 = dst_row0 + t * STREAM_TILE + group * V7X_SUBLANES
                dst_ref[row0:row0 + V7X_SUBLANES, lanes] = slab[t, g, _stream_rows(group), :]


def _reorder_slab(rows, d):
    return pltpu.VMEM((rows // STREAM_TILE, d // V7X_LANES, STREAM_TILE, V7X_LANES), _F32)


def _tile_time(idx):
    steps = STREAM_TILE // V7X_SUBLANES
    shift = V7X_SUBLANES.bit_length() - 1
    return (idx & (V7X_SUBLANES - 1)) * steps + (idx >> shift)


def _layer_norm(y, g, b):
    mu = jnp.mean(y, axis=-1, keepdims=True)
    yc = y - mu
    var = jnp.mean(yc * yc, axis=-1, keepdims=True)
    return yc * lax.rsqrt(var + LN_EPS) * g + b


def _dot(a, b):
    return jnp.dot(a, b, preferred_element_type=_F32)


def _qkv_kernel(x_ref, w_ref, q_ref, k_ref, vt_ref, kmean_ref, *stream_out, d_model, scale):
    if stream_out:
        stream_ref, slab = stream_out
        x = _natural_to_stream(x_ref, slab)
        stream_ref[...] = x
    else:
        x = x_ref[...]
    xb = x.astype(_BF16)
    rows = xb.shape[0]
    chunk = QKV_COL_CHUNK
    dh = d_model // N_HEADS
    key_time = _tile_time(lax.broadcasted_iota(jnp.int32, (rows, dh), 0) & (MOBA_BLOCK - 1))
    key_lane = lax.broadcasted_iota(jnp.int32, (rows, dh), 1)
    time_lanes = jnp.where(key_lane < MOBA_BIAS_LANES, key_time, 0).astype(_BF16)
    ones_rows = jnp.where(lax.broadcasted_iota(jnp.int32, (MOBA_VT_PAD, rows), 0) == 0,
                          1.0, 0.0).astype(_BF16)
    for c in range(0, 3 * d_model, chunk):
        acc = _dot(xb, w_ref[:, c:c + chunk])
        if c >= 2 * d_model:
            acc_t = acc.T.astype(_BF16)
            for i in range(chunk // dh):
                base = ((c - 2 * d_model) // dh + i) * (dh + MOBA_VT_PAD)
                vt_ref[base:base + dh, :] = acc_t[i * dh:(i + 1) * dh, :]
                vt_ref[base + dh:base + dh + MOBA_VT_PAD, :] = ones_rows
        elif c >= d_model:
            for r in range(rows // MOBA_BLOCK):
                blk = acc[r * MOBA_BLOCK:(r + 1) * MOBA_BLOCK, :]
                kmean_ref[0, r:r + 1, c - d_model:c - d_model + chunk] = (
                    jnp.sum(blk, axis=0, keepdims=True) * (1.0 / MOBA_BLOCK))
            k = acc.astype(_BF16)
            for i in range(chunk // dh):
                base = ((c - d_model) // dh + i) * 2 * dh
                k_ref[:, base:base + dh] = k[:, i * dh:(i + 1) * dh]
                k_ref[:, base + dh:base + 2 * dh] = time_lanes
        else:
            q_ref[:, c:c + chunk] = (acc * scale).astype(_BF16)


def _qkv_projection(h, w_qkv, layer, scale, batch, seq, time_ordered_input=False):
    m, d = h.shape
    blocks_per_tile = ROW_TILE // MOBA_BLOCK
    tiles_per_seq = seq // ROW_TILE
    row_spec = pl.BlockSpec((ROW_TILE, d), lambda i: (i, 0))
    vt_rows = N_HEADS * (d // N_HEADS + MOBA_VT_PAD)
    out_specs = [row_spec,
                 pl.BlockSpec((ROW_TILE, 2 * d), lambda i: (i, 0)),
                 pl.BlockSpec((None, vt_rows, ROW_TILE),
                              lambda i: (i // tiles_per_seq, 0, i % tiles_per_seq)),
                 pl.BlockSpec((1, blocks_per_tile, d), lambda i: (i, 0, 0))]
    out_shape = [jax.ShapeDtypeStruct((m, d), _BF16),
                 jax.ShapeDtypeStruct((m, 2 * d), _BF16),
                 jax.ShapeDtypeStruct((batch, vt_rows, seq), _BF16),
                 jax.ShapeDtypeStruct((m // ROW_TILE, blocks_per_tile, d), _F32)]
    scratch = []
    if time_ordered_input:
        out_specs.append(row_spec)
        out_shape.append(jax.ShapeDtypeStruct((m, d), _F32))
        scratch.append(_reorder_slab(ROW_TILE, d))
    return pl.pallas_call(
        functools.partial(_qkv_kernel, d_model=d, scale=scale),
        grid=(m // ROW_TILE,),
        in_specs=[row_spec, _resident_layer(w_qkv.shape, layer)],
        out_specs=out_specs,
        out_shape=out_shape,
        scratch_shapes=scratch,
        compiler_params=_compiler_params(("arbitrary",)),
        name="qkv_projection",
    )(h, w_qkv)


def _split_bf16(x, parts):
    terms = []
    for _ in range(parts):
        t = x.astype(_BF16)
        terms.append(t)
        x = x - t.astype(_F32)
    return terms


def _moba_kernel(slopes_ref, q_ref, k_ref, vt_ref, kmean_ref, o_ref,
                 causal, s_scr, p_scr, *, n_blk):
    blk = MOBA_BLOCK
    n_heads = s_scr.shape[0]
    dh = q_ref.shape[1] // n_heads
    vt_rows = dh + MOBA_VT_PAD
    bias_lanes = MOBA_BIAS_LANES
    head_cols = lambda hd: slice(hd * dh, (hd + 1) * dh)

    @pl.when((pl.program_id(0) == 0) & (pl.program_id(1) == 0))
    def _():
        key_t = _tile_time(lax.broadcasted_iota(jnp.int32, (blk, blk), 0))
        query_t = _tile_time(lax.broadcasted_iota(jnp.int32, (blk, blk), 1))
        causal[...] = jnp.where(key_t > query_t, NEG_INF, 0.0)

    lane = lax.broadcasted_iota(jnp.int32, (blk, dh), 1)
    block_id = lax.broadcasted_iota(jnp.int32, (n_blk, blk), 0)
    slope2, q_bias, km_split = [], [], []
    for hd in range(n_heads):
        slope2.append(slopes_ref[pl.program_id(1) * n_heads + hd] * LOG2_E)
        bias = jnp.zeros((blk, dh), _F32)
        for i, term in enumerate(_split_bf16(jnp.full((blk, dh), slope2[hd], _F32), bias_lanes)):
            bias = jnp.where(lane == i, term.astype(_F32), bias)
        q_bias.append(bias.astype(_BF16))
        km_split.append(_split_bf16(kmean_ref[:, head_cols(hd)], 2))

    def score_stage(hd, qi):
        width = (qi + 1) * blk
        q = q_ref[qi * blk:(qi + 1) * blk, head_cols(hd)]
        s_scr[hd, qi % 2, 0:width, :] = lax.dot_general(
            k_ref[0:width, hd * 2 * dh:(hd + 1) * 2 * dh],
            jnp.concatenate([q, q_bias[hd]], axis=-1), _NT,
            preferred_element_type=_F32)
        if qi <= MOBA_TOPK:
            return None
        km_hi, km_lo = km_split[hd]
        gate = (lax.dot_general(km_hi, q, _NT, preferred_element_type=_F32)
                + lax.dot_general(km_lo, q, _NT, preferred_element_type=_F32))
        ahead = jnp.zeros((n_blk, blk), _F32)
        for m in range(qi):
            g_m = gate[m:m + 1, :]
            wins = (g_m > gate) | ((g_m == gate) & (block_id > m))
            ahead = ahead + jnp.where(wins, 1.0, 0.0)
        return (ahead < float(MOBA_TOPK)) & (block_id < qi)

    def softmax_stage(hd, qi, picked):
        s_buf, p_buf = s_scr.at[hd, qi % 2], p_scr.at[hd, qi % 2]

        def block_scores(n):
            t = s_buf[n * blk:(n + 1) * blk, :]
            return t + causal[...] if n == qi else t

        query_term = [slope2[hd] * float(-blk * (qi - n)) for n in range(qi)]
        if picked is not None:
            query_term = [jnp.where(picked[n:n + 1, :], query_term[n], NEG_INF)
                          for n in range(qi)]
        m_run = jnp.max(block_scores(qi), axis=0, keepdims=True)
        for n in range(qi):
            m_run = jnp.maximum(m_run,
                                jnp.max(block_scores(n), axis=0, keepdims=True) + query_term[n])
        for n in range(qi + 1):
            shift = m_run if n == qi else m_run - query_term[n]
            p_buf[n * blk:(n + 1) * blk, :] = jnp.exp2(block_scores(n) - shift).astype(_BF16)

    def output_stage(hd, qi):
        width = (qi + 1) * blk
        out = _dot(vt_ref[hd * vt_rows:(hd + 1) * vt_rows, 0:width], p_scr[hd, qi % 2, 0:width, :])
        o_ref[qi * blk:(qi + 1) * blk, head_cols(hd)] = (
            out[0:dh, :] / out[dh:dh + 1, :]).T.astype(o_ref.dtype)

    heads = range(n_heads)
    picked = [score_stage(hd, 0) for hd in heads]
    for qi in range(n_blk):
        picked_next = []
        for hd in heads:
            picked_next.append(score_stage(hd, qi + 1) if qi + 1 < n_blk else None)
            softmax_stage(hd, qi, picked[hd])
            if hd > 0:
                output_stage(hd - 1, qi)
        output_stage(n_heads - 1, qi)
        picked = picked_next


def _moba_attention(q, k, vt, kmean, slopes, batch, seq):
    m, d = q.shape
    dh = d // N_HEADS
    n_blk = seq // MOBA_BLOCK
    hps = MOBA_HEADS_PER_STEP
    groups = N_HEADS // hps
    wide = hps * dh
    return pl.pallas_call(
        functools.partial(_moba_kernel, n_blk=n_blk),
        grid=(batch, groups),
        in_specs=[pl.BlockSpec(memory_space=pltpu.SMEM),
                  pl.BlockSpec((seq, wide), lambda b, g: (b, g)),
                  pl.BlockSpec((seq, 2 * wide), lambda b, g: (b, g)),
                  pl.BlockSpec((None, hps * (dh + MOBA_VT_PAD), seq), lambda b, g: (b, g, 0)),
                  pl.BlockSpec((None, n_blk, wide), lambda b, g: (b, 0, g))],
        out_specs=pl.BlockSpec((seq, wide), lambda b, g: (b, g)),
        out_shape=jax.ShapeDtypeStruct((m, d), _BF16),
        scratch_shapes=[pltpu.VMEM((MOBA_BLOCK, MOBA_BLOCK), _F32),
                        pltpu.VMEM((hps, 2, seq, MOBA_BLOCK), _F32),
                        pltpu.VMEM((hps, 2, seq, MOBA_BLOCK), _BF16)],
        compiler_params=_compiler_params(("arbitrary", "arbitrary")),
        name="moba_attention",
    )(slopes, q, k, vt, kmean)


def _ffn_kernel(*refs, hidden, alpha, mixer_projection, time_ordered_output):
    if mixer_projection:
        o_ref, w_o_ref, g0_ref, b0_ref, *refs = refs
    x_ref, w_in_ref, w_out_ref, g_ref, b_ref, out_ref, *slab = refs
    groups = range(0, x_ref.shape[0], FFN_SUB_ROWS)
    inputs = [x_ref[r:r + FFN_SUB_ROWS, :] for r in groups]
    if mixer_projection:
        inputs = [_layer_norm(_dot(o_ref[r:r + FFN_SUB_ROWS, :], w_o_ref[...]) + alpha * x,
                              g0_ref[...], b0_ref[...]) for r, x in zip(groups, inputs)]
    for r, x in zip(groups, inputs):
        xb = x.astype(_BF16)
        acc = alpha * x
        for c in range(0, hidden, FFN_CHUNK):
            gate = _dot(xb, w_in_ref[:, c:c + FFN_CHUNK])
            up = _dot(xb, w_in_ref[:, hidden + c:hidden + c + FFN_CHUNK])
            act = (gate * jax.nn.sigmoid(gate) * up).astype(_BF16)
            acc = acc + _dot(act, w_out_ref[c:c + FFN_CHUNK, :])
        y = _layer_norm(acc, g_ref[...], b_ref[...])
        if time_ordered_output:
            _stream_to_natural(y, slab[0], out_ref, r)
        else:
            out_ref[r:r + FFN_SUB_ROWS, :] = y


def _ffn_layer(h, w_in, w_out, layer, g, b, alpha, mixer_projection=None,
               time_ordered_output=False):
    m, d = h.shape
    hidden = w_out.shape[1]
    row_spec = lambda width: pl.BlockSpec((FFN_ROW_TILE, width), lambda i: (i, 0))
    operands = [h, w_in, w_out, g, b]
    in_specs = [row_spec(d), _resident_layer(w_in.shape, layer),
                _resident_layer(w_out.shape, layer), _resident((1, d)), _resident((1, d))]
    if mixer_projection is not None:
        o, w_o, o_layer, g0, b0 = mixer_projection
        operands = [o, w_o, g0, b0] + operands
        in_specs = [row_spec(o.shape[1]), _resident_layer(w_o.shape, o_layer),
                    _resident((1, d)), _resident((1, d))] + in_specs
    return pl.pallas_call(
        functools.partial(_ffn_kernel, hidden=hidden, alpha=alpha,
                          mixer_projection=mixer_projection is not None,
                          time_ordered_output=time_ordered_output),
        grid=(m // FFN_ROW_TILE,),
        in_specs=in_specs,
        out_specs=row_spec(d),
        out_shape=jax.ShapeDtypeStruct((m, d), _F32),
        scratch_shapes=[_reorder_slab(FFN_SUB_ROWS, d)] if time_ordered_output else [],
        compiler_params=_compiler_params(("arbitrary",)),
        name="swiglu_ffn_ln",
    )(*operands)


def _gelu_tanh(x):
    inner = x * (0.7978845608028654 + (0.7978845608028654 * 0.044715) * (x * x))
    half_x = 0.5 * x
    return half_x * jnp.tanh(inner) + half_x


def _softplus(z):
    return jnp.maximum(z, 0.0) + jnp.log1p(jnp.exp(-jnp.abs(z)))


def _lru_kernel(x_ref, x_next_ref, w_in_ref, conv_w_ref, conv_b_ref, w_gate_ref, b_a_ref,
                b_x_ref, lam_ref, w_out_ref, g_ref, b_ref, out_ref,
                xbuf, gate_pre, tail, a_s, b_s, h_s, ac_s, carry, *, alpha, steps_per_seq):
    sub = V7X_SUBLANES
    seq_tile, width = a_s.shape
    n_tiles = x_ref.shape[0] // seq_tile
    steps = seq_tile // sub
    halo = (CONV_WIDTH - 1) * sub
    group = width // LRU_BLOCKS
    step = pl.program_id(0)
    cur, nxt = step % 2, (step + 1) % 2

    def in_projection(src_ref, slot, t):
        xb = src_ref[t * seq_tile:(t + 1) * seq_tile, :].astype(_BF16)
        xbuf[slot, t, halo:halo + seq_tile, :] = _dot(xb, w_in_ref[:, 0:width])
        gate_pre[slot, t] = _dot(xb, w_in_ref[:, width:2 * width])

    @pl.when(step == 0)
    def _():
        for t in range(n_tiles):
            in_projection(x_ref, 0, t)

    @pl.when(step % steps_per_seq == 0)
    def _():
        tail[...] = jnp.zeros_like(tail)
        carry[...] = jnp.zeros_like(carry)

    half_rate = (-0.5 * LRU_C * LOG2_E) * _softplus(-lam_ref[...])

    def conv(t):
        first_sublane = lax.broadcasted_iota(jnp.int32, (sub, width), 0) == 0
        for g in range(CONV_WIDTH - 1):
            rows = slice(g * sub, (g + 1) * sub)
            last = xbuf[cur, t, seq_tile + g * sub:seq_tile + (g + 1) * sub, :]
            xbuf[cur, t, rows, :] = jnp.where(first_sublane,
                                              pltpu.roll(tail[rows, :], 1, axis=0),
                                              pltpu.roll(last, 1, axis=0))
            tail[rows, :] = last
        xc = conv_b_ref[...]
        for tap in range(CONV_WIDTH):
            start = halo - (CONV_WIDTH - 1 - tap) * sub
            xc = xc + xbuf[cur, t, start:start + seq_tile, :] * conv_w_ref[tap:tap + 1, :]
        return xc

    def gate_projection(xc):
        xcb = xc.astype(_BF16)
        pre_a, pre_x = [], []
        for gidx in range(LRU_BLOCKS):
            both = _dot(xcb[:, gidx * group:(gidx + 1) * group], w_gate_ref[gidx])
            pre_a.append(both[:, 0:group])
            pre_x.append(both[:, group:2 * group])
        return jnp.concatenate(pre_a, axis=-1), jnp.concatenate(pre_x, axis=-1)

    def recurrence(xc, pre_a, pre_x):
        tanh_a = jnp.tanh(0.5 * (pre_a + b_a_ref[...]))
        tanh_x = jnp.tanh(0.5 * (pre_x + b_x_ref[...]))
        a = jnp.exp2(tanh_a * half_rate + half_rate)
        half_xc = 0.5 * xc
        gated_in = tanh_x * half_xc + half_xc
        y = 1.0 - a * a
        a_s[...] = a
        b_s[...] = jnp.where(y > 0.0, y * lax.rsqrt(y), 0.0) * gated_in

        h_loc = jnp.zeros((sub, width), _F32)
        a_cum = jnp.ones((sub, width), _F32)
        for j in range(steps):
            rows = slice(j * sub, (j + 1) * sub)
            a_j = a_s[rows, :]
            h_loc = a_j * h_loc + b_s[rows, :]
            a_cum = a_j * a_cum
            h_s[rows, :] = h_loc
            ac_s[rows, :] = a_cum

        state = carry[...]
        starts = []
        for c in range(sub):
            starts.append(state)
            state = h_loc[c:c + 1, :] + a_cum[c:c + 1, :] * state
        carry[...] = state
        start = jnp.tile(jnp.concatenate(starts, axis=0), (steps, 1))
        return h_s[...] + ac_s[...] * start

    def out_projection(t, h):
        rows = slice(t * seq_tile, (t + 1) * seq_tile)
        gated = (h * _gelu_tanh(gate_pre[cur, t])).astype(_BF16)
        y = _dot(gated, w_out_ref[...]) + alpha * x_ref[rows, :]
        out_ref[rows, :] = _layer_norm(y, g_ref[...], b_ref[...])

    for t in range(n_tiles):
        xc = conv(t)
        pre_a, pre_x = gate_projection(xc)
        in_projection(x_next_ref, nxt, t)
        out_projection(t, recurrence(xc, pre_a, pre_x))


def _lru_layer(h, batch, seq, layer, w_in, conv_w, conv_b, w_gate, b_a, b_x, lam, w_out, g, b,
               alpha):
    m, d = h.shape
    width = w_out.shape[1]
    rows = LRU_TILES * STREAM_TILE
    steps_per_seq = seq // rows
    n_steps = m // rows
    halo = (CONV_WIDTH - 1) * V7X_SUBLANES
    tile_f32 = pltpu.VMEM((STREAM_TILE, width), _F32)
    row_spec = pl.BlockSpec((rows, d), lambda i: (i, 0))
    next_spec = pl.BlockSpec((rows, d), lambda i: (jnp.minimum(i + 1, n_steps - 1), 0))
    return pl.pallas_call(
        functools.partial(_lru_kernel, alpha=alpha, steps_per_seq=steps_per_seq),
        grid=(n_steps,),
        in_specs=[row_spec, next_spec, _resident_layer(w_in.shape, layer),
                  _resident(conv_w.shape),
                  _resident(conv_b.shape), _resident_layer(w_gate.shape, layer),
                  _resident(b_a.shape), _resident(b_x.shape), _resident(lam.shape),
                  _resident_layer(w_out.shape, layer), _resident(g.shape), _resident(b.shape)],
        out_specs=row_spec,
        out_shape=jax.ShapeDtypeStruct((m, d), _F32),
        scratch_shapes=[pltpu.VMEM((2, LRU_TILES, halo + STREAM_TILE, width), _F32),
                        pltpu.VMEM((2, LRU_TILES, STREAM_TILE, width), _F32),
                        pltpu.VMEM((halo, width), _F32),
                        tile_f32,
                        tile_f32,
                        tile_f32,
                        tile_f32,
                        pltpu.VMEM((1, width), _F32)],
        compiler_params=_compiler_params(("arbitrary",)),
        name="rglru_block_ln",
    )(h, h, w_in, conv_w, conv_b, w_gate, b_a, b_x, lam, w_out, g, b)


def kernel(x, attn_w_qkv, attn_w_o, lru_w_in, lru_conv_w, lru_conv_b, lru_w_a, lru_b_a,
           lru_w_x, lru_b_x, lru_lambda, lru_w_out, ffn_w_in, ffn_w_out, ln_g, ln_b):
    batch, seq, d = x.shape
    depth = ffn_w_in.shape[0]
    rows = batch * seq
    assert d % N_HEADS == 0 and N_HEADS % MOBA_HEADS_PER_STEP == 0
    assert seq % ROW_TILE == 0 and ROW_TILE % MOBA_BLOCK == 0 and d % QKV_COL_CHUNK == 0
    assert seq % (LRU_TILES * STREAM_TILE) == 0
    assert rows % FFN_ROW_TILE == 0 and FFN_ROW_TILE % FFN_SUB_ROWS == 0
    assert FFN_SUB_ROWS % STREAM_TILE == 0 and ffn_w_out.shape[1] % FFN_CHUNK == 0
    alpha = (2 * depth) ** 0.25
    head_dim = d // N_HEADS
    n_blk = seq // MOBA_BLOCK
    slopes = jnp.exp2(-8.0 * (jnp.arange(N_HEADS, dtype=_F32) + 1.0) / N_HEADS)
    row = lambda v: v.reshape(1, -1)

    w_qkv, w_o = attn_w_qkv.astype(_BF16), attn_w_o.astype(_BF16)
    w_lru_in, w_lru_out = lru_w_in.astype(_BF16), lru_w_out.astype(_BF16)
    w_gate = jnp.concatenate([lru_w_a, lru_w_x], axis=-1).astype(_BF16)
    w_ffn_in, w_ffn_out = ffn_w_in.astype(_BF16), ffn_w_out.astype(_BF16)

    h = x.reshape(rows, d)
    for layer in range(depth):
        j = layer // 2
        g0, b0 = row(ln_g[layer, 0]), row(ln_b[layer, 0])
        g1, b1 = row(ln_g[layer, 1]), row(ln_b[layer, 1])
        if layer % 2 == 0:
            q, k, vt, kmean, *stream = _qkv_projection(h, w_qkv, j, head_dim ** -0.5 * LOG2_E,
                                                       batch, seq, time_ordered_input=layer == 0)
            if stream:
                h, = stream
            o = _moba_attention(q, k, vt, kmean.reshape(batch, n_blk, d), slopes, batch, seq)
            mixer_projection = (o, w_o, j, g0, b0)
        else:
            h = _lru_layer(h, batch, seq, j, w_lru_in, lru_conv_w[j], row(lru_conv_b[j]),
                           w_gate, row(lru_b_a[j]), row(lru_b_x[j]), row(lru_lambda[j]),
                           w_lru_out, g0, b0, alpha)
            mixer_projection = None
        h = _ffn_layer(h, w_ffn_in, w_ffn_out, layer, g1, b1, alpha, mixer_projection,
                       time_ordered_output=layer == depth - 1)
    return h.reshape(batch, seq, d)
```
